```python
import jax, jax.numpy as jnp
from jax import lax
import numpy as np

D_MODEL = 1024
BATCH = 4
SEQ = 4096
DEPTH = 4
DEC_BATCH = 16
DEC_SEQ = 2048
PAST_LEN = 128

PLE_DIM = 256
N_EVEN = (DEPTH + 1) // 2
N_ODD = DEPTH // 2
EPS = 1e-6
A_HEADS = 4
A_HD = 128
A_W = A_HEADS * A_HD
B_GROUPS = ((128, 1), (512, 4), (2048, 16))
N_B_GROUPS = 3
B_HEADS = 4
B_HD = 128
B_W = B_HEADS * B_HD
B_BLOCK = 64
ALIBI_MAX_EXP = 8.0
EV_IN = 5 * A_W + N_B_GROUPS * 3 * B_W
EV_OUT = A_W + B_W
C_HEADS = 4
C_KD = D_MODEL // 2
C_VD = D_MODEL
C_HDK = C_KD // C_HEADS
C_HDV = C_VD // C_HEADS
C_RANK = 16
GATE_NORMALIZER = 16.0
OD_IN = 2 * C_KD + 2 * C_VD + 2 * C_RANK
CHUNK = 64
SUB = 16
LOG_DECAY_MIN = -30.0
NEG_INF = -1e30
D_FF = 2816
CONV_W = 3

kernel_name = 'hybrid_bidir_hgrn2_dilattn_gla_encoder'


def _rmsnorm(x, g):
    x32 = x.astype(jnp.float32)
    y = x32 * lax.rsqrt(jnp.mean(x32 * x32, axis=-1, keepdims=True) + EPS)
    return y.astype(x.dtype) * g


def _head_rmsnorm(o, g):
    y = o * lax.rsqrt(jnp.mean(o * o, axis=-1, keepdims=True) + EPS)
    return y.reshape(o.shape[0], o.shape[1], -1) * g


def _chunk_gla(q, k, v, g):
    bsz, T, H, dk = q.shape
    dv = v.shape[-1]
    nc, ns = T // CHUNK, CHUNK // SUB
    f32 = jnp.float32

    def to_chunks(t):
        return t.astype(f32).reshape(bsz, nc, CHUNK, H, t.shape[-1]).transpose(1, 0, 3, 2, 4)

    g = jnp.maximum(g.astype(f32), LOG_DECAY_MIN)
    xs = (to_chunks(q), to_chunks(k), to_chunks(v), to_chunks(g))
    tri = jnp.tril(jnp.ones((SUB, SUB), dtype=bool))
    lower = jnp.tril(jnp.ones((ns, ns), dtype=bool), -1)

    def step(S, inp):
        qc, kc, vc, gc = inp
        b = jnp.cumsum(gc, axis=2)
        b_last = b[:, :, -1]
        o = jnp.einsum('bhcd,bhde->bhce', qc * jnp.exp(b), S)
        qs = qc.reshape(bsz, H, ns, SUB, dk)
        ks = kc.reshape(bsz, H, ns, SUB, dk)
        vs = vc.reshape(bsz, H, ns, SUB, dv)
        bs = b.reshape(bsz, H, ns, SUB, dk)
        r = bs[:, :, :, -1]
        q_off = qs[:, :, :, None] * jnp.exp(jnp.minimum(bs[:, :, :, None] - r[:, :, None, :, None, :], 0.0))
        k_off = ks * jnp.exp(r[:, :, :, None, :] - bs)
        a_off = jnp.where(lower[:, :, None, None],
                          jnp.einsum('bhijtd,bhjsd->bhijts', q_off, k_off), 0.0)
        w = jnp.where(tri[:, :, None],
                      jnp.exp(jnp.minimum(bs[:, :, :, :, None] - bs[:, :, :, None], 0.0)), 0.0)
        a_diag = jnp.einsum('bhitd,bhisd,bhitsd->bhits', qs, ks, w)
        o_intra = (jnp.einsum('bhijts,bhjse->bhite', a_off, vs)
                   + jnp.einsum('bhits,bhise->bhite', a_diag, vs))
        o = o + o_intra.reshape(bsz, H, CHUNK, dv)
        S = (jnp.exp(b_last)[..., None] * S
             + jnp.einsum('bhcd,bhce->bhde', kc * jnp.exp(b_last[:, :, None] - b), vc))
        return S, o

    S0 = jnp.zeros((bsz, H, dk, dv), f32)
    _, o = lax.scan(step, S0, xs)
    return o.transpose(1, 0, 3, 2, 4).reshape(bsz, T, H, dv)


def _bidir_gla(q, k_f, g_f, k_b, g_b, v):
    flip = lambda t: t[:, ::-1]
    o_f = _chunk_gla(q, k_f, v, g_f)
    o_b = flip(_chunk_gla(flip(q), flip(k_b), flip(v), flip(g_b)))
    return o_f + o_b


def _dilated_window_attn(q, k, v, dil, radius, slopes):
    bsz, T, H, dh = q.shape
    f32 = jnp.float32
    L = T // dil
    nb = -(-L // B_BLOCK)
    Lp = nb * B_BLOCK

    def split(t):
        return t.astype(f32).reshape(bsz, L, dil, H, dh).transpose(0, 2, 3, 1, 4)

    qb = jnp.pad(split(q) * (dh ** -0.5), ((0, 0), (0, 0), (0, 0), (0, Lp - L), (0, 0)))
    qb = qb.reshape(bsz, dil, H, nb, B_BLOCK, dh)
    pad_kv = ((0, 0), (0, 0), (0, 0), (B_BLOCK, Lp - L + B_BLOCK), (0, 0))
    kb = jnp.pad(split(k), pad_kv).reshape(bsz, dil, H, nb + 2, B_BLOCK, dh)
    vb = jnp.pad(split(v), pad_kv).reshape(bsz, dil, H, nb + 2, B_BLOCK, dh)
    s = jnp.concatenate([jnp.einsum('brhnad,brhncd->brhnac', qb, kb[:, :, :, i:i + nb])
                         for i in range(3)], axis=-1)
    rel = jnp.arange(3 * B_BLOCK)[None, :] - B_BLOCK - jnp.arange(B_BLOCK)[:, None]
    key_idx = jnp.arange(nb)[:, None] * B_BLOCK - B_BLOCK + jnp.arange(3 * B_BLOCK)[None, :]
    valid = (jnp.abs(rel) <= radius)[None] & ((key_idx >= 0) & (key_idx < L))[:, None, :]
    bias = -slopes.astype(f32)[:, None, None, None] * (dil * jnp.abs(rel)).astype(f32)
    s = jnp.where(valid, s + bias, NEG_INF)
    m = jnp.max(s, axis=-1, keepdims=True)
    pr = jnp.exp(s - m)
    den = jnp.sum(pr, axis=-1)
    o = jnp.einsum('brhnac,brhncd->brhnad', pr[..., :B_BLOCK], vb[:, :, :, 0:nb])
    for i in range(1, 3):
        o = o + jnp.einsum('brhnac,brhncd->brhnad', pr[..., i * B_BLOCK:(i + 1) * B_BLOCK],
                           vb[:, :, :, i:i + nb])
    o = o / den[..., None]
    lse = m[..., 0] + jnp.log(den)
    o = o.reshape(bsz, dil, H, Lp, dh)[:, :, :, :L].transpose(0, 3, 1, 2, 4).reshape(bsz, T, H, dh)
    lse = lse.reshape(bsz, dil, H, Lp)[..., :L].transpose(0, 3, 1, 2).reshape(bsz, T, H)
    return o, lse


def _alibi_slopes():
    n = N_B_GROUPS * B_HEADS
    e = jnp.arange(1, n + 1, dtype=jnp.float32)
    return (2.0 ** (-ALIBI_MAX_EXP * e / n)).reshape(N_B_GROUPS, B_HEADS)


def _hgrn_gates(z, lb):
    z = z.astype(jnp.float32)
    lb = lb.astype(jnp.float32).reshape(A_HEADS, A_HD)
    log_f = jnp.logaddexp(jnp.log(lb), jnp.log1p(-lb) + jax.nn.log_sigmoid(z))
    key = (1.0 - lb) * jax.nn.sigmoid(-z)
    return key, log_f


def _even_mixer(xn, w_in, lb_f, lb_b, a_norm_g, w_out):
    bsz, T, _ = xn.shape
    u = xn @ w_in
    a_q, a_zf, a_zb, a_i, a_g, b_qkv = jnp.split(u, [A_W, 2 * A_W, 3 * A_W, 4 * A_W, 5 * A_W], axis=-1)
    hd = lambda t: t.reshape(bsz, T, A_HEADS, A_HD)
    k_f, g_f = _hgrn_gates(hd(a_zf), lb_f)
    k_b, g_b = _hgrn_gates(hd(a_zb), lb_b)
    o_a = _bidir_gla(hd(a_q), k_f, g_f, k_b, g_b, hd(a_i))
    out_a = _head_rmsnorm(o_a, a_norm_g) * jax.nn.silu(a_g.astype(jnp.float32))
    b = b_qkv.reshape(bsz, T, N_B_GROUPS, 3, B_HEADS, B_HD)
    slopes = _alibi_slopes()
    outs, lses = [], []
    for gi, (win, dil) in enumerate(B_GROUPS):
        o_g, lse_g = _dilated_window_attn(b[:, :, gi, 0], b[:, :, gi, 1], b[:, :, gi, 2],
                                          dil, win // (2 * dil), slopes[gi])
        outs.append(o_g)
        lses.append(lse_g)
    alpha = jax.nn.softmax(jnp.stack(lses, axis=0), axis=0)
    out_b = jnp.sum(alpha[..., None] * jnp.stack(outs, axis=0), axis=0).reshape(bsz, T, B_W)
    mixed = jnp.concatenate([out_a, out_b], axis=-1).astype(xn.dtype)
    return mixed @ w_out


def _odd_mixer(xn, w_in, w_gate_up, b_gate, norm_g, w_out):
    bsz, T, _ = xn.shape
    u = xn @ w_in
    q, k, v, g, lr_f, lr_b = jnp.split(
        u, [C_KD, 2 * C_KD, 2 * C_KD + C_VD, 2 * C_KD + 2 * C_VD, 2 * C_KD + 2 * C_VD + C_RANK], axis=-1)
    hk = lambda t: t.reshape(bsz, T, C_HEADS, C_HDK)

    def decay(lr, d):
        logit = (lr @ w_gate_up[d] + b_gate[d]).astype(jnp.float32)
        return hk(jax.nn.log_sigmoid(logit) / GATE_NORMALIZER)

    qh = hk(q.astype(jnp.float32) * (C_HDK ** -0.5))
    kh = hk(k)
    vh = v.reshape(bsz, T, C_HEADS, C_HDV)
    o = _bidir_gla(qh, kh, decay(lr_f, 0), kh, decay(lr_b, 1), vh)
    out = _head_rmsnorm(o, norm_g) * jax.nn.silu(g.astype(jnp.float32))
    return out.astype(xn.dtype) @ w_out


def _conv_ffn(x, w_up, conv_w, conv_b, w_down):
    u = x @ w_up
    up = jnp.pad(u, ((0, 0), (1, 1), (0, 0)))
    u = up[:, :-2] * conv_w[0] + up[:, 1:-1] * conv_w[1] + up[:, 2:] * conv_w[2] + conv_b
    a, gate = jnp.split(u, 2, axis=-1)
    return (a * jax.nn.gelu(gate, approximate=False)) @ w_down


def _trunk(x, p, norm_mix_g, ev_w_in, hgrn_lb, hgrn_norm_g, ev_w_out, od_w_in, gla_w_gate_up,
           gla_b_gate, gla_norm_g, od_w_out, norm_ffn_g, ffn_w_up, ffn_conv_w, ffn_conv_b, ffn_w_down,
           norm_ple_g, ple_w_gate, ple_w_proj, norm_out_g):
    h = x
    for l in range(DEPTH):
        xn = _rmsnorm(h, norm_mix_g[l])
        if l % 2 == 0:
            e = l // 2
            h = h + _even_mixer(xn, ev_w_in[e], hgrn_lb[0, e], hgrn_lb[1, e], hgrn_norm_g[e], ev_w_out[e])
        else:
            o = l // 2
            h = h + _odd_mixer(xn, od_w_in[o], gla_w_gate_up[o], gla_b_gate[o], gla_norm_g[o], od_w_out[o])
        h = h + _conv_ffn(_rmsnorm(h, norm_ffn_g[l]), ffn_w_up[l], ffn_conv_w[l], ffn_conv_b[l], ffn_w_down[l])
        gate = jax.nn.sigmoid(_rmsnorm(h, norm_ple_g[l]) @ ple_w_gate[l])
        h = h + gate * (p[l] @ ple_w_proj[l])
    return _rmsnorm(h, norm_out_g)


def setup_inputs(seed: int = 0) -> dict:
    key = jax.random.key(seed)
    ks = iter(jax.random.split(key, 40))
    f32 = jnp.float32
    nrm = lambda shape, scale: jax.random.normal(next(ks), shape, f32) * scale
    gain = lambda shape: 1.0 + 0.05 * jax.random.normal(next(ks), shape, f32)
    centre = jnp.array([0.0, 1.0, 0.0], f32)[None, :, None]
    return {
        'x_prompt': nrm((BATCH, SEQ, D_MODEL), 1.0),
        'x_sample': nrm((DEC_BATCH, DEC_SEQ, D_MODEL), 1.0),
        'p_prompt': nrm((DEPTH, BATCH, SEQ, PLE_DIM), 1.0),
        'p_sample': nrm((DEPTH, DEC_BATCH, DEC_SEQ, PLE_DIM), 1.0),
        'norm_mix_g': gain((DEPTH, D_MODEL)),
        'ev_w_in': nrm((N_EVEN, D_MODEL, EV_IN), D_MODEL ** -0.5),
        'hgrn_lb_logits': nrm((2, N_EVEN, A_W), 1.0),
        'hgrn_norm_g': gain((N_EVEN, A_W)),
        'ev_w_out': nrm((N_EVEN, EV_OUT, D_MODEL), EV_OUT ** -0.5),
        'od_w_in': nrm((N_ODD, D_MODEL, OD_IN), D_MODEL ** -0.5),
        'gla_w_gate_up': nrm((N_ODD, 2, C_RANK, C_KD), C_RANK ** -0.5),
        'gla_b_gate': nrm((N_ODD, 2, C_KD), 0.1),
        'gla_norm_g': gain((N_ODD, C_VD)),
        'od_w_out': nrm((N_ODD, C_VD, D_MODEL), C_VD ** -0.5),
        'norm_ffn_g': gain((DEPTH, D_MODEL)),
        'ffn_w_up': nrm((DEPTH, D_MODEL, 2 * D_FF), D_MODEL ** -0.5),
        'ffn_conv_w': centre + nrm((DEPTH, CONV_W, 2 * D_FF), 0.3),
        'ffn_conv_b': nrm((DEPTH, 2 * D_FF), 0.02),
        'ffn_w_down': nrm((DEPTH, D_FF, D_MODEL), D_FF ** -0.5),
        'norm_ple_g': gain((DEPTH, D_MODEL)),
        'ple_w_gate': nrm((DEPTH, D_MODEL, D_MODEL), D_MODEL ** -0.5),
        'ple_w_proj': nrm((DEPTH, PLE_DIM, D_MODEL), PLE_DIM ** -0.5),
        'norm_out_g': gain((D_MODEL,)),
    }


def reference(x_prompt, x_sample, p_prompt, p_sample, norm_mix_g, ev_w_in, hgrn_lb_logits, hgrn_norm_g,
              ev_w_out, od_w_in, gla_w_gate_up, gla_b_gate, gla_norm_g, od_w_out, norm_ffn_g, ffn_w_up,
              ffn_conv_w, ffn_conv_b, ffn_w_down, norm_ple_g, ple_w_gate, ple_w_proj, norm_out_g):
    lb = jnp.cumsum(jax.nn.softmax(hgrn_lb_logits.astype(jnp.float32), axis=1), axis=1)
    lb = lb - lb[:, :1]
    y_prompt = _trunk(x_prompt, p_prompt, norm_mix_g, ev_w_in, lb, hgrn_norm_g, ev_w_out, od_w_in,
                      gla_w_gate_up, gla_b_gate, gla_norm_g, od_w_out, norm_ffn_g, ffn_w_up, ffn_conv_w,
                      ffn_conv_b, ffn_w_down, norm_ple_g, ple_w_gate, ple_w_proj, norm_out_g)
    y_sample = _trunk(x_sample, p_sample, norm_mix_g, ev_w_in, lb, hgrn_norm_g, ev_w_out, od_w_in,
                      gla_w_gate_up, gla_b_gate, gla_norm_g, od_w_out, norm_ffn_g, ffn_w_up, ffn_conv_w,
                      ffn_conv_b, ffn_w_down, norm_ple_g, ple_w_gate, ple_w_proj, norm_out_g)
    return (y_prompt, y_sample)
```

```python
import functools

import numpy as np
import jax
import jax.numpy as jnp
from jax import lax
from jax.experimental import pallas as pl
from jax.experimental.pallas import tpu as pltpu

F32 = jnp.float32
BF16 = jnp.bfloat16

D_MODEL = 1024
DEPTH = 4
PLE_DIM = 256
EPS = 1e-6
HEADS = 4
HD = 128
A_W = HEADS * HD
B_GROUPS = ((128, 1), (512, 4), (2048, 16))
N_GROUPS = len(B_GROUPS)
ATT_BLOCK = 64
ALIBI_MAX_EXP = 8.0
C_KD = 512
C_VD = 1024
C_HDV = C_VD // HEADS
C_RANK = 16
GATE_NORMALIZER = 16.0
CHUNK = 64
SUB = 16
N_SUB = CHUNK // SUB
LOG_DECAY_MIN = -30.0
NEG_INF = -1e30
D_FF = 2816
LSE_REP = 32
LR_PAD = 128

ROW_TILE = 512
HALO = 8
FF_CHUNK = 256
VMEM_LIMIT = 56 * 1024 * 1024


def _cparams(*sem):
    return pltpu.CompilerParams(dimension_semantics=sem, vmem_limit_bytes=VMEM_LIMIT)


def _resident(shape):
    nd = len(shape)
    return pl.BlockSpec(shape, lambda *_: (0,) * nd, pipeline_mode=pl.Buffered(1))


def _rms(x, g):
    ms = jnp.mean(x * x, axis=-1, keepdims=True)
    return x * lax.rsqrt(ms + EPS) * g


def _dot(a, b):
    return jnp.dot(a, b, preferred_element_type=F32)


def _dot_nt(a, b):
    return lax.dot_general(a, b, (((1,), (1,)), ((), ())), preferred_element_type=F32)


def _dot_tn(a, b):
    return lax.dot_general(a, b, (((0,), (0,)), ((), ())), preferred_element_type=F32)


def _split3(x):
    x1 = x.astype(BF16)
    r1 = x - x1.astype(F32)
    x2 = r1.astype(BF16)
    x3 = (r1 - x2.astype(F32)).astype(BF16)
    return x1, x2, x3


def _store_cols(xn, w_ref, out_refs, col_chunk=512):
    c0 = 0
    for o_ref in out_refs:
        n = o_ref.shape[-1]
        for j in range(0, n, col_chunk):
            wj = min(col_chunk, n - j)
            o_ref[:, j:j + wj] = _dot(xn, w_ref[:, c0 + j:c0 + j + wj]).astype(o_ref.dtype)
        c0 += n
    return c0


def _in_proj_even_body(x_ref, g_ref, w_ref, a16_ref, a32_ref, b16_ref):
    xn = _rms(x_ref[...], g_ref[...]).astype(BF16)
    _store_cols(xn, w_ref, (a16_ref, a32_ref, b16_ref))


def _in_proj_odd_body(x_ref, g_ref, w_ref, w2_ref, b2_ref, c16_ref, lg_ref):
    xn = _rms(x_ref[...], g_ref[...]).astype(BF16)
    c0 = _store_cols(xn, w_ref, (c16_ref,))
    lr = _dot(xn, w_ref[:, c0:c0 + LR_PAD]).astype(BF16)
    n = lg_ref.shape[-1]
    for j in range(0, n, 512):
        lg_ref[:, j:j + 512] = _dot(lr, w2_ref[:, j:j + 512]) + b2_ref[:, j:j + 512]


def _in_proj_even(x, g, w):
    m = x.shape[0]
    widths = ((3 * A_W, BF16), (2 * A_W, F32), (N_GROUPS * 3 * A_W, BF16))
    return pl.pallas_call(
        _in_proj_even_body,
        grid=(m // ROW_TILE,),
        in_specs=[pl.BlockSpec((ROW_TILE, D_MODEL), lambda i: (i, 0)),
                  pl.BlockSpec((1, D_MODEL), lambda i: (0, 0)),
                  _resident(w.shape)],
        out_specs=[pl.BlockSpec((ROW_TILE, n), lambda i: (i, 0)) for n, _ in widths],
        out_shape=[jax.ShapeDtypeStruct((m, n), dt) for n, dt in widths],
        compiler_params=_cparams("parallel"),
        name="in_proj_even",
    )(x, g, w)


def _in_proj_odd(x, g, w, w2, b2):
    m = x.shape[0]
    n16 = 2 * C_KD + 2 * C_VD
    return pl.pallas_call(
        _in_proj_odd_body,
        grid=(m // ROW_TILE,),
        in_specs=[pl.BlockSpec((ROW_TILE, D_MODEL), lambda i: (i, 0)),
                  pl.BlockSpec((1, D_MODEL), lambda i: (0, 0)),
                  _resident(w.shape), _resident(w2.shape),
                  pl.BlockSpec((1, 2 * C_KD), lambda i: (0, 0))],
        out_specs=[pl.BlockSpec((ROW_TILE, n16), lambda i: (i, 0)),
                   pl.BlockSpec((ROW_TILE, 2 * C_KD), lambda i: (i, 0))],
        out_shape=[jax.ShapeDtypeStruct((m, n16), BF16), jax.ShapeDtypeStruct((m, 2 * C_KD), F32)],
        compiler_params=_cparams("parallel"),
        name="in_proj_odd",
    )(x, g, w, w2, b2)


def _gla_constants():
    t = np.arange(CHUNK)
    tri_f = (t[None, :] <= t[:, None]).astype(np.float32)
    tri_b = (t[None, :] >= t[:, None]).astype(np.float32)
    e = np.zeros((SUB * HD, CHUNK), np.float32)
    for s in range(SUB):
        e[s * HD:(s + 1) * HD, s::SUB] = 1.0
    return jnp.asarray(tri_f, BF16), jnp.asarray(tri_b, BF16), jnp.asarray(e, BF16)


def _bcast_rows(x, rows):
    return jnp.concatenate([jnp.broadcast_to(x[r:r + 1, :], (SUB, x.shape[1])) for r in rows], axis=0)


def _gla_chunk(qs, k, v, g, st_ref, tri, emat, rev):
    g1, g2, g3 = _split3(g)
    b = _dot(tri, g1) + _dot(tri, g2) + _dot(tri, g3)
    edge = 0 if rev else CHUNK - 1
    b_tot = b[edge:edge + 1, :]

    st = st_ref[...]
    o = _dot_nt((qs * jnp.exp(b)).astype(BF16), st.astype(BF16))
    kd = (k * jnp.exp(b_tot - b)).astype(BF16)
    st_ref[...] = st * jnp.exp(b_tot) + _dot_tn(v, kd)

    sub_edge = [SUB * i + (0 if rev else SUB - 1) for i in range(N_SUB)]
    k_off = k * jnp.exp(_bcast_rows(b, sub_edge) - b)
    zq = jnp.zeros((SUB, HD), F32)
    q_parts, k_parts = [], []
    for j in range(N_SUB):
        lo, hi = (0, SUB * j) if rev else (SUB * (j + 1), CHUNK)
        if hi <= lo:
            continue
        r_j = b[sub_edge[j]:sub_edge[j] + 1, :]
        q_off = qs[lo:hi] * jnp.exp(jnp.minimum(b[lo:hi] - r_j, 0.0))
        q_parts.append(jnp.concatenate([zq] * (lo // SUB) + [q_off] + [zq] * ((CHUNK - hi) // SUB), axis=0))
        k_parts.append(jnp.concatenate([zq] * j + [k_off[SUB * j:SUB * (j + 1)]] + [zq] * (N_SUB - 1 - j),
                                       axis=0))
    a_off = _dot_nt(jnp.concatenate(q_parts, axis=1).astype(BF16),
                    jnp.concatenate(k_parts, axis=1).astype(BF16))

    t_loc = lax.broadcasted_iota(jnp.int32, (CHUNK, HD), 0) % SUB
    slabs = []
    for s in range(SUB):
        rows = [SUB * i + s for i in range(N_SUB)]
        causal = (t_loc <= s) if rev else (t_loc >= s)
        w = jnp.exp(jnp.where(causal, b - _bcast_rows(b, rows), NEG_INF))
        slabs.append((qs * _bcast_rows(k, rows) * w).astype(BF16))
    a_dg = _dot(jnp.concatenate(slabs, axis=1), emat)

    rb = lax.broadcasted_iota(jnp.int32, (CHUNK, CHUNK), 0) // SUB
    cb = lax.broadcasted_iota(jnp.int32, (CHUNK, CHUNK), 1) // SUB
    a = jnp.where(rb == cb, a_dg, a_off)
    return o + _dot(a.astype(BF16), v)


def _sigmoid_pair(z):
    e = jnp.exp(-jnp.abs(z))
    r = 1.0 / (1.0 + e)
    er = e * r
    pos = z >= 0.0
    return jnp.where(pos, r, er), jnp.where(pos, er, r)


def _log_sigmoid(x):
    return jnp.minimum(x, 0.0) - jnp.log1p(jnp.exp(-jnp.abs(x)))


def _bidir_body(q_ref, kf_ref, kb_ref, v_ref, gate_ref, lbf_ref, lbb_ref, ng_ref, trif_ref, trib_ref, e_ref,
                o_ref, accf_ref, accb_ref, sf_ref, sb_ref, *, nc, hgrn, q_scale):
    sf_ref[...] = jnp.zeros_like(sf_ref)
    sb_ref[...] = jnp.zeros_like(sb_ref)
    emat = e_ref[...]

    def step(c, carry):
        for rev in (False, True):
            cc = nc - 1 - c if rev else c
            sl = pl.ds(pl.multiple_of(cc * CHUNK, CHUNK), CHUNK)
            if hgrn:
                lb = (lbb_ref if rev else lbf_ref)[...]
                sig, nsig = _sigmoid_pair((kb_ref if rev else kf_ref)[sl, :])
                g = jnp.log(lb + (1.0 - lb) * sig)
                k = (1.0 - lb) * nsig
            else:
                g = _log_sigmoid((lbb_ref if rev else lbf_ref)[sl, :]) * (1.0 / GATE_NORMALIZER)
                k = kf_ref[sl, :].astype(F32)
            g = jnp.maximum(g, LOG_DECAY_MIN)
            qs = q_ref[sl, :].astype(F32)
            if q_scale != 1.0:
                qs = qs * q_scale
            o = _gla_chunk(qs, k, v_ref[sl, :], g, sb_ref if rev else sf_ref,
                           (trib_ref if rev else trif_ref)[...], emat, rev)
            (accb_ref if rev else accf_ref)[sl, :] = o
        return carry

    lax.fori_loop(0, nc, step, 0)

    ng = ng_ref[...]

    def finish(i, carry):
        sl = pl.ds(pl.multiple_of(i * ROW_TILE, ROW_TILE), ROW_TILE)
        o = accf_ref[sl, :] + accb_ref[sl, :]
        gt = gate_ref[sl, :].astype(F32)
        sig, _ = _sigmoid_pair(gt)
        o_ref[sl, :] = (_rms(o, ng) * (gt * sig)).astype(o_ref.dtype)
        return carry

    lax.fori_loop(0, (nc * CHUNK) // ROW_TILE, finish, 0)


def _bidir_call(name, bsz, t, dv, hgrn, q_scale, args, in_specs):
    tri_f, tri_b, emat = _gla_constants()
    const = lambda shape: pl.BlockSpec(shape, lambda b, h: (0,) * len(shape))
    return pl.pallas_call(
        functools.partial(_bidir_body, nc=t // CHUNK, hgrn=hgrn, q_scale=q_scale),
        grid=(bsz, HEADS),
        in_specs=in_specs + [const(tri_f.shape), const(tri_b.shape), const(emat.shape)],
        out_specs=pl.BlockSpec((None, t, dv), lambda b, h: (b, 0, h)),
        out_shape=jax.ShapeDtypeStruct((bsz, t, HEADS * dv), BF16),
        scratch_shapes=[pltpu.VMEM((t, dv), F32), pltpu.VMEM((t, dv), F32),
                        pltpu.VMEM((dv, HD), F32), pltpu.VMEM((dv, HD), F32)],
        compiler_params=_cparams("parallel", "parallel"),
        name=name,
    )(*args, tri_f, tri_b, emat)


def _hgrn_mixer(a16, a32, lb_f, lb_b, norm_g):
    bsz, t, _ = a16.shape
    col = lambda off: pl.BlockSpec((None, t, HD), lambda b, h: (b, 0, off + h))
    vec = pl.BlockSpec((1, HD), lambda b, h: (0, h))
    return _bidir_call("hgrn2_bidir", bsz, t, HD, True, 1.0,
                       (a16, a32, a32, a16, a16, lb_f, lb_b, norm_g),
                       [col(0), col(0), col(HEADS), col(HEADS), col(2 * HEADS), vec, vec, vec])


def _gla_mixer(c16, lg, norm_g):
    bsz, t, _ = c16.shape
    col = lambda off: pl.BlockSpec((None, t, HD), lambda b, h: (b, 0, off + h))
    wide = lambda off: pl.BlockSpec((None, t, C_HDV), lambda b, h: (b, 0, off + h))
    v_off = 2 * C_KD // C_HDV
    return _bidir_call("gla_bidir", bsz, t, C_HDV, False, HD ** -0.5,
                       (c16, c16, c16, c16, c16, lg, lg, norm_g),
                       [col(0), col(HEADS), col(HEADS), wide(v_off), wide(v_off + HEADS), col(0), col(HEADS),
                        pl.BlockSpec((1, C_HDV), lambda b, h: (0, h))])


def _attn_body(q_ref, k_ref, v_ref, o_ref, l_ref, *, length, dil, radius, slopes):
    win = min(3 * ATT_BLOCK, length)
    scale = HD ** -0.5

    def step(i, carry):
        m0 = pl.multiple_of(i * ATT_BLOCK, ATT_BLOCK)
        k0 = pl.multiple_of(jnp.clip(m0 - ATT_BLOCK, 0, length - win), ATT_BLOCK)
        rel = (k0 + lax.broadcasted_iota(jnp.int32, (ATT_BLOCK, win), 1)
               - m0 - lax.broadcasted_iota(jnp.int32, (ATT_BLOCK, win), 0))
        dist = jnp.abs(rel)
        valid = dist <= radius
        fdist = (dil * dist).astype(F32)
        for h in range(HEADS):
            cs = slice(h * HD, (h + 1) * HD)
            s = _dot_nt(q_ref[pl.ds(m0, ATT_BLOCK), cs], k_ref[pl.ds(k0, win), cs]) * scale - slopes[h] * fdist
            s = jnp.where(valid, s, NEG_INF)
            mx = jnp.max(s, axis=-1, keepdims=True)
            p = jnp.exp(s - mx)
            den = jnp.sum(p, axis=-1, keepdims=True)
            o = _dot(p.astype(BF16), v_ref[pl.ds(k0, win), cs]) * (1.0 / den)
            o_ref[pl.ds(m0, ATT_BLOCK), cs] = o.astype(o_ref.dtype)
            l_ref[pl.ds(m0, ATT_BLOCK), h * LSE_REP:(h + 1) * LSE_REP] = jnp.broadcast_to(
                mx + jnp.log(den), (ATT_BLOCK, LSE_REP))
        return carry

    lax.fori_loop(0, length // ATT_BLOCK, step, 0)


def _dilated_attention(b16, gi):
    bsz, t, width = b16.shape
    win, dil = B_GROUPS[gi]
    length = t // dil
    n = N_GROUPS * HEADS
    slopes = tuple(2.0 ** (-ALIBI_MAX_EXP * (gi * HEADS + h + 1) / n) for h in range(HEADS))
    view = b16.reshape(bsz, length, dil * width)
    per_r = width // A_W
    spec = lambda j: pl.BlockSpec((None, length, A_W), lambda b, r: (b, 0, r * per_r + 3 * gi + j))
    o, lse = pl.pallas_call(
        functools.partial(_attn_body, length=length, dil=dil, radius=win // (2 * dil), slopes=slopes),
        grid=(bsz, dil),
        in_specs=[spec(0), spec(1), spec(2)],
        out_specs=[pl.BlockSpec((None, length, A_W), lambda b, r: (b, 0, r)),
                   pl.BlockSpec((None, length, HEADS * LSE_REP), lambda b, r: (b, 0, r))],
        out_shape=[jax.ShapeDtypeStruct((bsz, length, dil * A_W), BF16),
                   jax.ShapeDtypeStruct((bsz, length, dil * HEADS * LSE_REP), F32)],
        compiler_params=_cparams("parallel", "parallel"),
        name=f"dilated_attn_g{gi}",
    )(view, view, view)
    return o.reshape(bsz * t, A_W), lse.reshape(bsz * t, HEADS * LSE_REP)


def _lse_expansion():
    e = np.zeros((HEADS * LSE_REP, A_W), np.float32)
    for h in range(HEADS):
        e[h * LSE_REP, h * HD:(h + 1) * HD] = 1.0
    return jnp.asarray(e, BF16)


def _out_even_body(h_ref, a_ref, o0_ref, o1_ref, o2_ref, l0_ref, l1_ref, l2_ref, e_ref, w_ref, y_ref):
    lses = (l0_ref[...], l1_ref[...], l2_ref[...])
    mx = jnp.maximum(jnp.maximum(lses[0], lses[1]), lses[2])
    es = [jnp.exp(l - mx) for l in lses]
    inv = 1.0 / (es[0] + es[1] + es[2])
    emat = e_ref[...]
    mixed_b = None
    for e_g, o_ref in zip(es, (o0_ref, o1_ref, o2_ref)):
        alpha = e_g * inv
        hi = alpha.astype(BF16)
        lo = (alpha - hi.astype(F32)).astype(BF16)
        term = (_dot(hi, emat) + _dot(lo, emat)) * o_ref[...].astype(F32)
        mixed_b = term if mixed_b is None else mixed_b + term
    y_ref[...] = (h_ref[...] + _dot(a_ref[...], w_ref[:A_W, :]) + _dot(mixed_b.astype(BF16), w_ref[A_W:, :]))


def _out_even(h, out_a, os_, lses, w):
    m = h.shape[0]
    emat = _lse_expansion()
    row = lambda n: pl.BlockSpec((ROW_TILE, n), lambda i: (i, 0))
    return pl.pallas_call(
        _out_even_body,
        grid=(m // ROW_TILE,),
        in_specs=[row(D_MODEL), row(A_W)] + [row(A_W)] * 3 + [row(HEADS * LSE_REP)] * 3
                 + [_resident(emat.shape), _resident(w.shape)],
        out_specs=row(D_MODEL),
        out_shape=jax.ShapeDtypeStruct((m, D_MODEL), F32),
        compiler_params=_cparams("parallel"),
        name="out_proj_even",
    )(h, out_a, *os_, *lses, emat, w)


def _out_odd_body(h_ref, a_ref, w_ref, y_ref):
    y_ref[...] = h_ref[...] + _dot(a_ref[...], w_ref[...])


def _out_odd(h, mixed, w):
    m = h.shape[0]
    row = lambda n: pl.BlockSpec((ROW_TILE, n), lambda i: (i, 0))
    return pl.pallas_call(
        _out_odd_body,
        grid=(m // ROW_TILE,),
        in_specs=[row(D_MODEL), row(C_VD), _resident(w.shape)],
        out_specs=row(D_MODEL),
        out_shape=jax.ShapeDtypeStruct((m, D_MODEL), F32),
        compiler_params=_cparams("parallel"),
        name="out_proj_odd",
    )(h, mixed, w)


def _ffn_body(hp_ref, h_ref, hn_ref, p_ref, gf_ref, wup_ref, cw_ref, cb_ref, wdn_ref, gp_ref, wg_ref, wp_ref,
              go_ref, y_ref, xe_ref, *, tiles_per_seq, final):
    i = pl.program_id(0) % tiles_per_seq
    gf = gf_ref[...]
    h = h_ref[...]
    keep_prev = jnp.where(i > 0, 1.0, 0.0).astype(F32)
    keep_next = jnp.where(i < tiles_per_seq - 1, 1.0, 0.0).astype(F32)
    xe_ref[0:HALO, :] = _rms(hp_ref[...], gf) * keep_prev
    xe_ref[HALO:HALO + ROW_TILE, :] = _rms(h, gf)
    xe_ref[HALO + ROW_TILE:, :] = _rms(hn_ref[...], gf) * keep_next
    xe = xe_ref[...].astype(BF16)
    rows = ROW_TILE + 2 * HALO

    def conv(u, c):
        w = cw_ref[:, c:c + FF_CHUNK]
        prev = pltpu.roll(u, 1, 0)[HALO:HALO + ROW_TILE]
        nxt = pltpu.roll(u, rows - 1, 0)[HALO:HALO + ROW_TILE]
        return (prev * w[0:1] + u[HALO:HALO + ROW_TILE] * w[1:2] + nxt * w[2:3] + cb_ref[:, c:c + FF_CHUNK])

    acc = jnp.zeros((ROW_TILE, D_MODEL), F32)
    for c in range(0, D_FF, FF_CHUNK):
        a = conv(_dot(xe, wup_ref[:, c:c + FF_CHUNK]), c)
        gt = conv(_dot(xe, wup_ref[:, D_FF + c:D_FF + c + FF_CHUNK]), D_FF + c)
        act = a * (0.5 * gt * (1.0 + lax.erf(gt * (2.0 ** -0.5))))
        acc = acc + _dot(act.astype(BF16), wdn_ref[c:c + FF_CHUNK, :])
    h2 = h + acc
    sig, _ = _sigmoid_pair(_dot(_rms(h2, gp_ref[...]).astype(BF16), wg_ref[...]))
    h3 = h2 + sig * _dot(p_ref[...].astype(BF16), wp_ref[...])
    y_ref[...] = _rms(h3, go_ref[...]) if final else h3


def _ffn_ple(h, p, t, gf, wup, cw, cb, wdn, gp, wg, wp, go, final):
    m = h.shape[0]
    per_tile = ROW_TILE // HALO
    last_blk = m // HALO - 1
    vec = lambda n: pl.BlockSpec((1, n), lambda i: (0, 0))
    return pl.pallas_call(
        functools.partial(_ffn_body, tiles_per_seq=t // ROW_TILE, final=final),
        grid=(m // ROW_TILE,),
        in_specs=[pl.BlockSpec((HALO, D_MODEL), lambda i: (jnp.maximum(i * per_tile - 1, 0), 0)),
                  pl.BlockSpec((ROW_TILE, D_MODEL), lambda i: (i, 0)),
                  pl.BlockSpec((HALO, D_MODEL), lambda i: (jnp.minimum((i + 1) * per_tile, last_blk), 0)),
                  pl.BlockSpec((ROW_TILE, PLE_DIM), lambda i: (i, 0)),
                  vec(D_MODEL), _resident(wup.shape), _resident(cw.shape), vec(2 * D_FF), _resident(wdn.shape),
                  vec(D_MODEL), _resident(wg.shape), _resident(wp.shape), vec(D_MODEL)],
        out_specs=pl.BlockSpec((ROW_TILE, D_MODEL), lambda i: (i, 0)),
        out_shape=jax.ShapeDtypeStruct((m, D_MODEL), F32),
        scratch_shapes=[pltpu.VMEM((ROW_TILE + 2 * HALO, D_MODEL), F32)],
        compiler_params=_cparams("parallel"),
        name="conv_ffn_ple",
    )(h, h, h, p, gf, wup, cw, cb, wdn, gp, wg, wp, go)


def _prep_weights(ev_w_in, od_w_in, gla_w_gate_up, gla_b_gate):
    aq, zf, zb, ai, ag, bqkv = jnp.split(ev_w_in, [A_W, 2 * A_W, 3 * A_W, 4 * A_W, 5 * A_W], axis=-1)
    ev_w = jnp.concatenate([aq, ai, ag, zf, zb, bqkv], axis=-1).astype(BF16)
    n16 = 2 * C_KD + 2 * C_VD
    od_w = jnp.pad(od_w_in, ((0, 0), (0, 0), (0, LR_PAD - 2 * C_RANK))).astype(BF16)
    n_odd = od_w_in.shape[0]
    w2 = jnp.zeros((n_odd, LR_PAD, 2 * C_KD), F32)
    w2 = w2.at[:, :C_RANK, :C_KD].set(gla_w_gate_up[:, 0]).at[:, C_RANK:2 * C_RANK, C_KD:].set(gla_w_gate_up[:, 1])
    b2 = gla_b_gate.reshape(n_odd, 1, 2 * C_KD)
    del n16
    return ev_w, od_w, w2.astype(BF16), b2


def _trunk(x, p, prm):
    bsz, t, _ = x.shape
    m = bsz * t
    h = x.reshape(m, D_MODEL)
    for l in range(DEPTH):
        g_mix = prm["norm_mix_g"][l][None]
        if l % 2 == 0:
            e = l // 2
            a16, a32, b16 = _in_proj_even(h, g_mix, prm["ev_w"][e])
            out_a = _hgrn_mixer(a16.reshape(bsz, t, -1), a32.reshape(bsz, t, -1),
                                prm["lb"][0, e][None], prm["lb"][1, e][None], prm["hgrn_norm_g"][e][None])
            b16 = b16.reshape(bsz, t, -1)
            att = [_dilated_attention(b16, gi) for gi in range(N_GROUPS)]
            h = _out_even(h, out_a.reshape(m, A_W), [a[0] for a in att], [a[1] for a in att], prm["ev_w_out"][e])
        else:
            o = l // 2
            c16, lg = _in_proj_odd(h, g_mix, prm["od_w"][o], prm["od_w2"][o], prm["od_b2"][o])
            mixed = _gla_mixer(c16.reshape(bsz, t, -1), lg.reshape(bsz, t, -1), prm["gla_norm_g"][o][None])
            h = _out_odd(h, mixed.reshape(m, C_VD), prm["od_w_out"][o])
        h = _ffn_ple(h, p[l].reshape(m, PLE_DIM), t, prm["norm_ffn_g"][l][None], prm["ffn_w_up"][l],
                     prm["ffn_conv_w"][l], prm["ffn_conv_b"][l][None], prm["ffn_w_down"][l],
                     prm["norm_ple_g"][l][None], prm["ple_w_gate"][l], prm["ple_w_proj"][l],
                     prm["norm_out_g"][None], final=(l == DEPTH - 1))
    return h.reshape(bsz, t, D_MODEL)


def kernel(x_prompt, x_sample, p_prompt, p_sample, norm_mix_g, ev_w_in, hgrn_lb_logits, hgrn_norm_g, ev_w_out, od_w_in, gla_w_gate_up, gla_b_gate, gla_norm_g, od_w_out, norm_ffn_g, ffn_w_up, ffn_conv_w, ffn_conv_b, ffn_w_down, norm_ple_g, ple_w_gate, ple_w_proj, norm_out_g):
    lb = jnp.cumsum(jax.nn.softmax(hgrn_lb_logits.astype(F32), axis=1), axis=1)
    lb = lb - lb[:, :1]
    ev_w, od_w, od_w2, od_b2 = _prep_weights(ev_w_in, od_w_in, gla_w_gate_up, gla_b_gate)
    prm = dict(norm_mix_g=norm_mix_g, ev_w=ev_w, lb=lb, hgrn_norm_g=hgrn_norm_g, ev_w_out=ev_w_out.astype(BF16),
               od_w=od_w, od_w2=od_w2, od_b2=od_b2, gla_norm_g=gla_norm_g, od_w_out=od_w_out.astype(BF16),
               norm_ffn_g=norm_ffn_g, ffn_w_up=ffn_w_up.astype(BF16), ffn_conv_w=ffn_conv_w,
               ffn_conv_b=ffn_conv_b, ffn_w_down=ffn_w_down.astype(BF16), norm_ple_g=norm_ple_g,
               ple_w_gate=ple_w_gate.astype(BF16), ple_w_proj=ple_w_proj.astype(BF16), norm_out_g=norm_out_g)
    return _trunk(x_prompt, p_prompt, prm), _trunk(x_sample, p_sample, prm)
```

```python
import functools

import numpy as np
import jax
import jax.numpy as jnp
from jax import lax
from jax.experimental import pallas as pl
from jax.experimental.pallas import tpu as pltpu

F32 = jnp.float32
BF16 = jnp.bfloat16

D_MODEL = 1024
DEPTH = 4
PLE_DIM = 256
EPS = 1e-6
HEADS = 4
HD = 128
A_W = HEADS * HD
B_GROUPS = ((128, 1), (512, 4), (2048, 16))
N_GROUPS = len(B_GROUPS)
QKV_W = 3 * A_W
ATT_Q = 128
ALIBI_MAX_EXP = 8.0
C_KD = 512
C_VD = 1024
C_HDV = C_VD // HEADS
C_RANK = 16
GATE_NORMALIZER = 16.0
CHUNK = 64
SUB = 16
N_SUB = CHUNK // SUB
SUB8 = 8
CHUNKS_PER_STEP = 2
LOG_DECAY_MIN = -30.0
LOG2E = 1.4426950408889634
NEG_INF = -1e30
D_FF = 2816
LSE_W = 128
LSE_REP = LSE_W // HEADS
LR_PAD = 128

ROW_TILE = 512
HALO = 8
FF_CHUNK = 256
VMEM_LIMIT = 56 * 1024 * 1024


def _cparams(*sem):
    return pltpu.CompilerParams(dimension_semantics=sem, vmem_limit_bytes=VMEM_LIMIT)


def _resident(shape):
    nd = len(shape)
    return pl.BlockSpec(shape, lambda *_: (0,) * nd, pipeline_mode=pl.Buffered(1))


def _rms(x, g):
    ms = jnp.mean(x * x, axis=-1, keepdims=True)
    return x * lax.rsqrt(ms + EPS) * g


def _dot(a, b):
    return jnp.dot(a, b, preferred_element_type=F32)


def _dot_nt(a, b):
    return lax.dot_general(a, b, (((1,), (1,)), ((), ())), preferred_element_type=F32)


def _dot_tn(a, b):
    return lax.dot_general(a, b, (((0,), (0,)), ((), ())), preferred_element_type=F32)


def _store_cols(xn, w_ref, out_refs, c0=0, col_chunk=512):
    for o_ref in out_refs:
        n = o_ref.shape[-1]
        for j in range(0, n, col_chunk):
            wj = min(col_chunk, n - j)
            o_ref[:, j:j + wj] = _dot(xn, w_ref[:, c0 + j:c0 + j + wj]).astype(o_ref.dtype)
        c0 += n
    return c0


def _in_proj_even_body(x_ref, g_ref, w_ref, a16_ref, a32_ref, g0_ref, g1_ref, g2_ref, u_ref):
    xn = _rms(x_ref[...], g_ref[...]).astype(BF16)
    c0 = _store_cols(xn, w_ref, (a16_ref, a32_ref))
    for (_, dil), o_ref in zip(B_GROUPS, (g0_ref, g1_ref, g2_ref)):
        n = ROW_TILE // dil
        for j in range(3):
            u = _dot(xn, w_ref[:, c0:c0 + A_W])
            c0 += A_W
            if dil == 1:
                o_ref[:, j * A_W:(j + 1) * A_W] = u.astype(BF16)
            else:
                for c in range(HEADS):
                    u_ref[c] = u[:, c * HD:(c + 1) * HD]
                for r in range(dil):
                    for c in range(HEADS):
                        col = (3 * r + j) * A_W + c * HD
                        o_ref[:, col:col + HD] = u_ref[c, pl.ds(r, n, stride=dil), :].astype(BF16)


def _in_proj_odd_body(x_ref, g_ref, w_ref, w2_ref, b2_ref, c16_ref, lg_ref):
    xn = _rms(x_ref[...], g_ref[...]).astype(BF16)
    c0 = _store_cols(xn, w_ref, (c16_ref,))
    lr = _dot(xn, w_ref[:, c0:c0 + LR_PAD]).astype(BF16)
    n = lg_ref.shape[-1]
    for j in range(0, n, 512):
        lg_ref[:, j:j + 512] = _dot(lr, w2_ref[:, j:j + 512]) + b2_ref[:, j:j + 512]


def _in_proj_even(x, g, w):
    m = x.shape[0]
    outs = [((m, 3 * A_W), (ROW_TILE, 3 * A_W), BF16), ((m, 2 * A_W), (ROW_TILE, 2 * A_W), F32)]
    for _, dil in B_GROUPS:
        outs.append(((m // dil, dil * QKV_W), (ROW_TILE // dil, dil * QKV_W), BF16))
    return pl.pallas_call(
        _in_proj_even_body,
        grid=(m // ROW_TILE,),
        in_specs=[pl.BlockSpec((ROW_TILE, D_MODEL), lambda i: (i, 0)),
                  pl.BlockSpec((1, D_MODEL), lambda i: (0, 0)),
                  _resident(w.shape)],
        out_specs=[pl.BlockSpec(blk, lambda i: (i, 0)) for _, blk, _ in outs],
        out_shape=[jax.ShapeDtypeStruct(shp, dt) for shp, _, dt in outs],
        scratch_shapes=[pltpu.VMEM((HEADS, ROW_TILE, HD), F32)],
        compiler_params=_cparams("parallel"),
        name="in_proj_even",
    )(x, g, w)


def _in_proj_odd(x, g, w, w2, b2):
    m = x.shape[0]
    n16 = 2 * C_KD + 2 * C_VD
    return pl.pallas_call(
        _in_proj_odd_body,
        grid=(m // ROW_TILE,),
        in_specs=[pl.BlockSpec((ROW_TILE, D_MODEL), lambda i: (i, 0)),
                  pl.BlockSpec((1, D_MODEL), lambda i: (0, 0)),
                  _resident(w.shape), _resident(w2.shape),
                  pl.BlockSpec((1, 2 * C_KD), lambda i: (0, 0))],
        out_specs=[pl.BlockSpec((ROW_TILE, n16), lambda i: (i, 0)),
                   pl.BlockSpec((ROW_TILE, 2 * C_KD), lambda i: (i, 0))],
        out_shape=[jax.ShapeDtypeStruct((m, n16), BF16), jax.ShapeDtypeStruct((m, 2 * C_KD), F32)],
        compiler_params=_cparams("parallel"),
        name="in_proj_odd",
    )(x, g, w, w2, b2)


def _off_ranges(rev):
    out = []
    for j in range(N_SUB):
        lo, hi = (0, SUB * j) if rev else (SUB * (j + 1), CHUNK)
        if hi > lo:
            out.append((j, lo, hi))
    return out


def _gla_constants(rev):
    t = np.arange(CHUNK)
    tri = (t[None, :] >= t[:, None]) if rev else (t[None, :] <= t[:, None])
    causal = tri
    same8 = (t[None, :] // SUB8) == (t[:, None] // SUB8)
    same16 = (t[None, :] // SUB) == (t[:, None] // SUB)
    f = lambda a: jnp.asarray(a.astype(np.float32))
    return (jnp.asarray(np.concatenate([tri, tri], axis=1).astype(np.float32), BF16),
            f(same8 & causal), f(same16 & ~same8 & causal))


def _gla_shared_constants():
    e = np.zeros((SUB8 * HD, CHUNK), np.float32)
    for s in range(SUB8):
        e[s * HD:(s + 1) * HD, s::SUB8] = 1.0
    return (jnp.asarray(e, BF16),)


def _rows(ref, r, n):
    return jnp.broadcast_to(ref[pl.ds(r, 1), :], (n, HD))


def _gla_cumsum(k, g2, b_ref, k_ref, tri_ref):
    g_hi = g2.astype(BF16)
    g_lo = (g2 - g_hi.astype(F32)).astype(BF16)
    b = _dot(tri_ref[...], jnp.concatenate([g_hi, g_lo], axis=0))
    b_ref[...] = b
    k_ref[...] = k
    return b


def _gla_scores(qs, k, v, b, st_ref, b_ref, k_ref, emat_ref, rev):
    b_tot = b_ref[pl.ds(0 if rev else CHUNK - 1, 1), :]

    st = st_ref[...]
    o = _dot_nt((qs * jnp.exp2(b)).astype(BF16), st.astype(BF16))
    st_ref[...] = st * jnp.exp2(b_tot) + _dot_tn(v, (k * jnp.exp2(b_tot - b)).astype(BF16))

    edge16 = [SUB * i + (0 if rev else SUB - 1) for i in range(N_SUB)]
    k_off = k * jnp.exp2(jnp.concatenate([_rows(b_ref, e, SUB) for e in edge16], axis=0) - b)
    zq = jnp.zeros((SUB, HD), F32)
    q_parts, k_parts = [], []
    for j, lo, hi in _off_ranges(rev):
        q_off = qs[lo:hi] * jnp.exp2(jnp.minimum(b[lo:hi] - _rows(b_ref, edge16[j], 1), 0.0))
        q_parts.append(jnp.concatenate([zq] * (lo // SUB) + [q_off] + [zq] * ((CHUNK - hi) // SUB), axis=0))
        k_parts.append(jnp.concatenate([zq] * j + [k_off[SUB * j:SUB * (j + 1)]] + [zq] * (N_SUB - 1 - j),
                                       axis=0))
    a_16 = _dot_nt(jnp.concatenate(q_parts, axis=1).astype(BF16),
                   jnp.concatenate(k_parts, axis=1).astype(BF16))

    edge8 = [SUB * i + (SUB8 if rev else SUB8 - 1) for i in range(N_SUB)]
    d8 = b - jnp.concatenate([_rows(b_ref, e, SUB) for e in edge8], axis=0)
    second = (lax.broadcasted_iota(jnp.int32, (CHUNK, HD), 0) % SUB) >= SUB8
    q_side = jnp.logical_not(second) if rev else second
    w8 = jnp.exp2(jnp.minimum(jnp.where(q_side, d8, -d8), 0.0))
    a_8 = _dot_nt(jnp.where(q_side, qs * w8, 0.0).astype(BF16), jnp.where(q_side, 0.0, k * w8).astype(BF16))

    slabs = []
    for s in range(SUB8):
        rows = [SUB8 * i + s for i in range(CHUNK // SUB8)]
        b_s = jnp.concatenate([_rows(b_ref, r, SUB8) for r in rows], axis=0)
        k_s = jnp.concatenate([_rows(k_ref, r, SUB8) for r in rows], axis=0)
        slabs.append((qs * k_s * jnp.exp2(jnp.minimum(b - b_s, 0.0))).astype(BF16))
    a_dg = _dot(jnp.concatenate(slabs, axis=1), emat_ref[...])
    return o, a_16, a_8, a_dg


def _gla_output(o, a_16, a_8, a_dg, v, m_dg_ref, m_8_ref):
    a = a_dg * m_dg_ref[...] + a_8 * m_8_ref[...] + a_16
    return o + _dot(a.astype(BF16), v)


def _sigmoid_pair(z):
    e = jnp.exp(-jnp.abs(z))
    r = 1.0 / (1.0 + e)
    er = e * r
    pos = z >= 0.0
    return jnp.where(pos, r, er), jnp.where(pos, er, r)


def _log_sigmoid(x):
    return jnp.minimum(x, 0.0) - jnp.log1p(jnp.exp(-jnp.abs(x)))


def _bidir_body(q_ref, kf_ref, kb_ref, v_ref, gate_ref, lbf_ref, lbb_ref, ng_ref,
                trif_ref, mdf_ref, m8f_ref, trib_ref, mdb_ref, m8b_ref, e_ref,
                o_ref, accf_ref, accb_ref, sf_ref, sb_ref, stage_ref, *, nc, hgrn, q_scale):
    sf_ref[...] = jnp.zeros_like(sf_ref)
    sb_ref[...] = jnp.zeros_like(sb_ref)

    def step(c, carry):
        streams = []
        for u, rev in [(u, rev) for u in range(CHUNKS_PER_STEP) for rev in (False, True)]:
            cf = c * CHUNKS_PER_STEP + u
            sl = pl.ds(pl.multiple_of((nc - 1 - cf if rev else cf) * CHUNK, CHUNK), CHUNK)
            b_ref = stage_ref.at[4 * u + 2 * int(rev)]
            kst_ref = stage_ref.at[4 * u + 2 * int(rev) + 1]
            if hgrn:
                lb = (lbb_ref if rev else lbf_ref)[...]
                sig, nsig = _sigmoid_pair((kb_ref if rev else kf_ref)[sl, :])
                g = jnp.log(lb + (1.0 - lb) * sig)
                k = (1.0 - lb) * nsig
            else:
                g = _log_sigmoid((lbb_ref if rev else lbf_ref)[sl, :]) * (1.0 / GATE_NORMALIZER)
                k = kf_ref[sl, :].astype(F32)
            g2 = jnp.maximum(g, LOG_DECAY_MIN) * LOG2E
            qs = q_ref[sl, :].astype(F32)
            if q_scale != 1.0:
                qs = qs * q_scale
            b = _gla_cumsum(k, g2, b_ref, kst_ref, trib_ref if rev else trif_ref)
            streams.append((rev, sl, qs, k, b, b_ref, kst_ref))
        scores = [_gla_scores(qs, k, v_ref[sl, :], b, sb_ref if rev else sf_ref, b_ref, kst_ref, e_ref, rev)
                  for rev, sl, qs, k, b, b_ref, kst_ref in streams]
        for (rev, sl, *_), sc in zip(streams, scores):
            o = _gla_output(*sc, v_ref[sl, :], mdb_ref if rev else mdf_ref, m8b_ref if rev else m8f_ref)
            (accb_ref if rev else accf_ref)[sl, :] = o
        return carry

    lax.fori_loop(0, nc // CHUNKS_PER_STEP, step, 0)

    ng = ng_ref[...]

    def finish(i, carry):
        sl = pl.ds(pl.multiple_of(i * ROW_TILE, ROW_TILE), ROW_TILE)
        o = accf_ref[sl, :] + accb_ref[sl, :]
        gt = gate_ref[sl, :].astype(F32)
        sig, _ = _sigmoid_pair(gt)
        o_ref[sl, :] = (_rms(o, ng) * (gt * sig)).astype(o_ref.dtype)
        return carry

    lax.fori_loop(0, (nc * CHUNK) // ROW_TILE, finish, 0)


def _bidir_call(name, bsz, t, dv, hgrn, q_scale, args, in_specs):
    consts = (*_gla_constants(False), *_gla_constants(True), *_gla_shared_constants())
    const = lambda shape: pl.BlockSpec(shape, lambda b, h: (0,) * len(shape))
    return pl.pallas_call(
        functools.partial(_bidir_body, nc=t // CHUNK, hgrn=hgrn, q_scale=q_scale),
        grid=(bsz, HEADS),
        in_specs=in_specs + [const(c.shape) for c in consts],
        out_specs=pl.BlockSpec((None, t, dv), lambda b, h: (b, 0, h)),
        out_shape=jax.ShapeDtypeStruct((bsz, t, HEADS * dv), BF16),
        scratch_shapes=[pltpu.VMEM((t, dv), F32), pltpu.VMEM((t, dv), F32),
                        pltpu.VMEM((dv, HD), F32), pltpu.VMEM((dv, HD), F32)]
                       + [pltpu.VMEM((4 * CHUNKS_PER_STEP, CHUNK, HD), F32)],
        compiler_params=_cparams("parallel", "parallel"),
        name=name,
    )(*args, *consts)


def _hgrn_mixer(a16, a32, lb_f, lb_b, norm_g):
    bsz, t, _ = a16.shape
    col = lambda off: pl.BlockSpec((None, t, HD), lambda b, h: (b, 0, off + h))
    vec = pl.BlockSpec((1, HD), lambda b, h: (0, h))
    return _bidir_call("hgrn2_bidir", bsz, t, HD, True, 1.0,
                       (a16, a32, a32, a16, a16, lb_f, lb_b, norm_g),
                       [col(0), col(0), col(HEADS), col(HEADS), col(2 * HEADS), vec, vec, vec])


def _gla_mixer(c16, lg, norm_g):
    bsz, t, _ = c16.shape
    col = lambda off: pl.BlockSpec((None, t, HD), lambda b, h: (b, 0, off + h))
    wide = lambda off: pl.BlockSpec((None, t, C_HDV), lambda b, h: (b, 0, off + h))
    v_off = 2 * C_KD // C_HDV
    return _bidir_call("gla_bidir", bsz, t, C_HDV, False, HD ** -0.5,
                       (c16, c16, c16, c16, c16, lg, lg, norm_g),
                       [col(0), col(HEADS), col(HEADS), wide(v_off), wide(v_off + HEADS), col(0), col(HEADS),
                        pl.BlockSpec((1, C_HDV), lambda b, h: (0, h))])


def _attn_body(q_ref, k_ref, v_ref, o_ref, l_ref, *, length, dil, radius, slopes):
    win = min(ATT_Q + 2 * radius, length)
    n_tiles = length // ATT_Q
    per = 2 if n_tiles % 2 == 0 else 1
    scale = HD ** -0.5

    def step(i, carry):
        probs = []
        for u in range(per):
            m0 = pl.multiple_of((i * per + u) * ATT_Q, ATT_Q)
            k0 = pl.multiple_of(jnp.clip(m0 - radius, 0, length - win), radius)
            dist = jnp.abs(k0 + lax.broadcasted_iota(jnp.int32, (ATT_Q, win), 1)
                           - m0 - lax.broadcasted_iota(jnp.int32, (ATT_Q, win), 0))
            valid = dist <= radius
            fdist = (dil * dist).astype(F32)
            for h in range(HEADS):
                probs.append((m0, k0, h, valid, fdist))
        cols = lambda h: slice(h * HD, (h + 1) * HD)
        s = [_dot_nt(q_ref[pl.ds(m0, ATT_Q), cols(h)], k_ref[pl.ds(k0, win), cols(h)])
             for m0, k0, h, _, _ in probs]
        s = [jnp.where(valid, x * scale - slopes[h] * fdist, NEG_INF)
             for x, (_, _, h, valid, fdist) in zip(s, probs)]
        mx = [jnp.max(x, axis=-1, keepdims=True) for x in s]
        p = [jnp.exp(x - m) for x, m in zip(s, mx)]
        den = [jnp.sum(x, axis=-1, keepdims=True) for x in p]
        o = [_dot(x.astype(BF16), v_ref[pl.ds(k0, win), cols(h)]) for x, (_, k0, h, _, _) in zip(p, probs)]
        for x, d, m, (m0, _, h, _, _) in zip(o, den, mx, probs):
            o_ref[pl.ds(m0, ATT_Q), cols(h)] = (x * (1.0 / d)).astype(o_ref.dtype)
            l_ref[pl.ds(m0, ATT_Q), h * LSE_REP:(h + 1) * LSE_REP] = jnp.broadcast_to(
                m + jnp.log(d), (ATT_Q, LSE_REP))
        return carry

    lax.fori_loop(0, n_tiles // per, step, 0)


def _dilated_attention(qkv, gi):
    bsz, length, _ = qkv.shape
    win, dil = B_GROUPS[gi]
    n = N_GROUPS * HEADS
    slopes = tuple(2.0 ** (-ALIBI_MAX_EXP * (gi * HEADS + h + 1) / n) for h in range(HEADS))
    spec = lambda j: pl.BlockSpec((None, length, A_W), lambda b, r: (b, 0, 3 * r + j))
    return pl.pallas_call(
        functools.partial(_attn_body, length=length, dil=dil, radius=win // (2 * dil), slopes=slopes),
        grid=(bsz, dil),
        in_specs=[spec(0), spec(1), spec(2)],
        out_specs=[pl.BlockSpec((None, length, A_W), lambda b, r: (b, 0, r)),
                   pl.BlockSpec((None, length, LSE_W), lambda b, r: (b, 0, r))],
        out_shape=[jax.ShapeDtypeStruct((bsz, length, dil * A_W), BF16),
                   jax.ShapeDtypeStruct((bsz, length, dil * LSE_W), F32)],
        compiler_params=_cparams("parallel", "parallel"),
        name=f"dilated_attn_g{gi}",
    )(qkv, qkv, qkv)


def _lse_expansion():
    e = np.zeros((LSE_W, A_W), np.float32)
    for h in range(HEADS):
        e[h * LSE_REP, h * HD:(h + 1) * HD] = 1.0
    return jnp.asarray(e, BF16)


def _out_even_body(h_ref, a_ref, o0_ref, o1_ref, o2_ref, l0_ref, l1_ref, l2_ref, e_ref, w_ref, y_ref,
                   ob1_ref, ob2_ref, lb1_ref, lb2_ref):
    def natural(o_ref, l_ref, ob_ref, lb_ref, dil):
        if dil == 1:
            return o_ref[...].astype(F32), l_ref[...]
        n = ROW_TILE // dil
        for r in range(dil):
            for c in range(HEADS):
                col = r * A_W + c * HD
                ob_ref[c, pl.ds(r, n, stride=dil), :] = o_ref[:, col:col + HD].astype(F32)
            lb_ref[pl.ds(r, n, stride=dil), :] = l_ref[:, r * LSE_W:(r + 1) * LSE_W]
        return jnp.concatenate([ob_ref[c] for c in range(HEADS)], axis=1), lb_ref[...]

    parts = [natural(o0_ref, l0_ref, None, None, B_GROUPS[0][1]),
             natural(o1_ref, l1_ref, ob1_ref, lb1_ref, B_GROUPS[1][1]),
             natural(o2_ref, l2_ref, ob2_ref, lb2_ref, B_GROUPS[2][1])]
    lses = [l for _, l in parts]
    mx = jnp.maximum(jnp.maximum(lses[0], lses[1]), lses[2])
    es = [jnp.exp(l - mx) for l in lses]
    inv = 1.0 / (es[0] + es[1] + es[2])
    emat = e_ref[...]
    mixed_b = None
    for e_g, (o_g, _) in zip(es, parts):
        alpha = e_g * inv
        hi = alpha.astype(BF16)
        lo = (alpha - hi.astype(F32)).astype(BF16)
        term = (_dot(hi, emat) + _dot(lo, emat)) * o_g
        mixed_b = term if mixed_b is None else mixed_b + term
    y_ref[...] = (h_ref[...] + _dot(a_ref[...], w_ref[:A_W, :]) + _dot(mixed_b.astype(BF16), w_ref[A_W:, :]))


def _out_even(h, out_a, os_, lses, w):
    m = h.shape[0]
    emat = _lse_expansion()
    row = lambda n: pl.BlockSpec((ROW_TILE, n), lambda i: (i, 0))
    grp = lambda width: [pl.BlockSpec((ROW_TILE // dil, dil * width), lambda i: (i, 0)) for _, dil in B_GROUPS]
    return pl.pallas_call(
        _out_even_body,
        grid=(m // ROW_TILE,),
        in_specs=[row(D_MODEL), row(A_W)] + grp(A_W) + grp(LSE_W) + [_resident(emat.shape), _resident(w.shape)],
        out_specs=row(D_MODEL),
        out_shape=jax.ShapeDtypeStruct((m, D_MODEL), F32),
        scratch_shapes=[pltpu.VMEM((HEADS, ROW_TILE, HD), F32), pltpu.VMEM((HEADS, ROW_TILE, HD), F32),
                        pltpu.VMEM((ROW_TILE, LSE_W), F32), pltpu.VMEM((ROW_TILE, LSE_W), F32)],
        compiler_params=_cparams("parallel"),
        name="out_proj_even",
    )(h, out_a, *os_, *lses, emat, w)


def _out_odd_body(h_ref, a_ref, w_ref, y_ref):
    y_ref[...] = h_ref[...] + _dot(a_ref[...], w_ref[...])


def _out_odd(h, mixed, w):
    m = h.shape[0]
    row = lambda n: pl.BlockSpec((ROW_TILE, n), lambda i: (i, 0))
    return pl.pallas_call(
        _out_odd_body,
        grid=(m // ROW_TILE,),
        in_specs=[row(D_MODEL), row(C_VD), _resident(w.shape)],
        out_specs=row(D_MODEL),
        out_shape=jax.ShapeDtypeStruct((m, D_MODEL), F32),
        compiler_params=_cparams("parallel"),
        name="out_proj_odd",
    )(h, mixed, w)


def _ffn_body(hp_ref, h_ref, hn_ref, p_ref, gf_ref, wup_ref, cw_ref, cb_ref, wdn_ref, gp_ref, wg_ref, wp_ref,
              go_ref, y_ref, xe_ref, *, tiles_per_seq, final):
    i = pl.program_id(0) % tiles_per_seq
    gf = gf_ref[...]
    h = h_ref[...]
    keep_prev = jnp.where(i > 0, 1.0, 0.0).astype(F32)
    keep_next = jnp.where(i < tiles_per_seq - 1, 1.0, 0.0).astype(F32)
    xe_ref[0:HALO, :] = _rms(hp_ref[...], gf) * keep_prev
    xe_ref[HALO:HALO + ROW_TILE, :] = _rms(h, gf)
    xe_ref[HALO + ROW_TILE:, :] = _rms(hn_ref[...], gf) * keep_next
    xe = xe_ref[...].astype(BF16)
    rows = ROW_TILE + 2 * HALO

    def conv(u, c):
        w = cw_ref[:, c:c + FF_CHUNK]
        prev = pltpu.roll(u, 1, 0)[HALO:HALO + ROW_TILE]
        nxt = pltpu.roll(u, rows - 1, 0)[HALO:HALO + ROW_TILE]
        return (prev * w[0:1] + u[HALO:HALO + ROW_TILE] * w[1:2] + nxt * w[2:3] + cb_ref[:, c:c + FF_CHUNK])

    acc = jnp.zeros((ROW_TILE, D_MODEL), F32)
    for c in range(0, D_FF, FF_CHUNK):
        a = conv(_dot(xe, wup_ref[:, c:c + FF_CHUNK]), c)
        gt = conv(_dot(xe, wup_ref[:, D_FF + c:D_FF + c + FF_CHUNK]), D_FF + c)
        act = a * (0.5 * gt * (1.0 + lax.erf(gt * (2.0 ** -0.5))))
        acc = acc + _dot(act.astype(BF16), wdn_ref[c:c + FF_CHUNK, :])
    h2 = h + acc
    sig, _ = _sigmoid_pair(_dot(_rms(h2, gp_ref[...]).astype(BF16), wg_ref[...]))
    h3 = h2 + sig * _dot(p_ref[...].astype(BF16), wp_ref[...])
    y_ref[...] = _rms(h3, go_ref[...]) if final else h3


def _ffn_ple(h, p, t, gf, wup, cw, cb, wdn, gp, wg, wp, go, final):
    m = h.shape[0]
    per_tile = ROW_TILE // HALO
    last_blk = m // HALO - 1
    vec = lambda n: pl.BlockSpec((1, n), lambda i: (0, 0))
    return pl.pallas_call(
        functools.partial(_ffn_body, tiles_per_seq=t // ROW_TILE, final=final),
        grid=(m // ROW_TILE,),
        in_specs=[pl.BlockSpec((HALO, D_MODEL), lambda i: (jnp.maximum(i * per_tile - 1, 0), 0)),
                  pl.BlockSpec((ROW_TILE, D_MODEL), lambda i: (i, 0)),
                  pl.BlockSpec((HALO, D_MODEL), lambda i: (jnp.minimum((i + 1) * per_tile, last_blk), 0)),
                  pl.BlockSpec((ROW_TILE, PLE_DIM), lambda i: (i, 0)),
                  vec(D_MODEL), _resident(wup.shape), _resident(cw.shape), vec(2 * D_FF), _resident(wdn.shape),
                  vec(D_MODEL), _resident(wg.shape), _resident(wp.shape), vec(D_MODEL)],
        out_specs=pl.BlockSpec((ROW_TILE, D_MODEL), lambda i: (i, 0)),
        out_shape=jax.ShapeDtypeStruct((m, D_MODEL), F32),
        scratch_shapes=[pltpu.VMEM((ROW_TILE + 2 * HALO, D_MODEL), F32)],
        compiler_params=_cparams("parallel"),
        name="conv_ffn_ple",
    )(h, h, h, p, gf, wup, cw, cb, wdn, gp, wg, wp, go)


def _prep_weights(ev_w_in, od_w_in, gla_w_gate_up, gla_b_gate):
    aq, zf, zb, ai, ag, bqkv = jnp.split(ev_w_in, [A_W, 2 * A_W, 3 * A_W, 4 * A_W, 5 * A_W], axis=-1)
    ev_w = jnp.concatenate([aq, ai, ag, zf, zb, bqkv], axis=-1).astype(BF16)
    od_w = jnp.pad(od_w_in, ((0, 0), (0, 0), (0, LR_PAD - 2 * C_RANK))).astype(BF16)
    n_odd = od_w_in.shape[0]
    w2 = jnp.zeros((n_odd, LR_PAD, 2 * C_KD), F32)
    w2 = w2.at[:, :C_RANK, :C_KD].set(gla_w_gate_up[:, 0]).at[:, C_RANK:2 * C_RANK, C_KD:].set(gla_w_gate_up[:, 1])
    b2 = gla_b_gate.reshape(n_odd, 1, 2 * C_KD)
    return ev_w, od_w, w2.astype(BF16), b2


def _trunk(x, p, prm):
    bsz, t, _ = x.shape
    m = bsz * t
    h = x.reshape(m, D_MODEL)
    for l in range(DEPTH):
        g_mix = prm["norm_mix_g"][l][None]
        if l % 2 == 0:
            e = l // 2
            a16, a32, *qkv = _in_proj_even(h, g_mix, prm["ev_w"][e])
            out_a = _hgrn_mixer(a16.reshape(bsz, t, -1), a32.reshape(bsz, t, -1),
                                prm["lb"][0, e][None], prm["lb"][1, e][None], prm["hgrn_norm_g"][e][None])
            att = [_dilated_attention(x_g.reshape(bsz, t // dil, dil * QKV_W), gi)
                   for gi, (x_g, (_, dil)) in enumerate(zip(qkv, B_GROUPS))]
            h = _out_even(h, out_a.reshape(m, A_W),
                          [o.reshape(m // dil, dil * A_W) for (o, _), (_, dil) in zip(att, B_GROUPS)],
                          [l_.reshape(m // dil, dil * LSE_W) for (_, l_), (_, dil) in zip(att, B_GROUPS)],
                          prm["ev_w_out"][e])
        else:
            o = l // 2
            c16, lg = _in_proj_odd(h, g_mix, prm["od_w"][o], prm["od_w2"][o], prm["od_b2"][o])
            mixed = _gla_mixer(c16.reshape(bsz, t, -1), lg.reshape(bsz, t, -1), prm["gla_norm_g"][o][None])
            h = _out_odd(h, mixed.reshape(m, C_VD), prm["od_w_out"][o])
        h = _ffn_ple(h, p[l].reshape(m, PLE_DIM), t, prm["norm_ffn_g"][l][None], prm["ffn_w_up"][l],
                     prm["ffn_conv_w"][l], prm["ffn_conv_b"][l][None], prm["ffn_w_down"][l],
                     prm["norm_ple_g"][l][None], prm["ple_w_gate"][l], prm["ple_w_proj"][l],
                     prm["norm_out_g"][None], final=(l == DEPTH - 1))
    return h.reshape(bsz, t, D_MODEL)


def kernel(x_prompt, x_sample, p_prompt, p_sample, norm_mix_g, ev_w_in, hgrn_lb_logits, hgrn_norm_g, ev_w_out, od_w_in, gla_w_gate_up, gla_b_gate, gla_norm_g, od_w_out, norm_ffn_g, ffn_w_up, ffn_conv_w, ffn_conv_b, ffn_w_down, norm_ple_g, ple_w_gate, ple_w_proj, norm_out_g):
    lb = jnp.cumsum(jax.nn.softmax(hgrn_lb_logits.astype(F32), axis=1), axis=1)
    lb = lb - lb[:, :1]
    ev_w, od_w, od_w2, od_b2 = _prep_weights(ev_w_in, od_w_in, gla_w_gate_up, gla_b_gate)
    prm = dict(norm_mix_g=norm_mix_g, ev_w=ev_w, lb=lb, hgrn_norm_g=hgrn_norm_g, ev_w_out=ev_w_out.astype(BF16),
               od_w=od_w, od_w2=od_w2, od_b2=od_b2, gla_norm_g=gla_norm_g, od_w_out=od_w_out.astype(BF16),
               norm_ffn_g=norm_ffn_g, ffn_w_up=ffn_w_up.astype(BF16), ffn_conv_w=ffn_conv_w,
               ffn_conv_b=ffn_conv_b, ffn_w_down=ffn_w_down.astype(BF16), norm_ple_g=norm_ple_g,
               ple_w_gate=ple_w_gate.astype(BF16), ple_w_proj=ple_w_proj.astype(BF16), norm_out_g=norm_out_g)
    return _trunk(x_prompt, p_prompt, prm), _trunk(x_sample, p_sample, prm)
```

```python
import functools

import numpy as np
import jax
import jax.numpy as jnp
from jax import lax
from jax.experimental import pallas as pl
from jax.experimental.pallas import tpu as pltpu

F32 = jnp.float32
BF16 = jnp.bfloat16

D_MODEL = 1024
DEPTH = 4
PLE_DIM = 256
EPS = 1e-6
HEADS = 4
HD = 128
A_W = HEADS * HD
B_GROUPS = ((128, 1), (512, 4), (2048, 16))
N_GROUPS = len(B_GROUPS)
QKV_W = 3 * A_W
ATT_Q = 128
ALIBI_MAX_EXP = 8.0
C_KD = 512
C_VD = 1024
C_HDV = C_VD // HEADS
C_RANK = 16
GATE_NORMALIZER = 16.0
CHUNK = 64
SUB = 16
N_SUB = CHUNK // SUB
SUB8 = 8
CHUNKS_PER_STEP = 4
LOG_DECAY_MIN = -30.0
LOG2E = 1.4426950408889634
NEG_INF = -1e30
D_FF = 2816
LSE_W = 128
LSE_REP = LSE_W // HEADS
LR_PAD = 128

ROW_TILE = 512
HALO = 8
FF_CHUNK = 256
VMEM_LIMIT = 56 * 1024 * 1024


def _cparams(*sem):
    return pltpu.CompilerParams(dimension_semantics=sem, vmem_limit_bytes=VMEM_LIMIT)


def _resident(shape):
    nd = len(shape)
    return pl.BlockSpec(shape, lambda *_: (0,) * nd, pipeline_mode=pl.Buffered(1))


def _rms(x, g):
    ms = jnp.mean(x * x, axis=-1, keepdims=True)
    return x * lax.rsqrt(ms + EPS) * g


def _dot(a, b):
    return jnp.dot(a, b, preferred_element_type=F32)


def _dot_nt(a, b):
    return lax.dot_general(a, b, (((1,), (1,)), ((), ())), preferred_element_type=F32)


def _dot_tn(a, b):
    return lax.dot_general(a, b, (((0,), (0,)), ((), ())), preferred_element_type=F32)


def _store_cols(xn, w_ref, out_refs, c0=0, col_chunk=512):
    for o_ref in out_refs:
        n = o_ref.shape[-1]
        for j in range(0, n, col_chunk):
            wj = min(col_chunk, n - j)
            o_ref[:, j:j + wj] = _dot(xn, w_ref[:, c0 + j:c0 + j + wj]).astype(o_ref.dtype)
        c0 += n
    return c0


def _in_proj_even_body(x_ref, g_ref, w_ref, a16_ref, a32_ref, g0_ref, g1_ref, g2_ref, u_ref):
    xn = _rms(x_ref[...], g_ref[...]).astype(BF16)
    c0 = _store_cols(xn, w_ref, (a16_ref, a32_ref))
    for (_, dil), o_ref in zip(B_GROUPS, (g0_ref, g1_ref, g2_ref)):
        n = ROW_TILE // dil
        for j in range(3):
            u = _dot(xn, w_ref[:, c0:c0 + A_W])
            c0 += A_W
            if dil == 1:
                o_ref[:, j * A_W:(j + 1) * A_W] = u.astype(BF16)
            else:
                for c in range(HEADS):
                    u_ref[c] = u[:, c * HD:(c + 1) * HD]
                for r in range(dil):
                    for c in range(HEADS):
                        col = (3 * r + j) * A_W + c * HD
                        o_ref[:, col:col + HD] = u_ref[c, pl.ds(r, n, stride=dil), :].astype(BF16)


def _in_proj_odd_body(x_ref, g_ref, w_ref, w2_ref, b2_ref, c16_ref, lg_ref):
    xn = _rms(x_ref[...], g_ref[...]).astype(BF16)
    c0 = _store_cols(xn, w_ref, (c16_ref,))
    lr = _dot(xn, w_ref[:, c0:c0 + LR_PAD]).astype(BF16)
    n = lg_ref.shape[-1]
    for j in range(0, n, 512):
        lg_ref[:, j:j + 512] = _dot(lr, w2_ref[:, j:j + 512]) + b2_ref[:, j:j + 512]


def _in_proj_even(x, g, w):
    m = x.shape[0]
    outs = [((m, 3 * A_W), (ROW_TILE, 3 * A_W), BF16), ((m, 2 * A_W), (ROW_TILE, 2 * A_W), F32)]
    for _, dil in B_GROUPS:
        outs.append(((m // dil, dil * QKV_W), (ROW_TILE // dil, dil * QKV_W), BF16))
    return pl.pallas_call(
        _in_proj_even_body,
        grid=(m // ROW_TILE,),
        in_specs=[pl.BlockSpec((ROW_TILE, D_MODEL), lambda i: (i, 0)),
                  pl.BlockSpec((1, D_MODEL), lambda i: (0, 0)),
                  _resident(w.shape)],
        out_specs=[pl.BlockSpec(blk, lambda i: (i, 0)) for _, blk, _ in outs],
        out_shape=[jax.ShapeDtypeStruct(shp, dt) for shp, _, dt in outs],
        scratch_shapes=[pltpu.VMEM((HEADS, ROW_TILE, HD), F32)],
        compiler_params=_cparams("parallel"),
        name="in_proj_even",
    )(x, g, w)


def _in_proj_odd(x, g, w, w2, b2):
    m = x.shape[0]
    n16 = 2 * C_KD + 2 * C_VD
    return pl.pallas_call(
        _in_proj_odd_body,
        grid=(m // ROW_TILE,),
        in_specs=[pl.BlockSpec((ROW_TILE, D_MODEL), lambda i: (i, 0)),
                  pl.BlockSpec((1, D_MODEL), lambda i: (0, 0)),
                  _resident(w.shape), _resident(w2.shape),
                  pl.BlockSpec((1, 2 * C_KD), lambda i: (0, 0))],
        out_specs=[pl.BlockSpec((ROW_TILE, n16), lambda i: (i, 0)),
                   pl.BlockSpec((ROW_TILE, 2 * C_KD), lambda i: (i, 0))],
        out_shape=[jax.ShapeDtypeStruct((m, n16), BF16), jax.ShapeDtypeStruct((m, 2 * C_KD), F32)],
        compiler_params=_cparams("parallel"),
        name="in_proj_odd",
    )(x, g, w, w2, b2)


def _off_ranges(rev):
    out = []
    for j in range(N_SUB):
        lo, hi = (0, SUB * j) if rev else (SUB * (j + 1), CHUNK)
        if hi > lo:
            out.append((j, lo, hi))
    return out


def _gla_constants(rev):
    t = np.arange(CHUNK)
    tri = (t[None, :] >= t[:, None]) if rev else (t[None, :] <= t[:, None])
    causal = tri
    same8 = (t[None, :] // SUB8) == (t[:, None] // SUB8)
    same16 = (t[None, :] // SUB) == (t[:, None] // SUB)
    f = lambda a: jnp.asarray(a.astype(np.float32))
    return (jnp.asarray(np.concatenate([tri, tri], axis=1).astype(np.float32), BF16),
            f(same8 & causal), f(same16 & ~same8 & causal))


def _gla_shared_constants():
    e = np.zeros((SUB8 * HD, CHUNK), np.float32)
    for s in range(SUB8):
        e[s * HD:(s + 1) * HD, s::SUB8] = 1.0
    return (jnp.asarray(e, BF16),)


def _rows(ref, r, n):
    return jnp.broadcast_to(ref[pl.ds(r, 1), :], (n, HD))


def _gla_cumsum(k, g2, b_ref, k_ref, tri_ref):
    g_hi = g2.astype(BF16)
    g_lo = (g2 - g_hi.astype(F32)).astype(BF16)
    b = _dot(tri_ref[...], jnp.concatenate([g_hi, g_lo], axis=0))
    b_ref[...] = b
    k_ref[...] = k
    return b


def _gla_scores(qs, k, v, b, st_ref, b_ref, k_ref, emat_ref, rev):
    b_tot = b_ref[pl.ds(0 if rev else CHUNK - 1, 1), :]

    st = st_ref[...]
    o = _dot_nt((qs * jnp.exp2(b)).astype(BF16), st.astype(BF16))
    st_ref[...] = st * jnp.exp2(b_tot) + _dot_tn(v, (k * jnp.exp2(b_tot - b)).astype(BF16))

    edge16 = [SUB * i + (0 if rev else SUB - 1) for i in range(N_SUB)]
    k_off = k * jnp.exp2(jnp.concatenate([_rows(b_ref, e, SUB) for e in edge16], axis=0) - b)
    zq = jnp.zeros((SUB, HD), F32)
    q_parts, k_parts = [], []
    for j, lo, hi in _off_ranges(rev):
        q_off = qs[lo:hi] * jnp.exp2(jnp.minimum(b[lo:hi] - _rows(b_ref, edge16[j], 1), 0.0))
        q_parts.append(jnp.concatenate([zq] * (lo // SUB) + [q_off] + [zq] * ((CHUNK - hi) // SUB), axis=0))
        k_parts.append(jnp.concatenate([zq] * j + [k_off[SUB * j:SUB * (j + 1)]] + [zq] * (N_SUB - 1 - j),
                                       axis=0))
    a_16 = _dot_nt(jnp.concatenate(q_parts, axis=1).astype(BF16),
                   jnp.concatenate(k_parts, axis=1).astype(BF16))

    edge8 = [SUB * i + (SUB8 if rev else SUB8 - 1) for i in range(N_SUB)]
    d8 = b - jnp.concatenate([_rows(b_ref, e, SUB) for e in edge8], axis=0)
    second = (lax.broadcasted_iota(jnp.int32, (CHUNK, HD), 0) % SUB) >= SUB8
    q_side = jnp.logical_not(second) if rev else second
    w8 = jnp.exp2(jnp.minimum(jnp.where(q_side, d8, -d8), 0.0))
    a_8 = _dot_nt(jnp.where(q_side, qs * w8, 0.0).astype(BF16), jnp.where(q_side, 0.0, k * w8).astype(BF16))

    slabs = []
    for s in range(SUB8):
        rows = [SUB8 * i + s for i in range(CHUNK // SUB8)]
        b_s = jnp.concatenate([_rows(b_ref, r, SUB8) for r in rows], axis=0)
        k_s = jnp.concatenate([_rows(k_ref, r, SUB8) for r in rows], axis=0)
        slabs.append((qs * k_s * jnp.exp2(jnp.minimum(b - b_s, 0.0))).astype(BF16))
    a_dg = _dot(jnp.concatenate(slabs, axis=1), emat_ref[...])
    return o, a_16, a_8, a_dg


def _gla_output(o, a_16, a_8, a_dg, v, m_dg_ref, m_8_ref):
    a = a_dg * m_dg_ref[...] + a_8 * m_8_ref[...] + a_16
    return o + _dot(a.astype(BF16), v)


def _sigmoid_pair(z):
    e = jnp.exp(-jnp.abs(z))
    r = 1.0 / (1.0 + e)
    er = e * r
    pos = z >= 0.0
    return jnp.where(pos, r, er), jnp.where(pos, er, r)


def _log_sigmoid(x):
    return jnp.minimum(x, 0.0) - jnp.log1p(jnp.exp(-jnp.abs(x)))


def _bidir_body(q_ref, kf_ref, kb_ref, v_ref, gate_ref, lbf_ref, lbb_ref, ng_ref,
                trif_ref, mdf_ref, m8f_ref, trib_ref, mdb_ref, m8b_ref, e_ref,
                o_ref, accf_ref, accb_ref, sf_ref, sb_ref, stage_ref, *, nc, hgrn, q_scale):
    sf_ref[...] = jnp.zeros_like(sf_ref)
    sb_ref[...] = jnp.zeros_like(sb_ref)

    def step(c, carry):
        streams = []
        for u, rev in [(u, rev) for u in range(CHUNKS_PER_STEP) for rev in (False, True)]:
            cf = c * CHUNKS_PER_STEP + u
            sl = pl.ds(pl.multiple_of((nc - 1 - cf if rev else cf) * CHUNK, CHUNK), CHUNK)
            b_ref = stage_ref.at[4 * u + 2 * int(rev)]
            kst_ref = stage_ref.at[4 * u + 2 * int(rev) + 1]
            if hgrn:
                lb = (lbb_ref if rev else lbf_ref)[...]
                sig, nsig = _sigmoid_pair((kb_ref if rev else kf_ref)[sl, :])
                g = jnp.log(lb + (1.0 - lb) * sig)
                k = (1.0 - lb) * nsig
            else:
                g = _log_sigmoid((lbb_ref if rev else lbf_ref)[sl, :]) * (1.0 / GATE_NORMALIZER)
                k = kf_ref[sl, :].astype(F32)
            g2 = jnp.maximum(g, LOG_DECAY_MIN) * LOG2E
            qs = q_ref[sl, :].astype(F32)
            if q_scale != 1.0:
                qs = qs * q_scale
            b = _gla_cumsum(k, g2, b_ref, kst_ref, trib_ref if rev else trif_ref)
            streams.append((rev, sl, qs, k, b, b_ref, kst_ref))
        scores = [_gla_scores(qs, k, v_ref[sl, :], b, sb_ref if rev else sf_ref, b_ref, kst_ref, e_ref, rev)
                  for rev, sl, qs, k, b, b_ref, kst_ref in streams]
        for (rev, sl, *_), sc in zip(streams, scores):
            o = _gla_output(*sc, v_ref[sl, :], mdb_ref if rev else mdf_ref, m8b_ref if rev else m8f_ref)
            (accb_ref if rev else accf_ref)[sl, :] = o
        return carry

    lax.fori_loop(0, nc // CHUNKS_PER_STEP, step, 0)

    ng = ng_ref[...]

    def finish(i, carry):
        sl = pl.ds(pl.multiple_of(i * ROW_TILE, ROW_TILE), ROW_TILE)
        o = accf_ref[sl, :] + accb_ref[sl, :]
        gt = gate_ref[sl, :].astype(F32)
        sig, _ = _sigmoid_pair(gt)
        o_ref[sl, :] = (_rms(o, ng) * (gt * sig)).astype(o_ref.dtype)
        return carry

    lax.fori_loop(0, (nc * CHUNK) // ROW_TILE, finish, 0)


def _bidir_call(name, bsz, t, dv, hgrn, q_scale, args, in_specs):
    consts = (*_gla_constants(False), *_gla_constants(True), *_gla_shared_constants())
    const = lambda shape: pl.BlockSpec(shape, lambda b, h: (0,) * len(shape))
    return pl.pallas_call(
        functools.partial(_bidir_body, nc=t // CHUNK, hgrn=hgrn, q_scale=q_scale),
        grid=(bsz, HEADS),
        in_specs=in_specs + [const(c.shape) for c in consts],
        out_specs=pl.BlockSpec((None, t, dv), lambda b, h: (b, 0, h)),
        out_shape=jax.ShapeDtypeStruct((bsz, t, HEADS * dv), BF16),
        scratch_shapes=[pltpu.VMEM((t, dv), F32), pltpu.VMEM((t, dv), F32),
                        pltpu.VMEM((dv, HD), F32), pltpu.VMEM((dv, HD), F32)]
                       + [pltpu.VMEM((4 * CHUNKS_PER_STEP, CHUNK, HD), F32)],
        compiler_params=_cparams("parallel", "parallel"),
        name=name,
    )(*args, *consts)


def _hgrn_mixer(a16, a32, lb_f, lb_b, norm_g):
    bsz, t, _ = a16.shape
    col = lambda off: pl.BlockSpec((None, t, HD), lambda b, h: (b, 0, off + h))
    vec = pl.BlockSpec((1, HD), lambda b, h: (0, h))
    return _bidir_call("hgrn2_bidir", bsz, t, HD, True, 1.0,
                       (a16, a32, a32, a16, a16, lb_f, lb_b, norm_g),
                       [col(0), col(0), col(HEADS), col(HEADS), col(2 * HEADS), vec, vec, vec])


def _gla_mixer(c16, lg, norm_g):
    bsz, t, _ = c16.shape
    col = lambda off: pl.BlockSpec((None, t, HD), lambda b, h: (b, 0, off + h))
    wide = lambda off: pl.BlockSpec((None, t, C_HDV), lambda b, h: (b, 0, off + h))
    v_off = 2 * C_KD // C_HDV
    return _bidir_call("gla_bidir", bsz, t, C_HDV, False, HD ** -0.5,
                       (c16, c16, c16, c16, c16, lg, lg, norm_g),
                       [col(0), col(HEADS), col(HEADS), wide(v_off), wide(v_off + HEADS), col(0), col(HEADS),
                        pl.BlockSpec((1, C_HDV), lambda b, h: (0, h))])


def _attn_body(q_ref, k_ref, v_ref, o_ref, l_ref, *, length, dil, radius, slopes):
    win = min(ATT_Q + 2 * radius, length)
    n_tiles = length // ATT_Q
    per = 2 if n_tiles % 2 == 0 else 1
    scale = HD ** -0.5

    def step(i, carry):
        probs = []
        for u in range(per):
            m0 = pl.multiple_of((i * per + u) * ATT_Q, ATT_Q)
            k0 = pl.multiple_of(jnp.clip(m0 - radius, 0, length - win), radius)
            dist = jnp.abs(k0 + lax.broadcasted_iota(jnp.int32, (ATT_Q, win), 1)
                           - m0 - lax.broadcasted_iota(jnp.int32, (ATT_Q, win), 0))
            valid = dist <= radius
            fdist = (dil * dist).astype(F32)
            for h in range(HEADS):
                probs.append((m0, k0, h, valid, fdist))
        cols = lambda h: slice(h * HD, (h + 1) * HD)
        s = [_dot_nt(q_ref[pl.ds(m0, ATT_Q), cols(h)], k_ref[pl.ds(k0, win), cols(h)])
             for m0, k0, h, _, _ in probs]
        s = [jnp.where(valid, x * scale - slopes[h] * fdist, NEG_INF)
             for x, (_, _, h, valid, fdist) in zip(s, probs)]
        mx = [jnp.max(x, axis=-1, keepdims=True) for x in s]
        p = [jnp.exp(x - m) for x, m in zip(s, mx)]
        den = [jnp.sum(x, axis=-1, keepdims=True) for x in p]
        o = [_dot(x.astype(BF16), v_ref[pl.ds(k0, win), cols(h)]) for x, (_, k0, h, _, _) in zip(p, probs)]
        for x, d, m, (m0, _, h, _, _) in zip(o, den, mx, probs):
            o_ref[pl.ds(m0, ATT_Q), cols(h)] = (x * (1.0 / d)).astype(o_ref.dtype)
            l_ref[pl.ds(m0, ATT_Q), h * LSE_REP:(h + 1) * LSE_REP] = jnp.broadcast_to(
                m + jnp.log(d), (ATT_Q, LSE_REP))
        return carry

    lax.fori_loop(0, n_tiles // per, step, 0)


def _dilated_attention(qkv, gi):
    bsz, length, _ = qkv.shape
    win, dil = B_GROUPS[gi]
    n = N_GROUPS * HEADS
    slopes = tuple(2.0 ** (-ALIBI_MAX_EXP * (gi * HEADS + h + 1) / n) for h in range(HEADS))
    spec = lambda j: pl.BlockSpec((None, length, A_W), lambda b, r: (b, 0, 3 * r + j))
    return pl.pallas_call(
        functools.partial(_attn_body, length=length, dil=dil, radius=win // (2 * dil), slopes=slopes),
        grid=(bsz, dil),
        in_specs=[spec(0), spec(1), spec(2)],
        out_specs=[pl.BlockSpec((None, length, A_W), lambda b, r: (b, 0, r)),
                   pl.BlockSpec((None, length, LSE_W), lambda b, r: (b, 0, r))],
        out_shape=[jax.ShapeDtypeStruct((bsz, length, dil * A_W), BF16),
                   jax.ShapeDtypeStruct((bsz, length, dil * LSE_W), F32)],
        compiler_params=_cparams("parallel", "parallel"),
        name=f"dilated_attn_g{gi}",
    )(qkv, qkv, qkv)


def _lse_expansion():
    e = np.zeros((LSE_W, A_W), np.float32)
    for h in range(HEADS):
        e[h * LSE_REP, h * HD:(h + 1) * HD] = 1.0
    return jnp.asarray(e, BF16)


def _out_even_body(h_ref, a_ref, o0_ref, o1_ref, o2_ref, l0_ref, l1_ref, l2_ref, e_ref, w_ref, y_ref,
                   ob1_ref, ob2_ref, lb1_ref, lb2_ref):
    def natural(o_ref, l_ref, ob_ref, lb_ref, dil):
        if dil == 1:
            return o_ref[...].astype(F32), l_ref[...]
        n = ROW_TILE // dil
        for r in range(dil):
            for c in range(HEADS):
                col = r * A_W + c * HD
                ob_ref[c, pl.ds(r, n, stride=dil), :] = o_ref[:, col:col + HD].astype(F32)
            lb_ref[pl.ds(r, n, stride=dil), :] = l_ref[:, r * LSE_W:(r + 1) * LSE_W]
        return jnp.concatenate([ob_ref[c] for c in range(HEADS)], axis=1), lb_ref[...]

    parts = [natural(o0_ref, l0_ref, None, None, B_GROUPS[0][1]),
             natural(o1_ref, l1_ref, ob1_ref, lb1_ref, B_GROUPS[1][1]),
             natural(o2_ref, l2_ref, ob2_ref, lb2_ref, B_GROUPS[2][1])]
    lses = [l for _, l in parts]
    mx = jnp.maximum(jnp.maximum(lses[0], lses[1]), lses[2])
    es = [jnp.exp(l - mx) for l in lses]
    inv = 1.0 / (es[0] + es[1] + es[2])
    emat = e_ref[...]
    mixed_b = None
    for e_g, (o_g, _) in zip(es, parts):
        alpha = (e_g * inv).astype(BF16)
        term = _dot(alpha, emat) * o_g
        mixed_b = term if mixed_b is None else mixed_b + term
    y_ref[...] = (h_ref[...] + _dot(a_ref[...], w_ref[:A_W, :]) + _dot(mixed_b.astype(BF16), w_ref[A_W:, :]))


def _out_even(h, out_a, os_, lses, w):
    m = h.shape[0]
    emat = _lse_expansion()
    row = lambda n: pl.BlockSpec((ROW_TILE, n), lambda i: (i, 0))
    grp = lambda width: [pl.BlockSpec((ROW_TILE // dil, dil * width), lambda i: (i, 0)) for _, dil in B_GROUPS]
    return pl.pallas_call(
        _out_even_body,
        grid=(m // ROW_TILE,),
        in_specs=[row(D_MODEL), row(A_W)] + grp(A_W) + grp(LSE_W) + [_resident(emat.shape), _resident(w.shape)],
        out_specs=row(D_MODEL),
        out_shape=jax.ShapeDtypeStruct((m, D_MODEL), F32),
        scratch_shapes=[pltpu.VMEM((HEADS, ROW_TILE, HD), F32), pltpu.VMEM((HEADS, ROW_TILE, HD), F32),
                        pltpu.VMEM((ROW_TILE, LSE_W), F32), pltpu.VMEM((ROW_TILE, LSE_W), F32)],
        compiler_params=_cparams("parallel"),
        name="out_proj_even",
    )(h, out_a, *os_, *lses, emat, w)


def _out_odd_body(h_ref, a_ref, w_ref, y_ref):
    y_ref[...] = h_ref[...] + _dot(a_ref[...], w_ref[...])


def _out_odd(h, mixed, w):
    m = h.shape[0]
    row = lambda n: pl.BlockSpec((ROW_TILE, n), lambda i: (i, 0))
    return pl.pallas_call(
        _out_odd_body,
        grid=(m // ROW_TILE,),
        in_specs=[row(D_MODEL), row(C_VD), _resident(w.shape)],
        out_specs=row(D_MODEL),
        out_shape=jax.ShapeDtypeStruct((m, D_MODEL), F32),
        compiler_params=_cparams("parallel"),
        name="out_proj_odd",
    )(h, mixed, w)


def _ffn_body(hp_ref, h_ref, hn_ref, p_ref, gf_ref, wup_ref, cw_ref, cb_ref, wdn_ref, gp_ref, wg_ref, wp_ref,
              go_ref, y_ref, xe_ref, act_ref, *, tiles_per_seq, final):
    i = pl.program_id(0) % tiles_per_seq
    gf = gf_ref[...]
    h = h_ref[...]
    keep_prev = jnp.where(i > 0, 1.0, 0.0).astype(F32)
    keep_next = jnp.where(i < tiles_per_seq - 1, 1.0, 0.0).astype(F32)
    xe_ref[0:HALO, :] = _rms(hp_ref[...], gf) * keep_prev
    xe_ref[HALO:HALO + ROW_TILE, :] = _rms(h, gf)
    xe_ref[HALO + ROW_TILE:, :] = _rms(hn_ref[...], gf) * keep_next
    xe = xe_ref[...].astype(BF16)
    rows = ROW_TILE + 2 * HALO

    def conv(u, c):
        w = cw_ref[:, c:c + FF_CHUNK]
        prev = pltpu.roll(u, 1, 0)[HALO:HALO + ROW_TILE]
        nxt = pltpu.roll(u, rows - 1, 0)[HALO:HALO + ROW_TILE]
        return (prev * w[0:1] + u[HALO:HALO + ROW_TILE] * w[1:2] + nxt * w[2:3] + cb_ref[:, c:c + FF_CHUNK])

    for c in range(0, D_FF, FF_CHUNK):
        a = conv(_dot(xe, wup_ref[:, c:c + FF_CHUNK]), c)
        gt = conv(_dot(xe, wup_ref[:, D_FF + c:D_FF + c + FF_CHUNK]), D_FF + c)
        act_ref[:, c:c + FF_CHUNK] = (a * (0.5 * gt * (1.0 + lax.erf(gt * (2.0 ** -0.5))))).astype(BF16)
    h2 = h + _dot(act_ref[...], wdn_ref[...])
    sig, _ = _sigmoid_pair(_dot(_rms(h2, gp_ref[...]).astype(BF16), wg_ref[...]))
    h3 = h2 + sig * _dot(p_ref[...].astype(BF16), wp_ref[...])
    y_ref[...] = _rms(h3, go_ref[...]) if final else h3


def _ffn_ple(h, p, t, gf, wup, cw, cb, wdn, gp, wg, wp, go, final):
    m = h.shape[0]
    per_tile = ROW_TILE // HALO
    last_blk = m // HALO - 1
    vec = lambda n: pl.BlockSpec((1, n), lambda i: (0, 0))
    return pl.pallas_call(
        functools.partial(_ffn_body, tiles_per_seq=t // ROW_TILE, final=final),
        grid=(m // ROW_TILE,),
        in_specs=[pl.BlockSpec((HALO, D_MODEL), lambda i: (jnp.maximum(i * per_tile - 1, 0), 0)),
                  pl.BlockSpec((ROW_TILE, D_MODEL), lambda i: (i, 0)),
                  pl.BlockSpec((HALO, D_MODEL), lambda i: (jnp.minimum((i + 1) * per_tile, last_blk), 0)),
                  pl.BlockSpec((ROW_TILE, PLE_DIM), lambda i: (i, 0)),
                  vec(D_MODEL), _resident(wup.shape), _resident(cw.shape), vec(2 * D_FF), _resident(wdn.shape),
                  vec(D_MODEL), _resident(wg.shape), _resident(wp.shape), vec(D_MODEL)],
        out_specs=pl.BlockSpec((ROW_TILE, D_MODEL), lambda i: (i, 0)),
        out_shape=jax.ShapeDtypeStruct((m, D_MODEL), F32),
        scratch_shapes=[pltpu.VMEM((ROW_TILE + 2 * HALO, D_MODEL), F32), pltpu.VMEM((ROW_TILE, D_FF), BF16)],
        compiler_params=_cparams("parallel"),
        name="conv_ffn_ple",
    )(h, h, h, p, gf, wup, cw, cb, wdn, gp, wg, wp, go)


def _prep_weights(ev_w_in, od_w_in, gla_w_gate_up, gla_b_gate):
    aq, zf, zb, ai, ag, bqkv = jnp.split(ev_w_in, [A_W, 2 * A_W, 3 * A_W, 4 * A_W, 5 * A_W], axis=-1)
    ev_w = jnp.concatenate([aq, ai, ag, zf, zb, bqkv], axis=-1).astype(BF16)
    od_w = jnp.pad(od_w_in, ((0, 0), (0, 0), (0, LR_PAD - 2 * C_RANK))).astype(BF16)
    n_odd = od_w_in.shape[0]
    w2 = jnp.zeros((n_odd, LR_PAD, 2 * C_KD), F32)
    w2 = w2.at[:, :C_RANK, :C_KD].set(gla_w_gate_up[:, 0]).at[:, C_RANK:2 * C_RANK, C_KD:].set(gla_w_gate_up[:, 1])
    b2 = gla_b_gate.reshape(n_odd, 1, 2 * C_KD)
    return ev_w, od_w, w2.astype(BF16), b2


def _trunk(x, p, prm):
    bsz, t, _ = x.shape
    m = bsz * t
    h = x.reshape(m, D_MODEL)
    for l in range(DEPTH):
        g_mix = prm["norm_mix_g"][l][None]
        if l % 2 == 0:
            e = l // 2
            a16, a32, *qkv = _in_proj_even(h, g_mix, prm["ev_w"][e])
            out_a = _hgrn_mixer(a16.reshape(bsz, t, -1), a32.reshape(bsz, t, -1),
                                prm["lb"][0, e][None], prm["lb"][1, e][None], prm["hgrn_norm_g"][e][None])
            att = [_dilated_attention(x_g.reshape(bsz, t // dil, dil * QKV_W), gi)
                   for gi, (x_g, (_, dil)) in enumerate(zip(qkv, B_GROUPS))]
            h = _out_even(h, out_a.reshape(m, A_W),
                          [o.reshape(m // dil, dil * A_W) for (o, _), (_, dil) in zip(att, B_GROUPS)],
                          [l_.reshape(m // dil, dil * LSE_W) for (_, l_), (_, dil) in zip(att, B_GROUPS)],
                          prm["ev_w_out"][e])
        else:
            o = l // 2
            c16, lg = _in_proj_odd(h, g_mix, prm["od_w"][o], prm["od_w2"][o], prm["od_b2"][o])
            mixed = _gla_mixer(c16.reshape(bsz, t, -1), lg.reshape(bsz, t, -1), prm["gla_norm_g"][o][None])
            h = _out_odd(h, mixed.reshape(m, C_VD), prm["od_w_out"][o])
        h = _ffn_ple(h, p[l].reshape(m, PLE_DIM), t, prm["norm_ffn_g"][l][None], prm["ffn_w_up"][l],
                     prm["ffn_conv_w"][l], prm["ffn_conv_b"][l][None], prm["ffn_w_down"][l],
                     prm["norm_ple_g"][l][None], prm["ple_w_gate"][l], prm["ple_w_proj"][l],
                     prm["norm_out_g"][None], final=(l == DEPTH - 1))
    return h.reshape(bsz, t, D_MODEL)


def kernel(x_prompt, x_sample, p_prompt, p_sample, norm_mix_g, ev_w_in, hgrn_lb_logits, hgrn_norm_g, ev_w_out, od_w_in, gla_w_gate_up, gla_b_gate, gla_norm_g, od_w_out, norm_ffn_g, ffn_w_up, ffn_conv_w, ffn_conv_b, ffn_w_down, norm_ple_g, ple_w_gate, ple_w_proj, norm_out_g):
    lb = jnp.cumsum(jax.nn.softmax(hgrn_lb_logits.astype(F32), axis=1), axis=1)
    lb = lb - lb[:, :1]
    ev_w, od_w, od_w2, od_b2 = _prep_weights(ev_w_in, od_w_in, gla_w_gate_up, gla_b_gate)
    prm = dict(norm_mix_g=norm_mix_g, ev_w=ev_w, lb=lb, hgrn_norm_g=hgrn_norm_g, ev_w_out=ev_w_out.astype(BF16),
               od_w=od_w, od_w2=od_w2, od_b2=od_b2, gla_norm_g=gla_norm_g, od_w_out=od_w_out.astype(BF16),
               norm_ffn_g=norm_ffn_g, ffn_w_up=ffn_w_up.astype(BF16), ffn_conv_w=ffn_conv_w,
               ffn_conv_b=ffn_conv_b, ffn_w_down=ffn_w_down.astype(BF16), norm_ple_g=norm_ple_g,
               ple_w_gate=ple_w_gate.astype(BF16), ple_w_proj=ple_w_proj.astype(BF16), norm_out_g=norm_out_g)
    return _trunk(x_prompt, p_prompt, prm), _trunk(x_sample, p_sample, prm)
```

```python
import functools

import numpy as np
import jax
import jax.numpy as jnp
from jax import lax
from jax.experimental import pallas as pl
from jax.experimental.pallas import tpu as pltpu

F32 = jnp.float32
BF16 = jnp.bfloat16

D_MODEL = 1024
DEPTH = 4
PLE_DIM = 256
EPS = 1e-6
HEADS = 4
HD = 128
A_W = HEADS * HD
B_GROUPS = ((128, 1), (512, 4), (2048, 16))
N_GROUPS = len(B_GROUPS)
QKV_W = 3 * A_W
ATT_Q = 128
ALIBI_MAX_EXP = 8.0
C_KD = 512
C_VD = 1024
C_HDV = C_VD // HEADS
C_RANK = 16
GATE_NORMALIZER = 16.0
CHUNK = 64
SUB = 16
N_SUB = CHUNK // SUB
SUB8 = 8
CHUNKS_PER_STEP = {True: 8, False: 4}
LOG_DECAY_MIN = -30.0
LOG2E = 1.4426950408889634
SAFE_LOG2_RANGE = 96.0
NEG_INF = -1e30
D_FF = 2816
LSE_W = 128
LSE_REP = LSE_W // HEADS
LR_PAD = 128

ROW_TILE = 512
HALO = 8
FF_CHUNK = 256
VMEM_LIMIT = 56 * 1024 * 1024


def _cparams(*sem):
    return pltpu.CompilerParams(dimension_semantics=sem, vmem_limit_bytes=VMEM_LIMIT)


def _resident(shape):
    nd = len(shape)
    return pl.BlockSpec(shape, lambda *_: (0,) * nd, pipeline_mode=pl.Buffered(1))


def _rms(x, g):
    ms = jnp.mean(x * x, axis=-1, keepdims=True)
    return x * lax.rsqrt(ms + EPS) * g


def _dot(a, b):
    return jnp.dot(a, b, preferred_element_type=F32)


def _dot_nt(a, b):
    return lax.dot_general(a, b, (((1,), (1,)), ((), ())), preferred_element_type=F32)


def _dot_tn(a, b):
    return lax.dot_general(a, b, (((0,), (0,)), ((), ())), preferred_element_type=F32)


def _store_cols(xn, w_ref, out_refs, c0=0, col_chunk=512):
    for o_ref in out_refs:
        n = o_ref.shape[-1]
        for j in range(0, n, col_chunk):
            wj = min(col_chunk, n - j)
            o_ref[:, j:j + wj] = _dot(xn, w_ref[:, c0 + j:c0 + j + wj]).astype(o_ref.dtype)
        c0 += n
    return c0


def _log2_decay(log_f):
    return jnp.maximum(log_f, LOG_DECAY_MIN) * LOG2E


def _in_proj_even_body(x_ref, g_ref, lb_ref, w_ref, a16_ref, a32_ref, g0_ref, g1_ref, g2_ref, u_ref):
    xn = _rms(x_ref[...], g_ref[...]).astype(BF16)
    c0 = _store_cols(xn, w_ref, (a16_ref.at[:, :3 * A_W],))
    for d in range(2):
        lb = lb_ref[:, d * A_W:(d + 1) * A_W]
        sig, nsig = _sigmoid_pair(_dot(xn, w_ref[:, c0:c0 + A_W]))
        c0 += A_W
        a32_ref[:, d * A_W:(d + 1) * A_W] = _log2_decay(jnp.log(lb + (1.0 - lb) * sig))
        a16_ref[:, (3 + d) * A_W:(4 + d) * A_W] = ((1.0 - lb) * nsig).astype(BF16)
    for (_, dil), o_ref in zip(B_GROUPS, (g0_ref, g1_ref, g2_ref)):
        n = ROW_TILE // dil
        for j in range(3):
            u = _dot(xn, w_ref[:, c0:c0 + A_W])
            c0 += A_W
            if dil == 1:
                o_ref[:, j * A_W:(j + 1) * A_W] = u.astype(BF16)
            else:
                for c in range(HEADS):
                    u_ref[c] = u[:, c * HD:(c + 1) * HD]
                for r in range(dil):
                    for c in range(HEADS):
                        col = (3 * r + j) * A_W + c * HD
                        o_ref[:, col:col + HD] = u_ref[c, pl.ds(r, n, stride=dil), :].astype(BF16)


def _in_proj_odd_body(x_ref, g_ref, w_ref, w2_ref, b2_ref, c16_ref, g2_ref):
    xn = _rms(x_ref[...], g_ref[...]).astype(BF16)
    c16_ref[:, :C_KD] = (_dot(xn, w_ref[:, :C_KD]) * (HD ** -0.5)).astype(BF16)
    c0 = _store_cols(xn, w_ref, (c16_ref.at[:, C_KD:],), c0=C_KD)
    lr = _dot(xn, w_ref[:, c0:c0 + LR_PAD]).astype(BF16)
    n = g2_ref.shape[-1]
    for j in range(0, n, 512):
        logit = _dot(lr, w2_ref[:, j:j + 512]) + b2_ref[:, j:j + 512]
        g2_ref[:, j:j + 512] = _log2_decay(_log_sigmoid(logit) * (1.0 / GATE_NORMALIZER))


def _in_proj_even(x, g, lb, w):
    m = x.shape[0]
    outs = [((m, 5 * A_W), (ROW_TILE, 5 * A_W), BF16), ((m, 2 * A_W), (ROW_TILE, 2 * A_W), F32)]
    for _, dil in B_GROUPS:
        outs.append(((m // dil, dil * QKV_W), (ROW_TILE // dil, dil * QKV_W), BF16))
    return pl.pallas_call(
        _in_proj_even_body,
        grid=(m // ROW_TILE,),
        in_specs=[pl.BlockSpec((ROW_TILE, D_MODEL), lambda i: (i, 0)),
                  pl.BlockSpec((1, D_MODEL), lambda i: (0, 0)),
                  pl.BlockSpec((1, 2 * A_W), lambda i: (0, 0)),
                  _resident(w.shape)],
        out_specs=[pl.BlockSpec(blk, lambda i: (i, 0)) for _, blk, _ in outs],
        out_shape=[jax.ShapeDtypeStruct(shp, dt) for shp, _, dt in outs],
        scratch_shapes=[pltpu.VMEM((HEADS, ROW_TILE, HD), F32)],
        compiler_params=_cparams("parallel"),
        name="in_proj_even",
    )(x, g, lb, w)


def _in_proj_odd(x, g, w, w2, b2):
    m = x.shape[0]
    n16 = 2 * C_KD + 2 * C_VD
    return pl.pallas_call(
        _in_proj_odd_body,
        grid=(m // ROW_TILE,),
        in_specs=[pl.BlockSpec((ROW_TILE, D_MODEL), lambda i: (i, 0)),
                  pl.BlockSpec((1, D_MODEL), lambda i: (0, 0)),
                  _resident(w.shape), _resident(w2.shape),
                  pl.BlockSpec((1, 2 * C_KD), lambda i: (0, 0))],
        out_specs=[pl.BlockSpec((ROW_TILE, n16), lambda i: (i, 0)),
                   pl.BlockSpec((ROW_TILE, 2 * C_KD), lambda i: (i, 0))],
        out_shape=[jax.ShapeDtypeStruct((m, n16), BF16), jax.ShapeDtypeStruct((m, 2 * C_KD), F32)],
        compiler_params=_cparams("parallel"),
        name="in_proj_odd",
    )(x, g, w, w2, b2)


def _off_ranges(rev):
    out = []
    for j in range(N_SUB):
        lo, hi = (0, SUB * j) if rev else (SUB * (j + 1), CHUNK)
        if hi > lo:
            out.append((j, lo, hi))
    return out


def _gla_constants(rev):
    t = np.arange(CHUNK)
    tri = (t[None, :] >= t[:, None]) if rev else (t[None, :] <= t[:, None])
    causal = tri
    same8 = (t[None, :] // SUB8) == (t[:, None] // SUB8)
    same16 = (t[None, :] // SUB) == (t[:, None] // SUB)
    f = lambda a: jnp.asarray(a.astype(np.float32))
    return (jnp.asarray(np.concatenate([tri, tri], axis=1).astype(np.float32), BF16),
            f(same16 & causal), f(same8 & causal), f(same16 & ~same8 & causal))


def _gla_shared_constants():
    e = np.zeros((SUB8 * HD, CHUNK), np.float32)
    for s in range(SUB8):
        e[s * HD:(s + 1) * HD, s::SUB8] = 1.0
    r = np.arange(ROW_TILE)
    bsel = (r[None, :] // SUB == np.arange(ROW_TILE // SUB)[:, None]).astype(np.float32)
    return jnp.asarray(e, BF16), jnp.asarray(bsel, BF16)


def _rows(ref, r, n):
    return jnp.broadcast_to(ref[pl.ds(r, 1), :], (n, HD))


def _gla_cumsum(k, g2, b_ref, k_ref, tri_ref, fast):
    g_hi = g2.astype(BF16)
    g_lo = (g2 - g_hi.astype(F32)).astype(BF16)
    b = _dot(tri_ref[...], jnp.concatenate([g_hi, g_lo], axis=0))
    b_ref[...] = b
    if not fast:
        k_ref[...] = k
    return b


def _gla_scores(qs, k, v, b, st_ref, b_ref, k_ref, emat_ref, rev, fast):
    b_tot = b_ref[pl.ds(0 if rev else CHUNK - 1, 1), :]

    st = st_ref[...]
    o = _dot_nt((qs * jnp.exp2(b)).astype(BF16), st.astype(BF16))
    st_ref[...] = st * jnp.exp2(b_tot) + _dot_tn(v, (k * jnp.exp2(b_tot - b)).astype(BF16))

    edge16 = [SUB * i + (0 if rev else SUB - 1) for i in range(N_SUB)]
    e_ko = jnp.exp2(jnp.concatenate([_rows(b_ref, e, SUB) for e in edge16], axis=0) - b)
    k_off = k * e_ko
    zq = jnp.zeros((SUB, HD), F32)
    q_parts, k_parts = [], []
    for j, lo, hi in _off_ranges(rev):
        q_off = qs[lo:hi] * jnp.exp2(jnp.minimum(b[lo:hi] - _rows(b_ref, edge16[j], 1), 0.0))
        q_parts.append(jnp.concatenate([zq] * (lo // SUB) + [q_off] + [zq] * ((CHUNK - hi) // SUB), axis=0))
        k_parts.append(jnp.concatenate([zq] * j + [k_off[SUB * j:SUB * (j + 1)]] + [zq] * (N_SUB - 1 - j),
                                       axis=0))
    a_16 = _dot_nt(jnp.concatenate(q_parts, axis=1).astype(BF16),
                   jnp.concatenate(k_parts, axis=1).astype(BF16))

    if fast:
        return o, a_16, _dot_nt((qs * (1.0 / e_ko)).astype(BF16), k_off.astype(BF16))

    edge8 = [SUB * i + (SUB8 if rev else SUB8 - 1) for i in range(N_SUB)]
    d8 = b - jnp.concatenate([_rows(b_ref, e, SUB) for e in edge8], axis=0)
    second = (lax.broadcasted_iota(jnp.int32, (CHUNK, HD), 0) % SUB) >= SUB8
    q_side = jnp.logical_not(second) if rev else second
    w8 = jnp.exp2(jnp.minimum(jnp.where(q_side, d8, -d8), 0.0))
    a_8 = _dot_nt(jnp.where(q_side, qs * w8, 0.0).astype(BF16), jnp.where(q_side, 0.0, k * w8).astype(BF16))

    slabs = []
    for s in range(SUB8):
        rows = [SUB8 * i + s for i in range(CHUNK // SUB8)]
        b_s = jnp.concatenate([_rows(b_ref, r, SUB8) for r in rows], axis=0)
        k_s = jnp.concatenate([_rows(k_ref, r, SUB8) for r in rows], axis=0)
        slabs.append((qs * k_s * jnp.exp2(jnp.minimum(b - b_s, 0.0))).astype(BF16))
    a_dg = _dot(jnp.concatenate(slabs, axis=1), emat_ref[...])
    return o, a_16, a_8, a_dg


def _gla_output(scores, v, m_16_ref, m_dg_ref, m_8_ref, fast):
    if fast:
        o, a_16, a_in = scores
        a = a_in * m_16_ref[...] + a_16
    else:
        o, a_16, a_8, a_dg = scores
        a = a_dg * m_dg_ref[...] + a_8 * m_8_ref[...] + a_16
    return o + _dot(a.astype(BF16), v)


def _decay_range_ok(g_ref, bsel_ref, n_tiles):
    sums = [_dot(bsel_ref[...], g_ref[i * ROW_TILE:(i + 1) * ROW_TILE, :].astype(BF16)) for i in range(n_tiles)]
    return jnp.min(functools.reduce(jnp.minimum, sums)) > -SAFE_LOG2_RANGE


def _sigmoid_pair(z):
    e = jnp.exp(-jnp.abs(z))
    r = 1.0 / (1.0 + e)
    er = e * r
    pos = z >= 0.0
    return jnp.where(pos, r, er), jnp.where(pos, er, r)


def _log_sigmoid(x):
    return jnp.minimum(x, 0.0) - jnp.log1p(jnp.exp(-jnp.abs(x)))


def _bidir_body(q_ref, kf_ref, kb_ref, g2f_ref, g2b_ref, v_ref, gate_ref, ng_ref,
                trif_ref, m16f_ref, mdf_ref, m8f_ref, trib_ref, m16b_ref, mdb_ref, m8b_ref, e_ref, bsel_ref,
                o_ref, accf_ref, accb_ref, sf_ref, sb_ref, stage_ref, *, nc):
    def run(fast):
        per = CHUNKS_PER_STEP[fast]
        sf_ref[...] = jnp.zeros_like(sf_ref)
        sb_ref[...] = jnp.zeros_like(sb_ref)

        def step(c, carry):
            streams = []
            for u, rev in [(u, rev) for u in range(per) for rev in (False, True)]:
                cf = c * per + u
                sl = pl.ds(pl.multiple_of((nc - 1 - cf if rev else cf) * CHUNK, CHUNK), CHUNK)
                b_ref = stage_ref.at[4 * u + 2 * int(rev)]
                kst_ref = stage_ref.at[4 * u + 2 * int(rev) + 1]
                k = (kb_ref if rev else kf_ref)[sl, :].astype(F32)
                qs = q_ref[sl, :].astype(F32)
                b = _gla_cumsum(k, (g2b_ref if rev else g2f_ref)[sl, :], b_ref, kst_ref,
                                trib_ref if rev else trif_ref, fast)
                streams.append((rev, sl, qs, k, b, b_ref, kst_ref))
            scores = [_gla_scores(qs, k, v_ref[sl, :], b, sb_ref if rev else sf_ref, b_ref, kst_ref, e_ref,
                                  rev, fast)
                      for rev, sl, qs, k, b, b_ref, kst_ref in streams]
            for (rev, sl, *_), sc in zip(streams, scores):
                masks = (m16b_ref, mdb_ref, m8b_ref) if rev else (m16f_ref, mdf_ref, m8f_ref)
                (accb_ref if rev else accf_ref)[sl, :] = _gla_output(sc, v_ref[sl, :], *masks, fast)
            return carry

        lax.fori_loop(0, nc // per, step, 0)

    n_tiles = (nc * CHUNK) // ROW_TILE
    ok = jnp.logical_and(_decay_range_ok(g2f_ref, bsel_ref, n_tiles), _decay_range_ok(g2b_ref, bsel_ref, n_tiles))
    pl.when(ok)(functools.partial(run, True))
    pl.when(jnp.logical_not(ok))(functools.partial(run, False))

    ng = ng_ref[...]

    def finish(i, carry):
        sl = pl.ds(pl.multiple_of(i * ROW_TILE, ROW_TILE), ROW_TILE)
        o = accf_ref[sl, :] + accb_ref[sl, :]
        gt = gate_ref[sl, :].astype(F32)
        sig, _ = _sigmoid_pair(gt)
        o_ref[sl, :] = (_rms(o, ng) * (gt * sig)).astype(o_ref.dtype)
        return carry

    lax.fori_loop(0, (nc * CHUNK) // ROW_TILE, finish, 0)


def _bidir_call(name, bsz, t, dv, args, in_specs):
    consts = (*_gla_constants(False), *_gla_constants(True), *_gla_shared_constants())
    const = lambda shape: pl.BlockSpec(shape, lambda b, h: (0,) * len(shape))
    return pl.pallas_call(
        functools.partial(_bidir_body, nc=t // CHUNK),
        grid=(bsz, HEADS),
        in_specs=in_specs + [const(c.shape) for c in consts],
        out_specs=pl.BlockSpec((None, t, dv), lambda b, h: (b, 0, h)),
        out_shape=jax.ShapeDtypeStruct((bsz, t, HEADS * dv), BF16),
        scratch_shapes=[pltpu.VMEM((t, dv), F32), pltpu.VMEM((t, dv), F32),
                        pltpu.VMEM((dv, HD), F32), pltpu.VMEM((dv, HD), F32)]
                       + [pltpu.VMEM((4 * max(CHUNKS_PER_STEP.values()), CHUNK, HD), F32)],
        compiler_params=_cparams("parallel", "parallel"),
        name=name,
    )(*args, *consts)


def _hgrn_mixer(a16, a32, norm_g):
    bsz, t, _ = a16.shape
    col = lambda off: pl.BlockSpec((None, t, HD), lambda b, h: (b, 0, off * HEADS + h))
    return _bidir_call("hgrn2_bidir", bsz, t, HD,
                       (a16, a16, a16, a32, a32, a16, a16, norm_g),
                       [col(0), col(3), col(4), col(0), col(1), col(1), col(2),
                        pl.BlockSpec((1, HD), lambda b, h: (0, h))])


def _gla_mixer(c16, g2, norm_g):
    bsz, t, _ = c16.shape
    col = lambda off: pl.BlockSpec((None, t, HD), lambda b, h: (b, 0, off * HEADS + h))
    wide = lambda off: pl.BlockSpec((None, t, C_HDV), lambda b, h: (b, 0, off + h))
    v_off = 2 * C_KD // C_HDV
    return _bidir_call("gla_bidir", bsz, t, C_HDV,
                       (c16, c16, c16, g2, g2, c16, c16, norm_g),
                       [col(0), col(1), col(1), col(0), col(1), wide(v_off), wide(v_off + HEADS),
                        pl.BlockSpec((1, C_HDV), lambda b, h: (0, h))])


def _attn_body(q_ref, k_ref, v_ref, o_ref, l_ref, *, length, dil, radius, slopes):
    win = min(ATT_Q + 2 * radius, length)
    n_tiles = length // ATT_Q
    per = 2 if n_tiles % 2 == 0 else 1
    scale = HD ** -0.5

    def step(i, carry):
        probs = []
        for u in range(per):
            m0 = pl.multiple_of((i * per + u) * ATT_Q, ATT_Q)
            k0 = pl.multiple_of(jnp.clip(m0 - radius, 0, length - win), radius)
            dist = jnp.abs(k0 + lax.broadcasted_iota(jnp.int32, (ATT_Q, win), 1)
                           - m0 - lax.broadcasted_iota(jnp.int32, (ATT_Q, win), 0))
            valid = dist <= radius
            fdist = (dil * dist).astype(F32)
            for h in range(HEADS):
                probs.append((m0, k0, h, valid, fdist))
        cols = lambda h: slice(h * HD, (h + 1) * HD)
        s = [_dot_nt(q_ref[pl.ds(m0, ATT_Q), cols(h)], k_ref[pl.ds(k0, win), cols(h)])
             for m0, k0, h, _, _ in probs]
        s = [jnp.where(valid, x * scale - slopes[h] * fdist, NEG_INF)
             for x, (_, _, h, valid, fdist) in zip(s, probs)]
        mx = [jnp.max(x, axis=-1, keepdims=True) for x in s]
        p = [jnp.exp(x - m) for x, m in zip(s, mx)]
        den = [jnp.sum(x, axis=-1, keepdims=True) for x in p]
        o = [_dot(x.astype(BF16), v_ref[pl.ds(k0, win), cols(h)]) for x, (_, k0, h, _, _) in zip(p, probs)]
        for x, d, m, (m0, _, h, _, _) in zip(o, den, mx, probs):
            o_ref[pl.ds(m0, ATT_Q), cols(h)] = (x * (1.0 / d)).astype(o_ref.dtype)
            l_ref[pl.ds(m0, ATT_Q), h * LSE_REP:(h + 1) * LSE_REP] = jnp.broadcast_to(
                m + jnp.log(d), (ATT_Q, LSE_REP))
        return carry

    lax.fori_loop(0, n_tiles // per, step, 0)


def _dilated_attention(qkv, gi):
    bsz, length, _ = qkv.shape
    win, dil = B_GROUPS[gi]
    n = N_GROUPS * HEADS
    slopes = tuple(2.0 ** (-ALIBI_MAX_EXP * (gi * HEADS + h + 1) / n) for h in range(HEADS))
    spec = lambda j: pl.BlockSpec((None, length, A_W), lambda b, r: (b, 0, 3 * r + j))
    return pl.pallas_call(
        functools.partial(_attn_body, length=length, dil=dil, radius=win // (2 * dil), slopes=slopes),
        grid=(bsz, dil),
        in_specs=[spec(0), spec(1), spec(2)],
        out_specs=[pl.BlockSpec((None, length, A_W), lambda b, r: (b, 0, r)),
                   pl.BlockSpec((None, length, LSE_W), lambda b, r: (b, 0, r))],
        out_shape=[jax.ShapeDtypeStruct((bsz, length, dil * A_W), BF16),
                   jax.ShapeDtypeStruct((bsz, length, dil * LSE_W), F32)],
        compiler_params=_cparams("parallel", "parallel"),
        name=f"dilated_attn_g{gi}",
    )(qkv, qkv, qkv)


def _lse_expansion():
    e = np.zeros((LSE_W, A_W), np.float32)
    for h in range(HEADS):
        e[h * LSE_REP, h * HD:(h + 1) * HD] = 1.0
    return jnp.asarray(e, BF16)


def _out_even_body(h_ref, a_ref, o0_ref, o1_ref, o2_ref, l0_ref, l1_ref, l2_ref, e_ref, w_ref, y_ref,
                   ob1_ref, ob2_ref, lb1_ref, lb2_ref):
    def natural(o_ref, l_ref, ob_ref, lb_ref, dil):
        if dil == 1:
            return o_ref[...].astype(F32), l_ref[...]
        n = ROW_TILE // dil
        for r in range(dil):
            for c in range(HEADS):
                col = r * A_W + c * HD
                ob_ref[c, pl.ds(r, n, stride=dil), :] = o_ref[:, col:col + HD].astype(F32)
            lb_ref[pl.ds(r, n, stride=dil), :] = l_ref[:, r * LSE_W:(r + 1) * LSE_W]
        return jnp.concatenate([ob_ref[c] for c in range(HEADS)], axis=1), lb_ref[...]

    parts = [natural(o0_ref, l0_ref, None, None, B_GROUPS[0][1]),
             natural(o1_ref, l1_ref, ob1_ref, lb1_ref, B_GROUPS[1][1]),
             natural(o2_ref, l2_ref, ob2_ref, lb2_ref, B_GROUPS[2][1])]
    lses = [l for _, l in parts]
    mx = jnp.maximum(jnp.maximum(lses[0], lses[1]), lses[2])
    es = [jnp.exp(l - mx) for l in lses]
    inv = 1.0 / (es[0] + es[1] + es[2])
    emat = e_ref[...]
    mixed_b = None
    for e_g, (o_g, _) in zip(es, parts):
        alpha = (e_g * inv).astype(BF16)
        term = _dot(alpha, emat) * o_g
        mixed_b = term if mixed_b is None else mixed_b + term
    y_ref[...] = (h_ref[...] + _dot(a_ref[...], w_ref[:A_W, :]) + _dot(mixed_b.astype(BF16), w_ref[A_W:, :]))


def _out_even(h, out_a, os_, lses, w):
    m = h.shape[0]
    emat = _lse_expansion()
    row = lambda n: pl.BlockSpec((ROW_TILE, n), lambda i: (i, 0))
    grp = lambda width: [pl.BlockSpec((ROW_TILE // dil, dil * width), lambda i: (i, 0)) for _, dil in B_GROUPS]
    return pl.pallas_call(
        _out_even_body,
        grid=(m // ROW_TILE,),
        in_specs=[row(D_MODEL), row(A_W)] + grp(A_W) + grp(LSE_W) + [_resident(emat.shape), _resident(w.shape)],
        out_specs=row(D_MODEL),
        out_shape=jax.ShapeDtypeStruct((m, D_MODEL), F32),
        scratch_shapes=[pltpu.VMEM((HEADS, ROW_TILE, HD), F32), pltpu.VMEM((HEADS, ROW_TILE, HD), F32),
                        pltpu.VMEM((ROW_TILE, LSE_W), F32), pltpu.VMEM((ROW_TILE, LSE_W), F32)],
        compiler_params=_cparams("parallel"),
        name="out_proj_even",
    )(h, out_a, *os_, *lses, emat, w)


def _out_odd_body(h_ref, a_ref, w_ref, y_ref):
    y_ref[...] = h_ref[...] + _dot(a_ref[...], w_ref[...])


def _out_odd(h, mixed, w):
    m = h.shape[0]
    row = lambda n: pl.BlockSpec((ROW_TILE, n), lambda i: (i, 0))
    return pl.pallas_call(
        _out_odd_body,
        grid=(m // ROW_TILE,),
        in_specs=[row(D_MODEL), row(C_VD), _resident(w.shape)],
        out_specs=row(D_MODEL),
        out_shape=jax.ShapeDtypeStruct((m, D_MODEL), F32),
        compiler_params=_cparams("parallel"),
        name="out_proj_odd",
    )(h, mixed, w)


def _ffn_body(hp_ref, h_ref, hn_ref, p_ref, gf_ref, wup_ref, cw_ref, cb_ref, wdn_ref, gp_ref, wg_ref, wp_ref,
              go_ref, y_ref, xe_ref, act_ref, *, tiles_per_seq, final):
    i = pl.program_id(0) % tiles_per_seq
    gf = gf_ref[...]
    h = h_ref[...]
    keep_prev = jnp.where(i > 0, 1.0, 0.0).astype(F32)
    keep_next = jnp.where(i < tiles_per_seq - 1, 1.0, 0.0).astype(F32)
    xe_ref[0:HALO, :] = _rms(hp_ref[...], gf) * keep_prev
    xe_ref[HALO:HALO + ROW_TILE, :] = _rms(h, gf)
    xe_ref[HALO + ROW_TILE:, :] = _rms(hn_ref[...], gf) * keep_next
    xe = xe_ref[...].astype(BF16)
    rows = ROW_TILE + 2 * HALO

    def conv(u, c):
        w = cw_ref[:, c:c + FF_CHUNK]
        prev = pltpu.roll(u, 1, 0)[HALO:HALO + ROW_TILE]
        nxt = pltpu.roll(u, rows - 1, 0)[HALO:HALO + ROW_TILE]
        return (prev * w[0:1] + u[HALO:HALO + ROW_TILE] * w[1:2] + nxt * w[2:3] + cb_ref[:, c:c + FF_CHUNK])

    for c in range(0, D_FF, FF_CHUNK):
        a = conv(_dot(xe, wup_ref[:, c:c + FF_CHUNK]), c)
        gt = conv(_dot(xe, wup_ref[:, D_FF + c:D_FF + c + FF_CHUNK]), D_FF + c)
        act_ref[:, c:c + FF_CHUNK] = (a * (0.5 * gt * (1.0 + lax.erf(gt * (2.0 ** -0.5))))).astype(BF16)
    h2 = h + _dot(act_ref[...], wdn_ref[...])
    sig, _ = _sigmoid_pair(_dot(_rms(h2, gp_ref[...]).astype(BF16), wg_ref[...]))
    h3 = h2 + sig * _dot(p_ref[...].astype(BF16), wp_ref[...])
    y_ref[...] = _rms(h3, go_ref[...]) if final else h3


def _ffn_ple(h, p, t, gf, wup, cw, cb, wdn, gp, wg, wp, go, final):
    m = h.shape[0]
    per_tile = ROW_TILE // HALO
    last_blk = m // HALO - 1
    vec = lambda n: pl.BlockSpec((1, n), lambda i: (0, 0))
    return pl.pallas_call(
        functools.partial(_ffn_body, tiles_per_seq=t // ROW_TILE, final=final),
        grid=(m // ROW_TILE,),
        in_specs=[pl.BlockSpec((HALO, D_MODEL), lambda i: (jnp.maximum(i * per_tile - 1, 0), 0)),
                  pl.BlockSpec((ROW_TILE, D_MODEL), lambda i: (i, 0)),
                  pl.BlockSpec((HALO, D_MODEL), lambda i: (jnp.minimum((i + 1) * per_tile, last_blk), 0)),
                  pl.BlockSpec((ROW_TILE, PLE_DIM), lambda i: (i, 0)),
                  vec(D_MODEL), _resident(wup.shape), _resident(cw.shape), vec(2 * D_FF), _resident(wdn.shape),
                  vec(D_MODEL), _resident(wg.shape), _resident(wp.shape), vec(D_MODEL)],
        out_specs=pl.BlockSpec((ROW_TILE, D_MODEL), lambda i: (i, 0)),
        out_shape=jax.ShapeDtypeStruct((m, D_MODEL), F32),
        scratch_shapes=[pltpu.VMEM((ROW_TILE + 2 * HALO, D_MODEL), F32), pltpu.VMEM((ROW_TILE, D_FF), BF16)],
        compiler_params=_cparams("parallel"),
        name="conv_ffn_ple",
    )(h, h, h, p, gf, wup, cw, cb, wdn, gp, wg, wp, go)


def _prep_weights(ev_w_in, od_w_in, gla_w_gate_up, gla_b_gate):
    aq, zf, zb, ai, ag, bqkv = jnp.split(ev_w_in, [A_W, 2 * A_W, 3 * A_W, 4 * A_W, 5 * A_W], axis=-1)
    ev_w = jnp.concatenate([aq, ai, ag, zf, zb, bqkv], axis=-1).astype(BF16)
    od_w = jnp.pad(od_w_in, ((0, 0), (0, 0), (0, LR_PAD - 2 * C_RANK))).astype(BF16)
    n_odd = od_w_in.shape[0]
    w2 = jnp.zeros((n_odd, LR_PAD, 2 * C_KD), F32)
    w2 = w2.at[:, :C_RANK, :C_KD].set(gla_w_gate_up[:, 0]).at[:, C_RANK:2 * C_RANK, C_KD:].set(gla_w_gate_up[:, 1])
    b2 = gla_b_gate.reshape(n_odd, 1, 2 * C_KD)
    return ev_w, od_w, w2.astype(BF16), b2


def _trunk(x, p, prm):
    bsz, t, _ = x.shape
    m = bsz * t
    h = x.reshape(m, D_MODEL)
    for l in range(DEPTH):
        g_mix = prm["norm_mix_g"][l][None]
        if l % 2 == 0:
            e = l // 2
            lb = jnp.concatenate([prm["lb"][0, e], prm["lb"][1, e]])[None]
            a16, a32, *qkv = _in_proj_even(h, g_mix, lb, prm["ev_w"][e])
            out_a = _hgrn_mixer(a16.reshape(bsz, t, -1), a32.reshape(bsz, t, -1), prm["hgrn_norm_g"][e][None])
            att = [_dilated_attention(x_g.reshape(bsz, t // dil, dil * QKV_W), gi)
                   for gi, (x_g, (_, dil)) in enumerate(zip(qkv, B_GROUPS))]
            h = _out_even(h, out_a.reshape(m, A_W),
                          [o.reshape(m // dil, dil * A_W) for (o, _), (_, dil) in zip(att, B_GROUPS)],
                          [l_.reshape(m // dil, dil * LSE_W) for (_, l_), (_, dil) in zip(att, B_GROUPS)],
                          prm["ev_w_out"][e])
        else:
            o = l // 2
            c16, lg = _in_proj_odd(h, g_mix, prm["od_w"][o], prm["od_w2"][o], prm["od_b2"][o])
            mixed = _gla_mixer(c16.reshape(bsz, t, -1), lg.reshape(bsz, t, -1), prm["gla_norm_g"][o][None])
            h = _out_odd(h, mixed.reshape(m, C_VD), prm["od_w_out"][o])
        h = _ffn_ple(h, p[l].reshape(m, PLE_DIM), t, prm["norm_ffn_g"][l][None], prm["ffn_w_up"][l],
                     prm["ffn_conv_w"][l], prm["ffn_conv_b"][l][None], prm["ffn_w_down"][l],
                     prm["norm_ple_g"][l][None], prm["ple_w_gate"][l], prm["ple_w_proj"][l],
                     prm["norm_out_g"][None], final=(l == DEPTH - 1))
    return h.reshape(bsz, t, D_MODEL)


def kernel(x_prompt, x_sample, p_prompt, p_sample, norm_mix_g, ev_w_in, hgrn_lb_logits, hgrn_norm_g, ev_w_out, od_w_in, gla_w_gate_up, gla_b_gate, gla_norm_g, od_w_out, norm_ffn_g, ffn_w_up, ffn_conv_w, ffn_conv_b, ffn_w_down, norm_ple_g, ple_w_gate, ple_w_proj, norm_out_g):
    lb = jnp.cumsum(jax.nn.softmax(hgrn_lb_logits.astype(F32), axis=1), axis=1)
    lb = lb - lb[:, :1]
    ev_w, od_w, od_w2, od_b2 = _prep_weights(ev_w_in, od_w_in, gla_w_gate_up, gla_b_gate)
    prm = dict(norm_mix_g=norm_mix_g, ev_w=ev_w, lb=lb, hgrn_norm_g=hgrn_norm_g, ev_w_out=ev_w_out.astype(BF16),
               od_w=od_w, od_w2=od_w2, od_b2=od_b2, gla_norm_g=gla_norm_g, od_w_out=od_w_out.astype(BF16),
               norm_ffn_g=norm_ffn_g, ffn_w_up=ffn_w_up.astype(BF16), ffn_conv_w=ffn_conv_w,
               ffn_conv_b=ffn_conv_b, ffn_w_down=ffn_w_down.astype(BF16), norm_ple_g=norm_ple_g,
               ple_w_gate=ple_w_gate.astype(BF16), ple_w_proj=ple_w_proj.astype(BF16), norm_out_g=norm_out_g)
    return _trunk(x_prompt, p_prompt, prm), _trunk(x_sample, p_sample, prm)
```

```python
import functools

import numpy as np
import jax
import jax.numpy as jnp
from jax import lax
from jax.experimental import pallas as pl
from jax.experimental.pallas import tpu as pltpu

F32 = jnp.float32
BF16 = jnp.bfloat16

D_MODEL = 1024
DEPTH = 4
PLE_DIM = 256
EPS = 1e-6
HEADS = 4
HD = 128
A_W = HEADS * HD
B_GROUPS = ((128, 1), (512, 4), (2048, 16))
N_GROUPS = len(B_GROUPS)
QKV_W = 3 * A_W
ATT_Q = 128
ATT_CLASSES = 4
ALIBI_MAX_EXP = 8.0
C_KD = 512
C_VD = 1024
C_HDV = C_VD // HEADS
C_RANK = 16
GATE_NORMALIZER = 16.0
CHUNK = 64
SUB = 16
N_SUB = CHUNK // SUB
SUB8 = 8
CHUNKS_PER_STEP = {True: 8, False: 4}
LOG_DECAY_MIN = -30.0
LOG2E = 1.4426950408889634
SAFE_LOG2_RANGE = 96.0
NEG_INF = -1e30
D_FF = 2816
LSE_W = 128
LSE_REP = LSE_W // HEADS
LR_PAD = 128

ROW_TILE = 512
HALO = 8
FF_CHUNK = 256
VMEM_LIMIT = 56 * 1024 * 1024


def _cparams(*sem):
    return pltpu.CompilerParams(dimension_semantics=sem, vmem_limit_bytes=VMEM_LIMIT)


def _resident(shape):
    nd = len(shape)
    return pl.BlockSpec(shape, lambda *_: (0,) * nd, pipeline_mode=pl.Buffered(1))


def _rms(x, g):
    ms = jnp.mean(x * x, axis=-1, keepdims=True)
    return x * lax.rsqrt(ms + EPS) * g


def _dot(a, b):
    return jnp.dot(a, b, preferred_element_type=F32)


def _dot_nt(a, b):
    return lax.dot_general(a, b, (((1,), (1,)), ((), ())), preferred_element_type=F32)


def _dot_tn(a, b):
    return lax.dot_general(a, b, (((0,), (0,)), ((), ())), preferred_element_type=F32)


def _store_cols(xn, w_ref, out_refs, c0=0, col_chunk=512):
    for o_ref in out_refs:
        n = o_ref.shape[-1]
        for j in range(0, n, col_chunk):
            wj = min(col_chunk, n - j)
            o_ref[:, j:j + wj] = _dot(xn, w_ref[:, c0 + j:c0 + j + wj]).astype(o_ref.dtype)
        c0 += n
    return c0


def _log2_decay(log_f):
    return jnp.maximum(log_f, LOG_DECAY_MIN) * LOG2E


def _in_proj_even_body(x_ref, g_ref, lb_ref, w_ref, a16_ref, a32_ref, g0_ref, g1_ref, g2_ref, u_ref):
    xn = _rms(x_ref[...], g_ref[...]).astype(BF16)
    c0 = _store_cols(xn, w_ref, (a16_ref.at[:, :3 * A_W],))
    for d in range(2):
        lb = lb_ref[:, d * A_W:(d + 1) * A_W]
        sig, nsig = _sigmoid_pair(_dot(xn, w_ref[:, c0:c0 + A_W]))
        c0 += A_W
        a32_ref[:, d * A_W:(d + 1) * A_W] = _log2_decay(jnp.log(lb + (1.0 - lb) * sig))
        a16_ref[:, (3 + d) * A_W:(4 + d) * A_W] = ((1.0 - lb) * nsig).astype(BF16)
    for (_, dil), o_ref in zip(B_GROUPS, (g0_ref, g1_ref, g2_ref)):
        n = ROW_TILE // dil
        for j in range(3):
            u = _dot(xn, w_ref[:, c0:c0 + A_W])
            c0 += A_W
            if dil == 1:
                o_ref[:, j * A_W:(j + 1) * A_W] = u.astype(BF16)
            else:
                for c in range(HEADS):
                    u_ref[c] = u[:, c * HD:(c + 1) * HD]
                for r in range(dil):
                    for c in range(HEADS):
                        col = (3 * r + j) * A_W + c * HD
                        o_ref[:, col:col + HD] = u_ref[c, pl.ds(r, n, stride=dil), :].astype(BF16)


def _in_proj_odd_body(x_ref, g_ref, w_ref, w2_ref, b2_ref, c16_ref, g2_ref):
    xn = _rms(x_ref[...], g_ref[...]).astype(BF16)
    c16_ref[:, :C_KD] = (_dot(xn, w_ref[:, :C_KD]) * (HD ** -0.5)).astype(BF16)
    c0 = _store_cols(xn, w_ref, (c16_ref.at[:, C_KD:],), c0=C_KD)
    lr = _dot(xn, w_ref[:, c0:c0 + LR_PAD]).astype(BF16)
    n = g2_ref.shape[-1]
    for j in range(0, n, 512):
        logit = _dot(lr, w2_ref[:, j:j + 512]) + b2_ref[:, j:j + 512]
        g2_ref[:, j:j + 512] = _log2_decay(_log_sigmoid(logit) * (1.0 / GATE_NORMALIZER))


def _in_proj_even(x, g, lb, w):
    m = x.shape[0]
    outs = [((m, 5 * A_W), (ROW_TILE, 5 * A_W), BF16), ((m, 2 * A_W), (ROW_TILE, 2 * A_W), F32)]
    for _, dil in B_GROUPS:
        outs.append(((m // dil, dil * QKV_W), (ROW_TILE // dil, dil * QKV_W), BF16))
    return pl.pallas_call(
        _in_proj_even_body,
        grid=(m // ROW_TILE,),
        in_specs=[pl.BlockSpec((ROW_TILE, D_MODEL), lambda i: (i, 0)),
                  pl.BlockSpec((1, D_MODEL), lambda i: (0, 0)),
                  pl.BlockSpec((1, 2 * A_W), lambda i: (0, 0)),
                  _resident(w.shape)],
        out_specs=[pl.BlockSpec(blk, lambda i: (i, 0)) for _, blk, _ in outs],
        out_shape=[jax.ShapeDtypeStruct(shp, dt) for shp, _, dt in outs],
        scratch_shapes=[pltpu.VMEM((HEADS, ROW_TILE, HD), F32)],
        compiler_params=_cparams("parallel"),
        name="in_proj_even",
    )(x, g, lb, w)


def _in_proj_odd(x, g, w, w2, b2):
    m = x.shape[0]
    n16 = 2 * C_KD + 2 * C_VD
    return pl.pallas_call(
        _in_proj_odd_body,
        grid=(m // ROW_TILE,),
        in_specs=[pl.BlockSpec((ROW_TILE, D_MODEL), lambda i: (i, 0)),
                  pl.BlockSpec((1, D_MODEL), lambda i: (0, 0)),
                  _resident(w.shape), _resident(w2.shape),
                  pl.BlockSpec((1, 2 * C_KD), lambda i: (0, 0))],
        out_specs=[pl.BlockSpec((ROW_TILE, n16), lambda i: (i, 0)),
                   pl.BlockSpec((ROW_TILE, 2 * C_KD), lambda i: (i, 0))],
        out_shape=[jax.ShapeDtypeStruct((m, n16), BF16), jax.ShapeDtypeStruct((m, 2 * C_KD), F32)],
        compiler_params=_cparams("parallel"),
        name="in_proj_odd",
    )(x, g, w, w2, b2)


def _off_ranges(rev):
    out = []
    for j in range(N_SUB):
        lo, hi = (0, SUB * j) if rev else (SUB * (j + 1), CHUNK)
        if hi > lo:
            out.append((j, lo, hi))
    return out


def _gla_constants(rev):
    t = np.arange(CHUNK)
    tri = (t[None, :] >= t[:, None]) if rev else (t[None, :] <= t[:, None])
    causal = tri
    same8 = (t[None, :] // SUB8) == (t[:, None] // SUB8)
    same16 = (t[None, :] // SUB) == (t[:, None] // SUB)
    f = lambda a: jnp.asarray(a.astype(np.float32))
    return (jnp.asarray(np.concatenate([tri, tri], axis=1).astype(np.float32), BF16),
            f(same16 & causal), f(same8 & causal), f(same16 & ~same8 & causal))


def _gla_shared_constants():
    e = np.zeros((SUB8 * HD, CHUNK), np.float32)
    for s in range(SUB8):
        e[s * HD:(s + 1) * HD, s::SUB8] = 1.0
    r = np.arange(ROW_TILE)
    bsel = (r[None, :] // SUB == np.arange(ROW_TILE // SUB)[:, None]).astype(np.float32)
    return jnp.asarray(e, BF16), jnp.asarray(bsel, BF16)


def _rows(ref, r, n):
    return jnp.broadcast_to(ref[pl.ds(r, 1), :], (n, HD))


def _gla_cumsum(k, g2, b_ref, k_ref, tri_ref, fast):
    g_hi = g2.astype(BF16)
    g_lo = (g2 - g_hi.astype(F32)).astype(BF16)
    b = _dot(tri_ref[...], jnp.concatenate([g_hi, g_lo], axis=0))
    b_ref[...] = b
    if not fast:
        k_ref[...] = k
    return b


def _gla_scores(qs, k, v, b, st_ref, b_ref, k_ref, emat_ref, rev, fast):
    b_tot = b_ref[pl.ds(0 if rev else CHUNK - 1, 1), :]

    st = st_ref[...]
    o = _dot_nt((qs * jnp.exp2(b)).astype(BF16), st.astype(BF16))
    st_ref[...] = st * jnp.exp2(b_tot) + _dot_tn(v, (k * jnp.exp2(b_tot - b)).astype(BF16))

    edge16 = [SUB * i + (0 if rev else SUB - 1) for i in range(N_SUB)]
    e_ko = jnp.exp2(jnp.concatenate([_rows(b_ref, e, SUB) for e in edge16], axis=0) - b)
    k_off = k * e_ko
    zq = jnp.zeros((SUB, HD), F32)
    q_parts, k_parts = [], []
    for j, lo, hi in _off_ranges(rev):
        q_off = qs[lo:hi] * jnp.exp2(jnp.minimum(b[lo:hi] - _rows(b_ref, edge16[j], 1), 0.0))
        q_parts.append(jnp.concatenate([zq] * (lo // SUB) + [q_off] + [zq] * ((CHUNK - hi) // SUB), axis=0))
        k_parts.append(jnp.concatenate([zq] * j + [k_off[SUB * j:SUB * (j + 1)]] + [zq] * (N_SUB - 1 - j),
                                       axis=0))
    a_16 = _dot_nt(jnp.concatenate(q_parts, axis=1).astype(BF16),
                   jnp.concatenate(k_parts, axis=1).astype(BF16))

    if fast:
        return o, a_16, _dot_nt((qs * (1.0 / e_ko)).astype(BF16), k_off.astype(BF16))

    edge8 = [SUB * i + (SUB8 if rev else SUB8 - 1) for i in range(N_SUB)]
    d8 = b - jnp.concatenate([_rows(b_ref, e, SUB) for e in edge8], axis=0)
    second = (lax.broadcasted_iota(jnp.int32, (CHUNK, HD), 0) % SUB) >= SUB8
    q_side = jnp.logical_not(second) if rev else second
    w8 = jnp.exp2(jnp.minimum(jnp.where(q_side, d8, -d8), 0.0))
    a_8 = _dot_nt(jnp.where(q_side, qs * w8, 0.0).astype(BF16), jnp.where(q_side, 0.0, k * w8).astype(BF16))

    slabs = []
    for s in range(SUB8):
        rows = [SUB8 * i + s for i in range(CHUNK // SUB8)]
        b_s = jnp.concatenate([_rows(b_ref, r, SUB8) for r in rows], axis=0)
        k_s = jnp.concatenate([_rows(k_ref, r, SUB8) for r in rows], axis=0)
        slabs.append((qs * k_s * jnp.exp2(jnp.minimum(b - b_s, 0.0))).astype(BF16))
    a_dg = _dot(jnp.concatenate(slabs, axis=1), emat_ref[...])
    return o, a_16, a_8, a_dg


def _gla_output(scores, v, m_16_ref, m_dg_ref, m_8_ref, fast):
    if fast:
        o, a_16, a_in = scores
        a = a_in * m_16_ref[...] + a_16
    else:
        o, a_16, a_8, a_dg = scores
        a = a_dg * m_dg_ref[...] + a_8 * m_8_ref[...] + a_16
    return o + _dot(a.astype(BF16), v)


def _decay_range_ok(g_ref, bsel_ref, n_tiles):
    sums = [_dot(bsel_ref[...], g_ref[i * ROW_TILE:(i + 1) * ROW_TILE, :].astype(BF16)) for i in range(n_tiles)]
    return jnp.min(functools.reduce(jnp.minimum, sums)) > -SAFE_LOG2_RANGE


def _sigmoid_pair(z):
    e = jnp.exp(-jnp.abs(z))
    r = 1.0 / (1.0 + e)
    er = e * r
    pos = z >= 0.0
    return jnp.where(pos, r, er), jnp.where(pos, er, r)


def _sigmoid(x):
    return 1.0 / (1.0 + jnp.exp(-x))


def _log_sigmoid(x):
    return jnp.minimum(x, 0.0) - jnp.log1p(jnp.exp(-jnp.abs(x)))


def _bidir_body(q_ref, kf_ref, kb_ref, g2f_ref, g2b_ref, v_ref, gate_ref, ng_ref,
                trif_ref, m16f_ref, mdf_ref, m8f_ref, trib_ref, m16b_ref, mdb_ref, m8b_ref, e_ref, bsel_ref,
                o_ref, accf_ref, accb_ref, sf_ref, sb_ref, stage_ref, *, nc):
    def run(fast):
        per = CHUNKS_PER_STEP[fast]
        sf_ref[...] = jnp.zeros_like(sf_ref)
        sb_ref[...] = jnp.zeros_like(sb_ref)

        def step(c, carry):
            streams = []
            for u, rev in [(u, rev) for u in range(per) for rev in (False, True)]:
                cf = c * per + u
                sl = pl.ds(pl.multiple_of((nc - 1 - cf if rev else cf) * CHUNK, CHUNK), CHUNK)
                b_ref = stage_ref.at[4 * u + 2 * int(rev)]
                kst_ref = stage_ref.at[4 * u + 2 * int(rev) + 1]
                k = (kb_ref if rev else kf_ref)[sl, :].astype(F32)
                qs = q_ref[sl, :].astype(F32)
                b = _gla_cumsum(k, (g2b_ref if rev else g2f_ref)[sl, :], b_ref, kst_ref,
                                trib_ref if rev else trif_ref, fast)
                streams.append((rev, sl, qs, k, b, b_ref, kst_ref))
            scores = [_gla_scores(qs, k, v_ref[sl, :], b, sb_ref if rev else sf_ref, b_ref, kst_ref, e_ref,
                                  rev, fast)
                      for rev, sl, qs, k, b, b_ref, kst_ref in streams]
            for (rev, sl, *_), sc in zip(streams, scores):
                masks = (m16b_ref, mdb_ref, m8b_ref) if rev else (m16f_ref, mdf_ref, m8f_ref)
                (accb_ref if rev else accf_ref)[sl, :] = _gla_output(sc, v_ref[sl, :], *masks, fast)
            return carry

        lax.fori_loop(0, nc // per, step, 0)

    n_tiles = (nc * CHUNK) // ROW_TILE
    ok = jnp.logical_and(_decay_range_ok(g2f_ref, bsel_ref, n_tiles), _decay_range_ok(g2b_ref, bsel_ref, n_tiles))
    pl.when(ok)(functools.partial(run, True))
    pl.when(jnp.logical_not(ok))(functools.partial(run, False))

    ng = ng_ref[...]

    def finish(i, carry):
        sl = pl.ds(pl.multiple_of(i * ROW_TILE, ROW_TILE), ROW_TILE)
        o = accf_ref[sl, :] + accb_ref[sl, :]
        gt = gate_ref[sl, :].astype(F32)
        o_ref[sl, :] = (_rms(o, ng) * (gt * _sigmoid(gt))).astype(o_ref.dtype)
        return carry

    lax.fori_loop(0, (nc * CHUNK) // ROW_TILE, finish, 0)


def _bidir_call(name, bsz, t, dv, args, in_specs):
    consts = (*_gla_constants(False), *_gla_constants(True), *_gla_shared_constants())
    const = lambda shape: pl.BlockSpec(shape, lambda b, h: (0,) * len(shape))
    return pl.pallas_call(
        functools.partial(_bidir_body, nc=t // CHUNK),
        grid=(bsz, HEADS),
        in_specs=in_specs + [const(c.shape) for c in consts],
        out_specs=pl.BlockSpec((None, t, dv), lambda b, h: (b, 0, h)),
        out_shape=jax.ShapeDtypeStruct((bsz, t, HEADS * dv), BF16),
        scratch_shapes=[pltpu.VMEM((t, dv), F32), pltpu.VMEM((t, dv), F32),
                        pltpu.VMEM((dv, HD), F32), pltpu.VMEM((dv, HD), F32)]
                       + [pltpu.VMEM((4 * max(CHUNKS_PER_STEP.values()), CHUNK, HD), F32)],
        compiler_params=_cparams("parallel", "parallel"),
        name=name,
    )(*args, *consts)


def _hgrn_mixer(a16, a32, norm_g):
    bsz, t, _ = a16.shape
    col = lambda off: pl.BlockSpec((None, t, HD), lambda b, h: (b, 0, off * HEADS + h))
    return _bidir_call("hgrn2_bidir", bsz, t, HD,
                       (a16, a16, a16, a32, a32, a16, a16, norm_g),
                       [col(0), col(3), col(4), col(0), col(1), col(1), col(2),
                        pl.BlockSpec((1, HD), lambda b, h: (0, h))])


def _gla_mixer(c16, g2, norm_g):
    bsz, t, _ = c16.shape
    col = lambda off: pl.BlockSpec((None, t, HD), lambda b, h: (b, 0, off * HEADS + h))
    wide = lambda off: pl.BlockSpec((None, t, C_HDV), lambda b, h: (b, 0, off + h))
    v_off = 2 * C_KD // C_HDV
    return _bidir_call("gla_bidir", bsz, t, C_HDV,
                       (c16, c16, c16, g2, g2, c16, c16, norm_g),
                       [col(0), col(1), col(1), col(0), col(1), wide(v_off), wide(v_off + HEADS),
                        pl.BlockSpec((1, C_HDV), lambda b, h: (0, h))])


def _attn_window(length, radius):
    win = min(ATT_Q + 2 * radius, length)
    return win, (0, -radius, ATT_Q - win)


def _attn_bias(gi, length):
    wsize, dil = B_GROUPS[gi]
    radius = wsize // (2 * dil)
    win, offsets = _attn_window(length, radius)
    n = N_GROUPS * HEADS
    slopes = np.array([2.0 ** (-ALIBI_MAX_EXP * (gi * HEADS + h + 1) / n) for h in range(HEADS)])
    dist = np.abs(np.array(offsets)[:, None, None] + np.arange(win)[None, None, :] - np.arange(ATT_Q)[None, :, None])
    bias = np.where(dist[:, None] <= radius, -slopes[None, :, None, None] * (dil * dist[:, None]), NEG_INF)
    return jnp.asarray(bias, F32)


def _attn_body(x_ref, bias_ref, o_ref, l_ref, *, length, radius, classes):
    win, _ = _attn_window(length, radius)
    n_tiles = length // ATT_Q
    per = 2 if (classes == 1 and n_tiles % 2 == 0) else 1
    scale = HD ** -0.5

    def step(i, carry):
        probs = []
        for u in range(per):
            tile = i * per + u
            m0 = pl.multiple_of(tile * ATT_Q, ATT_Q)
            k0 = pl.multiple_of(jnp.clip(m0 - radius, 0, length - win), radius)
            place = jnp.where(tile == 0, 0, jnp.where(tile == n_tiles - 1, 2, 1))
            probs += [(m0, k0, place, r, h) for r in range(classes) for h in range(HEADS)]
        col = lambda r, j, h: pl.ds((3 * r + j) * A_W + h * HD, HD)
        s = [_dot_nt(x_ref[pl.ds(m0, ATT_Q), col(r, 0, h)], x_ref[pl.ds(k0, win), col(r, 1, h)])
             for m0, k0, _, r, h in probs]
        s = [x * scale + bias_ref[place, h] for x, (_, _, place, _, h) in zip(s, probs)]
        mx = [jnp.max(x, axis=-1, keepdims=True) for x in s]
        p = [jnp.exp(x - m) for x, m in zip(s, mx)]
        den = [jnp.sum(x, axis=-1, keepdims=True) for x in p]
        o = [_dot(x.astype(BF16), x_ref[pl.ds(k0, win), col(r, 2, h)]) for x, (_, k0, _, r, h) in zip(p, probs)]
        for x, d, m, (m0, _, _, r, h) in zip(o, den, mx, probs):
            o_ref[pl.ds(m0, ATT_Q), pl.ds(r * A_W + h * HD, HD)] = (x * (1.0 / d)).astype(o_ref.dtype)
            l_ref[pl.ds(m0, ATT_Q), pl.ds(r * LSE_W + h * LSE_REP, LSE_REP)] = jnp.broadcast_to(
                m + jnp.log(d), (ATT_Q, LSE_REP))
        return carry

    lax.fori_loop(0, n_tiles // per, step, 0)


def _dilated_attention(qkv, gi):
    bsz, length, _ = qkv.shape
    wsize, dil = B_GROUPS[gi]
    classes = min(dil, ATT_CLASSES)
    bias = _attn_bias(gi, length)
    return pl.pallas_call(
        functools.partial(_attn_body, length=length, radius=wsize // (2 * dil), classes=classes),
        grid=(bsz, dil // classes),
        in_specs=[pl.BlockSpec((None, length, classes * QKV_W), lambda b, r: (b, 0, r)),
                  pl.BlockSpec(bias.shape, lambda b, r: (0, 0, 0, 0))],
        out_specs=[pl.BlockSpec((None, length, classes * A_W), lambda b, r: (b, 0, r)),
                   pl.BlockSpec((None, length, classes * LSE_W), lambda b, r: (b, 0, r))],
        out_shape=[jax.ShapeDtypeStruct((bsz, length, dil * A_W), BF16),
                   jax.ShapeDtypeStruct((bsz, length, dil * LSE_W), F32)],
        compiler_params=_cparams("parallel", "parallel"),
        name=f"dilated_attn_g{gi}",
    )(qkv, bias)


def _lse_expansion():
    e = np.zeros((LSE_W, A_W), np.float32)
    for h in range(HEADS):
        e[h * LSE_REP, h * HD:(h + 1) * HD] = 1.0
    return jnp.asarray(e, BF16)


def _out_even_body(h_ref, a_ref, o0_ref, o1_ref, o2_ref, l0_ref, l1_ref, l2_ref, e_ref, w_ref, y_ref,
                   ob1_ref, ob2_ref, lb1_ref, lb2_ref):
    def natural(o_ref, l_ref, ob_ref, lb_ref, dil):
        if dil == 1:
            return o_ref[...].astype(F32), l_ref[...]
        n = ROW_TILE // dil
        for r in range(dil):
            for c in range(HEADS):
                col = r * A_W + c * HD
                ob_ref[c, pl.ds(r, n, stride=dil), :] = o_ref[:, col:col + HD].astype(F32)
            lb_ref[pl.ds(r, n, stride=dil), :] = l_ref[:, r * LSE_W:(r + 1) * LSE_W]
        return jnp.concatenate([ob_ref[c] for c in range(HEADS)], axis=1), lb_ref[...]

    parts = [natural(o0_ref, l0_ref, None, None, B_GROUPS[0][1]),
             natural(o1_ref, l1_ref, ob1_ref, lb1_ref, B_GROUPS[1][1]),
             natural(o2_ref, l2_ref, ob2_ref, lb2_ref, B_GROUPS[2][1])]
    lses = [l for _, l in parts]
    mx = jnp.maximum(jnp.maximum(lses[0], lses[1]), lses[2])
    es = [jnp.exp(l - mx) for l in lses]
    inv = 1.0 / (es[0] + es[1] + es[2])
    emat = e_ref[...]
    mixed_b = None
    for e_g, (o_g, _) in zip(es, parts):
        alpha = (e_g * inv).astype(BF16)
        term = _dot(alpha, emat) * o_g
        mixed_b = term if mixed_b is None else mixed_b + term
    y_ref[...] = (h_ref[...] + _dot(a_ref[...], w_ref[:A_W, :]) + _dot(mixed_b.astype(BF16), w_ref[A_W:, :]))


def _out_even(h, out_a, os_, lses, w):
    m = h.shape[0]
    emat = _lse_expansion()
    row = lambda n: pl.BlockSpec((ROW_TILE, n), lambda i: (i, 0))
    grp = lambda width: [pl.BlockSpec((ROW_TILE // dil, dil * width), lambda i: (i, 0)) for _, dil in B_GROUPS]
    return pl.pallas_call(
        _out_even_body,
        grid=(m // ROW_TILE,),
        in_specs=[row(D_MODEL), row(A_W)] + grp(A_W) + grp(LSE_W) + [_resident(emat.shape), _resident(w.shape)],
        out_specs=row(D_MODEL),
        out_shape=jax.ShapeDtypeStruct((m, D_MODEL), F32),
        scratch_shapes=[pltpu.VMEM((HEADS, ROW_TILE, HD), F32), pltpu.VMEM((HEADS, ROW_TILE, HD), F32),
                        pltpu.VMEM((ROW_TILE, LSE_W), F32), pltpu.VMEM((ROW_TILE, LSE_W), F32)],
        compiler_params=_cparams("parallel"),
        name="out_proj_even",
    )(h, out_a, *os_, *lses, emat, w)


def _out_odd_body(h_ref, a_ref, w_ref, y_ref):
    y_ref[...] = h_ref[...] + _dot(a_ref[...], w_ref[...])


def _out_odd(h, mixed, w):
    m = h.shape[0]
    row = lambda n: pl.BlockSpec((ROW_TILE, n), lambda i: (i, 0))
    return pl.pallas_call(
        _out_odd_body,
        grid=(m // ROW_TILE,),
        in_specs=[row(D_MODEL), row(C_VD), _resident(w.shape)],
        out_specs=row(D_MODEL),
        out_shape=jax.ShapeDtypeStruct((m, D_MODEL), F32),
        compiler_params=_cparams("parallel"),
        name="out_proj_odd",
    )(h, mixed, w)


def _ffn_body(hp_ref, h_ref, hn_ref, p_ref, gf_ref, wup_ref, cw_ref, cb_ref, wdn_ref, gp_ref, wg_ref, wp_ref,
              go_ref, y_ref, xe_ref, act_ref, *, tiles_per_seq, final):
    i = pl.program_id(0) % tiles_per_seq
    gf = gf_ref[...]
    h = h_ref[...]
    keep_prev = jnp.where(i > 0, 1.0, 0.0).astype(F32)
    keep_next = jnp.where(i < tiles_per_seq - 1, 1.0, 0.0).astype(F32)
    xe_ref[0:HALO, :] = _rms(hp_ref[...], gf) * keep_prev
    xe_ref[HALO:HALO + ROW_TILE, :] = _rms(h, gf)
    xe_ref[HALO + ROW_TILE:, :] = _rms(hn_ref[...], gf) * keep_next
    xe = xe_ref[...].astype(BF16)
    rows = ROW_TILE + 2 * HALO

    def conv(u, c):
        w = cw_ref[:, c:c + FF_CHUNK]
        prev = pltpu.roll(u, 1, 0)[HALO:HALO + ROW_TILE]
        nxt = pltpu.roll(u, rows - 1, 0)[HALO:HALO + ROW_TILE]
        return (prev * w[0:1] + u[HALO:HALO + ROW_TILE] * w[1:2] + nxt * w[2:3] + cb_ref[:, c:c + FF_CHUNK])

    for c in range(0, D_FF, FF_CHUNK):
        a = conv(_dot(xe, wup_ref[:, c:c + FF_CHUNK]), c)
        gt = conv(_dot(xe, wup_ref[:, D_FF + c:D_FF + c + FF_CHUNK]), D_FF + c)
        act_ref[:, c:c + FF_CHUNK] = (a * (0.5 * gt * (1.0 + lax.erf(gt * (2.0 ** -0.5))))).astype(BF16)
    h2 = h + _dot(act_ref[...], wdn_ref[...])
    sig = _sigmoid(_dot(_rms(h2, gp_ref[...]).astype(BF16), wg_ref[...]))
    h3 = h2 + sig * _dot(p_ref[...].astype(BF16), wp_ref[...])
    y_ref[...] = _rms(h3, go_ref[...]) if final else h3


def _ffn_ple(h, p, t, gf, wup, cw, cb, wdn, gp, wg, wp, go, final):
    m = h.shape[0]
    per_tile = ROW_TILE // HALO
    last_blk = m // HALO - 1
    vec = lambda n: pl.BlockSpec((1, n), lambda i: (0, 0))
    return pl.pallas_call(
        functools.partial(_ffn_body, tiles_per_seq=t // ROW_TILE, final=final),
        grid=(m // ROW_TILE,),
        in_specs=[pl.BlockSpec((HALO, D_MODEL), lambda i: (jnp.maximum(i * per_tile - 1, 0), 0)),
                  pl.BlockSpec((ROW_TILE, D_MODEL), lambda i: (i, 0)),
                  pl.BlockSpec((HALO, D_MODEL), lambda i: (jnp.minimum((i + 1) * per_tile, last_blk), 0)),
                  pl.BlockSpec((ROW_TILE, PLE_DIM), lambda i: (i, 0)),
                  vec(D_MODEL), _resident(wup.shape), _resident(cw.shape), vec(2 * D_FF), _resident(wdn.shape),
                  vec(D_MODEL), _resident(wg.shape), _resident(wp.shape), vec(D_MODEL)],
        out_specs=pl.BlockSpec((ROW_TILE, D_MODEL), lambda i: (i, 0)),
        out_shape=jax.ShapeDtypeStruct((m, D_MODEL), F32),
        scratch_shapes=[pltpu.VMEM((ROW_TILE + 2 * HALO, D_MODEL), F32), pltpu.VMEM((ROW_TILE, D_FF), BF16)],
        compiler_params=_cparams("parallel"),
        name="conv_ffn_ple",
    )(h, h, h, p, gf, wup, cw, cb, wdn, gp, wg, wp, go)


def _prep_weights(ev_w_in, od_w_in, gla_w_gate_up, gla_b_gate):
    aq, zf, zb, ai, ag, bqkv = jnp.split(ev_w_in, [A_W, 2 * A_W, 3 * A_W, 4 * A_W, 5 * A_W], axis=-1)
    ev_w = jnp.concatenate([aq, ai, ag, zf, zb, bqkv], axis=-1).astype(BF16)
    od_w = jnp.pad(od_w_in, ((0, 0), (0, 0), (0, LR_PAD - 2 * C_RANK))).astype(BF16)
    n_odd = od_w_in.shape[0]
    w2 = jnp.zeros((n_odd, LR_PAD, 2 * C_KD), F32)
    w2 = w2.at[:, :C_RANK, :C_KD].set(gla_w_gate_up[:, 0]).at[:, C_RANK:2 * C_RANK, C_KD:].set(gla_w_gate_up[:, 1])
    b2 = gla_b_gate.reshape(n_odd, 1, 2 * C_KD)
    return ev_w, od_w, w2.astype(BF16), b2


def _trunk(x, p, prm):
    bsz, t, _ = x.shape
    m = bsz * t
    h = x.reshape(m, D_MODEL)
    for l in range(DEPTH):
        g_mix = prm["norm_mix_g"][l][None]
        if l % 2 == 0:
            e = l // 2
            lb = jnp.concatenate([prm["lb"][0, e], prm["lb"][1, e]])[None]
            a16, a32, *qkv = _in_proj_even(h, g_mix, lb, prm["ev_w"][e])
            out_a = _hgrn_mixer(a16.reshape(bsz, t, -1), a32.reshape(bsz, t, -1), prm["hgrn_norm_g"][e][None])
            att = [_dilated_attention(x_g.reshape(bsz, t // dil, dil * QKV_W), gi)
                   for gi, (x_g, (_, dil)) in enumerate(zip(qkv, B_GROUPS))]
            h = _out_even(h, out_a.reshape(m, A_W),
                          [o.reshape(m // dil, dil * A_W) for (o, _), (_, dil) in zip(att, B_GROUPS)],
                          [l_.reshape(m // dil, dil * LSE_W) for (_, l_), (_, dil) in zip(att, B_GROUPS)],
                          prm["ev_w_out"][e])
        else:
            o = l // 2
            c16, lg = _in_proj_odd(h, g_mix, prm["od_w"][o], prm["od_w2"][o], prm["od_b2"][o])
            mixed = _gla_mixer(c16.reshape(bsz, t, -1), lg.reshape(bsz, t, -1), prm["gla_norm_g"][o][None])
            h = _out_odd(h, mixed.reshape(m, C_VD), prm["od_w_out"][o])
        h = _ffn_ple(h, p[l].reshape(m, PLE_DIM), t, prm["norm_ffn_g"][l][None], prm["ffn_w_up"][l],
                     prm["ffn_conv_w"][l], prm["ffn_conv_b"][l][None], prm["ffn_w_down"][l],
                     prm["norm_ple_g"][l][None], prm["ple_w_gate"][l], prm["ple_w_proj"][l],
                     prm["norm_out_g"][None], final=(l == DEPTH - 1))
    return h.reshape(bsz, t, D_MODEL)


def kernel(x_prompt, x_sample, p_prompt, p_sample, norm_mix_g, ev_w_in, hgrn_lb_logits, hgrn_norm_g, ev_w_out, od_w_in, gla_w_gate_up, gla_b_gate, gla_norm_g, od_w_out, norm_ffn_g, ffn_w_up, ffn_conv_w, ffn_conv_b, ffn_w_down, norm_ple_g, ple_w_gate, ple_w_proj, norm_out_g):
    lb = jnp.cumsum(jax.nn.softmax(hgrn_lb_logits.astype(F32), axis=1), axis=1)
    lb = lb - lb[:, :1]
    ev_w, od_w, od_w2, od_b2 = _prep_weights(ev_w_in, od_w_in, gla_w_gate_up, gla_b_gate)
    prm = dict(norm_mix_g=norm_mix_g, ev_w=ev_w, lb=lb, hgrn_norm_g=hgrn_norm_g, ev_w_out=ev_w_out.astype(BF16),
               od_w=od_w, od_w2=od_w2, od_b2=od_b2, gla_norm_g=gla_norm_g, od_w_out=od_w_out.astype(BF16),
               norm_ffn_g=norm_ffn_g, ffn_w_up=ffn_w_up.astype(BF16), ffn_conv_w=ffn_conv_w,
               ffn_conv_b=ffn_conv_b, ffn_w_down=ffn_w_down.astype(BF16), norm_ple_g=norm_ple_g,
               ple_w_gate=ple_w_gate.astype(BF16), ple_w_proj=ple_w_proj.astype(BF16), norm_out_g=norm_out_g)
    return _trunk(x_prompt, p_prompt, prm), _trunk(x_sample, p_sample, prm)
```

```python
import functools

import numpy as np
import jax
import jax.numpy as jnp
from jax import lax
from jax.experimental import pallas as pl
from jax.experimental.pallas import tpu as pltpu

F32 = jnp.float32
BF16 = jnp.bfloat16

D_MODEL = 1024
DEPTH = 4
PLE_DIM = 256
EPS = 1e-6
HEADS = 4
HD = 128
A_W = HEADS * HD
B_GROUPS = ((128, 1), (512, 4), (2048, 16))
N_GROUPS = len(B_GROUPS)
QKV_W = 3 * A_W
ATT_Q = 128
ATT_CLASSES = 4
ALIBI_MAX_EXP = 8.0
C_KD = 512
C_VD = 1024
C_HDV = C_VD // HEADS
C_RANK = 16
GATE_NORMALIZER = 16.0
CHUNK = 64
SUB = 16
N_SUB = CHUNK // SUB
SUB8 = 8
CHUNKS_PER_STEP = {"chunk": 8, "block": 8, "pair": 4}
LOG_DECAY_MIN = -30.0
LOG2E = 1.4426950408889634
SAFE_LOG2_RANGE = {"chunk": 116.0, "block": 96.0}
NEG_INF = -1e30
D_FF = 2816
LSE_W = 128
LSE_REP = LSE_W // HEADS
LR_PAD = 128

ROW_TILE = 512
HALO = 8
FF_CHUNK = 256
VMEM_LIMIT = 56 * 1024 * 1024


def _cparams(*sem):
    return pltpu.CompilerParams(dimension_semantics=sem, vmem_limit_bytes=VMEM_LIMIT)


def _resident(shape):
    nd = len(shape)
    return pl.BlockSpec(shape, lambda *_: (0,) * nd, pipeline_mode=pl.Buffered(1))


def _rms(x, g):
    ms = jnp.mean(x * x, axis=-1, keepdims=True)
    return x * lax.rsqrt(ms + EPS) * g


def _dot(a, b):
    return jnp.dot(a, b, preferred_element_type=F32)


def _dot_nt(a, b):
    return lax.dot_general(a, b, (((1,), (1,)), ((), ())), preferred_element_type=F32)


def _dot_tn(a, b):
    return lax.dot_general(a, b, (((0,), (0,)), ((), ())), preferred_element_type=F32)


def _store_cols(xn, w_ref, out_refs, c0=0, col_chunk=512):
    for o_ref in out_refs:
        n = o_ref.shape[-1]
        for j in range(0, n, col_chunk):
            wj = min(col_chunk, n - j)
            o_ref[:, j:j + wj] = _dot(xn, w_ref[:, c0 + j:c0 + j + wj]).astype(o_ref.dtype)
        c0 += n
    return c0


def _log2_decay(log_f):
    return jnp.maximum(log_f, LOG_DECAY_MIN) * LOG2E


def _in_proj_even_body(x_ref, g_ref, lb_ref, w_ref, a16_ref, a32_ref, g0_ref, g1_ref, g2_ref, u_ref):
    xn = _rms(x_ref[...], g_ref[...]).astype(BF16)
    c0 = _store_cols(xn, w_ref, (a16_ref.at[:, :3 * A_W],))
    for d in range(2):
        lb = lb_ref[:, d * A_W:(d + 1) * A_W]
        sig, nsig = _sigmoid_pair(_dot(xn, w_ref[:, c0:c0 + A_W]))
        c0 += A_W
        a32_ref[:, d * A_W:(d + 1) * A_W] = _log2_decay(jnp.log(lb + (1.0 - lb) * sig))
        a16_ref[:, (3 + d) * A_W:(4 + d) * A_W] = ((1.0 - lb) * nsig).astype(BF16)
    for (_, dil), o_ref in zip(B_GROUPS, (g0_ref, g1_ref, g2_ref)):
        n = ROW_TILE // dil
        for j in range(3):
            u = _dot(xn, w_ref[:, c0:c0 + A_W])
            c0 += A_W
            if dil == 1:
                o_ref[:, j * A_W:(j + 1) * A_W] = u.astype(BF16)
            else:
                for c in range(HEADS):
                    u_ref[c] = u[:, c * HD:(c + 1) * HD]
                for r in range(dil):
                    for c in range(HEADS):
                        col = (3 * r + j) * A_W + c * HD
                        o_ref[:, col:col + HD] = u_ref[c, pl.ds(r, n, stride=dil), :].astype(BF16)


def _in_proj_odd_body(x_ref, g_ref, w_ref, w2_ref, b2_ref, c16_ref, g2_ref):
    xn = _rms(x_ref[...], g_ref[...]).astype(BF16)
    c16_ref[:, :C_KD] = (_dot(xn, w_ref[:, :C_KD]) * (HD ** -0.5)).astype(BF16)
    c0 = _store_cols(xn, w_ref, (c16_ref.at[:, C_KD:],), c0=C_KD)
    lr = _dot(xn, w_ref[:, c0:c0 + LR_PAD]).astype(BF16)
    n = g2_ref.shape[-1]
    for j in range(0, n, 512):
        logit = _dot(lr, w2_ref[:, j:j + 512]) + b2_ref[:, j:j + 512]
        g2_ref[:, j:j + 512] = _log2_decay(_log_sigmoid(logit) * (1.0 / GATE_NORMALIZER))


def _in_proj_even(x, g, lb, w):
    m = x.shape[0]
    outs = [((m, 5 * A_W), (ROW_TILE, 5 * A_W), BF16), ((m, 2 * A_W), (ROW_TILE, 2 * A_W), F32)]
    for _, dil in B_GROUPS:
        outs.append(((m // dil, dil * QKV_W), (ROW_TILE // dil, dil * QKV_W), BF16))
    return pl.pallas_call(
        _in_proj_even_body,
        grid=(m // ROW_TILE,),
        in_specs=[pl.BlockSpec((ROW_TILE, D_MODEL), lambda i: (i, 0)),
                  pl.BlockSpec((1, D_MODEL), lambda i: (0, 0)),
                  pl.BlockSpec((1, 2 * A_W), lambda i: (0, 0)),
                  _resident(w.shape)],
        out_specs=[pl.BlockSpec(blk, lambda i: (i, 0)) for _, blk, _ in outs],
        out_shape=[jax.ShapeDtypeStruct(shp, dt) for shp, _, dt in outs],
        scratch_shapes=[pltpu.VMEM((HEADS, ROW_TILE, HD), F32)],
        compiler_params=_cparams("parallel"),
        name="in_proj_even",
    )(x, g, lb, w)


def _in_proj_odd(x, g, w, w2, b2):
    m = x.shape[0]
    n16 = 2 * C_KD + 2 * C_VD
    return pl.pallas_call(
        _in_proj_odd_body,
        grid=(m // ROW_TILE,),
        in_specs=[pl.BlockSpec((ROW_TILE, D_MODEL), lambda i: (i, 0)),
                  pl.BlockSpec((1, D_MODEL), lambda i: (0, 0)),
                  _resident(w.shape), _resident(w2.shape),
                  pl.BlockSpec((1, 2 * C_KD), lambda i: (0, 0))],
        out_specs=[pl.BlockSpec((ROW_TILE, n16), lambda i: (i, 0)),
                   pl.BlockSpec((ROW_TILE, 2 * C_KD), lambda i: (i, 0))],
        out_shape=[jax.ShapeDtypeStruct((m, n16), BF16), jax.ShapeDtypeStruct((m, 2 * C_KD), F32)],
        compiler_params=_cparams("parallel"),
        name="in_proj_odd",
    )(x, g, w, w2, b2)


def _off_ranges(rev):
    out = []
    for j in range(N_SUB):
        lo, hi = (0, SUB * j) if rev else (SUB * (j + 1), CHUNK)
        if hi > lo:
            out.append((j, lo, hi))
    return out


def _gla_constants(rev):
    t = np.arange(CHUNK)
    tri = (t[None, :] >= t[:, None]) if rev else (t[None, :] <= t[:, None])
    causal = tri
    same8 = (t[None, :] // SUB8) == (t[:, None] // SUB8)
    same16 = (t[None, :] // SUB) == (t[:, None] // SUB)
    f = lambda a: jnp.asarray(a.astype(np.float32))
    return (jnp.asarray(np.concatenate([tri, tri], axis=1).astype(np.float32), BF16),
            f(causal), f(same16 & causal), f(same8 & causal), f(same16 & ~same8 & causal))


def _gla_shared_constants():
    e = np.zeros((SUB8 * HD, CHUNK), np.float32)
    for s in range(SUB8):
        e[s * HD:(s + 1) * HD, s::SUB8] = 1.0
    r = np.arange(ROW_TILE)
    sel = [(r[None, :] // n == np.arange(ROW_TILE // n)[:, None]).astype(np.float32) for n in (SUB, CHUNK)]
    return jnp.asarray(e, BF16), jnp.asarray(np.concatenate(sel, axis=0), BF16)


def _rows(ref, r, n):
    return jnp.broadcast_to(ref[pl.ds(r, 1), :], (n, HD))


def _gla_cumsum(k, g2, b_ref, k_ref, tri_ref, form):
    g_hi = g2.astype(BF16)
    g_lo = (g2 - g_hi.astype(F32)).astype(BF16)
    b = _dot(tri_ref[...], jnp.concatenate([g_hi, g_lo], axis=0))
    b_ref[...] = b
    if form == "pair":
        k_ref[...] = k
    return b


def _gla_scores(qs, k, v, b, st_ref, b_ref, k_ref, emat_ref, rev, form):
    b_tot = b_ref[pl.ds(0 if rev else CHUNK - 1, 1), :]

    st = st_ref[...]
    qd = qs * jnp.exp2(b)
    kd = (k * jnp.exp2(b_tot - b)).astype(BF16)
    o = _dot_nt(qd.astype(BF16), st.astype(BF16))
    st_ref[...] = st * jnp.exp2(b_tot) + _dot_tn(v, kd)
    if form == "chunk":
        return o, _dot_nt((qd * jnp.exp2(-b_tot)).astype(BF16), kd)

    edge16 = [SUB * i + (0 if rev else SUB - 1) for i in range(N_SUB)]
    e_ko = jnp.exp2(jnp.concatenate([_rows(b_ref, e, SUB) for e in edge16], axis=0) - b)
    k_off = k * e_ko
    zq = jnp.zeros((SUB, HD), F32)
    q_parts, k_parts = [], []
    for j, lo, hi in _off_ranges(rev):
        q_off = qs[lo:hi] * jnp.exp2(jnp.minimum(b[lo:hi] - _rows(b_ref, edge16[j], 1), 0.0))
        q_parts.append(jnp.concatenate([zq] * (lo // SUB) + [q_off] + [zq] * ((CHUNK - hi) // SUB), axis=0))
        k_parts.append(jnp.concatenate([zq] * j + [k_off[SUB * j:SUB * (j + 1)]] + [zq] * (N_SUB - 1 - j),
                                       axis=0))
    a_16 = _dot_nt(jnp.concatenate(q_parts, axis=1).astype(BF16),
                   jnp.concatenate(k_parts, axis=1).astype(BF16))

    if form == "block":
        return o, a_16, _dot_nt((qs * (1.0 / e_ko)).astype(BF16), k_off.astype(BF16))

    edge8 = [SUB * i + (SUB8 if rev else SUB8 - 1) for i in range(N_SUB)]
    d8 = b - jnp.concatenate([_rows(b_ref, e, SUB) for e in edge8], axis=0)
    second = (lax.broadcasted_iota(jnp.int32, (CHUNK, HD), 0) % SUB) >= SUB8
    q_side = jnp.logical_not(second) if rev else second
    w8 = jnp.exp2(jnp.minimum(jnp.where(q_side, d8, -d8), 0.0))
    a_8 = _dot_nt(jnp.where(q_side, qs * w8, 0.0).astype(BF16), jnp.where(q_side, 0.0, k * w8).astype(BF16))

    slabs = []
    for s in range(SUB8):
        rows = [SUB8 * i + s for i in range(CHUNK // SUB8)]
        b_s = jnp.concatenate([_rows(b_ref, r, SUB8) for r in rows], axis=0)
        k_s = jnp.concatenate([_rows(k_ref, r, SUB8) for r in rows], axis=0)
        slabs.append((qs * k_s * jnp.exp2(jnp.minimum(b - b_s, 0.0))).astype(BF16))
    a_dg = _dot(jnp.concatenate(slabs, axis=1), emat_ref[...])
    return o, a_16, a_8, a_dg


def _gla_output(scores, v, m_c_ref, m_16_ref, m_dg_ref, m_8_ref, form):
    if form == "chunk":
        o, a_c = scores
        a = jnp.where(m_c_ref[...] > 0.0, a_c, 0.0)
    elif form == "block":
        o, a_16, a_in = scores
        a = a_in * m_16_ref[...] + a_16
    else:
        o, a_16, a_8, a_dg = scores
        a = a_dg * m_dg_ref[...] + a_8 * m_8_ref[...] + a_16
    return o + _dot(a.astype(BF16), v)


def _decay_ranges_ok(g_ref, sel_ref, n_tiles):
    sums = [_dot(sel_ref[...], g_ref[i * ROW_TILE:(i + 1) * ROW_TILE, :].astype(BF16)) for i in range(n_tiles)]
    low = functools.reduce(jnp.minimum, sums)
    n16 = ROW_TILE // SUB
    return (jnp.min(low[n16:]) > -SAFE_LOG2_RANGE["chunk"], jnp.min(low[:n16]) > -SAFE_LOG2_RANGE["block"])


def _sigmoid_pair(z):
    e = jnp.exp(-jnp.abs(z))
    r = 1.0 / (1.0 + e)
    er = e * r
    pos = z >= 0.0
    return jnp.where(pos, r, er), jnp.where(pos, er, r)


def _sigmoid(x):
    return 1.0 / (1.0 + jnp.exp(-x))


def _log_sigmoid(x):
    return jnp.minimum(x, 0.0) - jnp.log1p(jnp.exp(-jnp.abs(x)))


def _bidir_body(q_ref, kf_ref, kb_ref, g2f_ref, g2b_ref, v_ref, gate_ref, ng_ref,
                trif_ref, mcf_ref, m16f_ref, mdf_ref, m8f_ref, trib_ref, mcb_ref, m16b_ref, mdb_ref, m8b_ref,
                e_ref, sel_ref,
                o_ref, accf_ref, accb_ref, sf_ref, sb_ref, stage_ref, *, nc):
    def run(form):
        per = CHUNKS_PER_STEP[form]
        sf_ref[...] = jnp.zeros_like(sf_ref)
        sb_ref[...] = jnp.zeros_like(sb_ref)

        def step(c, carry):
            streams = []
            for u, rev in [(u, rev) for u in range(per) for rev in (False, True)]:
                cf = c * per + u
                sl = pl.ds(pl.multiple_of((nc - 1 - cf if rev else cf) * CHUNK, CHUNK), CHUNK)
                b_ref = stage_ref.at[4 * u + 2 * int(rev)]
                kst_ref = stage_ref.at[4 * u + 2 * int(rev) + 1]
                k = (kb_ref if rev else kf_ref)[sl, :].astype(F32)
                qs = q_ref[sl, :].astype(F32)
                b = _gla_cumsum(k, (g2b_ref if rev else g2f_ref)[sl, :], b_ref, kst_ref,
                                trib_ref if rev else trif_ref, form)
                streams.append((rev, sl, qs, k, b, b_ref, kst_ref))
            scores = [_gla_scores(qs, k, v_ref[sl, :], b, sb_ref if rev else sf_ref, b_ref, kst_ref, e_ref,
                                  rev, form)
                      for rev, sl, qs, k, b, b_ref, kst_ref in streams]
            for (rev, sl, *_), sc in zip(streams, scores):
                masks = (mcb_ref, m16b_ref, mdb_ref, m8b_ref) if rev else (mcf_ref, m16f_ref, mdf_ref, m8f_ref)
                (accb_ref if rev else accf_ref)[sl, :] = _gla_output(sc, v_ref[sl, :], *masks, form)
            return carry

        lax.fori_loop(0, nc // per, step, 0)

    n_tiles = (nc * CHUNK) // ROW_TILE
    chunk_f, block_f = _decay_ranges_ok(g2f_ref, sel_ref, n_tiles)
    chunk_b, block_b = _decay_ranges_ok(g2b_ref, sel_ref, n_tiles)
    chunk_ok = jnp.logical_and(chunk_f, chunk_b)
    block_ok = jnp.logical_and(jnp.logical_and(block_f, block_b), jnp.logical_not(chunk_ok))
    pl.when(chunk_ok)(functools.partial(run, "chunk"))
    pl.when(block_ok)(functools.partial(run, "block"))
    pl.when(jnp.logical_not(jnp.logical_or(chunk_ok, block_ok)))(functools.partial(run, "pair"))

    ng = ng_ref[...]

    def finish(i, carry):
        sl = pl.ds(pl.multiple_of(i * ROW_TILE, ROW_TILE), ROW_TILE)
        o = accf_ref[sl, :] + accb_ref[sl, :]
        gt = gate_ref[sl, :].astype(F32)
        o_ref[sl, :] = (_rms(o, ng) * (gt * _sigmoid(gt))).astype(o_ref.dtype)
        return carry

    lax.fori_loop(0, (nc * CHUNK) // ROW_TILE, finish, 0)


def _bidir_call(name, bsz, t, dv, args, in_specs):
    consts = (*_gla_constants(False), *_gla_constants(True), *_gla_shared_constants())
    const = lambda shape: pl.BlockSpec(shape, lambda b, h: (0,) * len(shape))
    return pl.pallas_call(
        functools.partial(_bidir_body, nc=t // CHUNK),
        grid=(bsz, HEADS),
        in_specs=in_specs + [const(c.shape) for c in consts],
        out_specs=pl.BlockSpec((None, t, dv), lambda b, h: (b, 0, h)),
        out_shape=jax.ShapeDtypeStruct((bsz, t, HEADS * dv), BF16),
        scratch_shapes=[pltpu.VMEM((t, dv), F32), pltpu.VMEM((t, dv), F32),
                        pltpu.VMEM((dv, HD), F32), pltpu.VMEM((dv, HD), F32)]
                       + [pltpu.VMEM((4 * max(CHUNKS_PER_STEP.values()), CHUNK, HD), F32)],
        compiler_params=_cparams("parallel", "parallel"),
        name=name,
    )(*args, *consts)


def _hgrn_mixer(a16, a32, norm_g):
    bsz, t, _ = a16.shape
    col = lambda off: pl.BlockSpec((None, t, HD), lambda b, h: (b, 0, off * HEADS + h))
    return _bidir_call("hgrn2_bidir", bsz, t, HD,
                       (a16, a16, a16, a32, a32, a16, a16, norm_g),
                       [col(0), col(3), col(4), col(0), col(1), col(1), col(2),
                        pl.BlockSpec((1, HD), lambda b, h: (0, h))])


def _gla_mixer(c16, g2, norm_g):
    bsz, t, _ = c16.shape
    col = lambda off: pl.BlockSpec((None, t, HD), lambda b, h: (b, 0, off * HEADS + h))
    wide = lambda off: pl.BlockSpec((None, t, C_HDV), lambda b, h: (b, 0, off + h))
    v_off = 2 * C_KD // C_HDV
    return _bidir_call("gla_bidir", bsz, t, C_HDV,
                       (c16, c16, c16, g2, g2, c16, c16, norm_g),
                       [col(0), col(1), col(1), col(0), col(1), wide(v_off), wide(v_off + HEADS),
                        pl.BlockSpec((1, C_HDV), lambda b, h: (0, h))])


def _attn_window(length, radius):
    win = min(ATT_Q + 2 * radius, length)
    return win, (0, -radius, ATT_Q - win)


def _attn_bias(gi, length):
    wsize, dil = B_GROUPS[gi]
    radius = wsize // (2 * dil)
    win, offsets = _attn_window(length, radius)
    n = N_GROUPS * HEADS
    slopes = np.array([2.0 ** (-ALIBI_MAX_EXP * (gi * HEADS + h + 1) / n) for h in range(HEADS)])
    dist = np.abs(np.array(offsets)[:, None, None] + np.arange(win)[None, None, :] - np.arange(ATT_Q)[None, :, None])
    bias = np.where(dist[:, None] <= radius, -slopes[None, :, None, None] * (dil * dist[:, None]), NEG_INF)
    return jnp.asarray(bias, F32)


def _attn_body(x_ref, bias_ref, o_ref, l_ref, *, length, radius, classes):
    win, _ = _attn_window(length, radius)
    n_tiles = length // ATT_Q
    per = 2 if (classes == 1 and n_tiles % 2 == 0) else 1
    scale = HD ** -0.5

    def step(i, carry):
        probs = []
        for u in range(per):
            tile = i * per + u
            m0 = pl.multiple_of(tile * ATT_Q, ATT_Q)
            k0 = pl.multiple_of(jnp.clip(m0 - radius, 0, length - win), radius)
            place = jnp.where(tile == 0, 0, jnp.where(tile == n_tiles - 1, 2, 1))
            probs += [(m0, k0, place, r, h) for r in range(classes) for h in range(HEADS)]
        col = lambda r, j, h: pl.ds((3 * r + j) * A_W + h * HD, HD)
        s = [_dot_nt(x_ref[pl.ds(m0, ATT_Q), col(r, 0, h)], x_ref[pl.ds(k0, win), col(r, 1, h)])
             for m0, k0, _, r, h in probs]
        s = [x * scale + bias_ref[place, h] for x, (_, _, place, _, h) in zip(s, probs)]
        mx = [jnp.max(x, axis=-1, keepdims=True) for x in s]
        p = [jnp.exp(x - m) for x, m in zip(s, mx)]
        den = [jnp.sum(x, axis=-1, keepdims=True) for x in p]
        o = [_dot(x.astype(BF16), x_ref[pl.ds(k0, win), col(r, 2, h)]) for x, (_, k0, _, r, h) in zip(p, probs)]
        for x, d, m, (m0, _, _, r, h) in zip(o, den, mx, probs):
            o_ref[pl.ds(m0, ATT_Q), pl.ds(r * A_W + h * HD, HD)] = (x * (1.0 / d)).astype(o_ref.dtype)
            l_ref[pl.ds(m0, ATT_Q), pl.ds(r * LSE_W + h * LSE_REP, LSE_REP)] = jnp.broadcast_to(
                m + jnp.log(d), (ATT_Q, LSE_REP))
        return carry

    lax.fori_loop(0, n_tiles // per, step, 0)


def _dilated_attention(qkv, gi):
    bsz, length, _ = qkv.shape
    wsize, dil = B_GROUPS[gi]
    classes = min(dil, ATT_CLASSES)
    bias = _attn_bias(gi, length)
    return pl.pallas_call(
        functools.partial(_attn_body, length=length, radius=wsize // (2 * dil), classes=classes),
        grid=(bsz, dil // classes),
        in_specs=[pl.BlockSpec((None, length, classes * QKV_W), lambda b, r: (b, 0, r)),
                  pl.BlockSpec(bias.shape, lambda b, r: (0, 0, 0, 0))],
        out_specs=[pl.BlockSpec((None, length, classes * A_W), lambda b, r: (b, 0, r)),
                   pl.BlockSpec((None, length, classes * LSE_W), lambda b, r: (b, 0, r))],
        out_shape=[jax.ShapeDtypeStruct((bsz, length, dil * A_W), BF16),
                   jax.ShapeDtypeStruct((bsz, length, dil * LSE_W), F32)],
        compiler_params=_cparams("parallel", "parallel"),
        name=f"dilated_attn_g{gi}",
    )(qkv, bias)


def _lse_expansion():
    e = np.zeros((LSE_W, A_W), np.float32)
    for h in range(HEADS):
        e[h * LSE_REP, h * HD:(h + 1) * HD] = 1.0
    return jnp.asarray(e, BF16)


def _out_even_body(h_ref, a_ref, o0_ref, o1_ref, o2_ref, l0_ref, l1_ref, l2_ref, e_ref, w_ref, y_ref,
                   ob1_ref, ob2_ref, lb1_ref, lb2_ref):
    def natural(o_ref, l_ref, ob_ref, lb_ref, dil):
        if dil == 1:
            return o_ref[...].astype(F32), l_ref[...]
        n = ROW_TILE // dil
        for r in range(dil):
            for c in range(HEADS):
                col = r * A_W + c * HD
                ob_ref[c, pl.ds(r, n, stride=dil), :] = o_ref[:, col:col + HD].astype(F32)
            lb_ref[pl.ds(r, n, stride=dil), :] = l_ref[:, r * LSE_W:(r + 1) * LSE_W]
        return jnp.concatenate([ob_ref[c] for c in range(HEADS)], axis=1), lb_ref[...]

    parts = [natural(o0_ref, l0_ref, None, None, B_GROUPS[0][1]),
             natural(o1_ref, l1_ref, ob1_ref, lb1_ref, B_GROUPS[1][1]),
             natural(o2_ref, l2_ref, ob2_ref, lb2_ref, B_GROUPS[2][1])]
    lses = [l for _, l in parts]
    mx = jnp.maximum(jnp.maximum(lses[0], lses[1]), lses[2])
    es = [jnp.exp(l - mx) for l in lses]
    inv = 1.0 / (es[0] + es[1] + es[2])
    emat = e_ref[...]
    mixed_b = None
    for e_g, (o_g, _) in zip(es, parts):
        alpha = (e_g * inv).astype(BF16)
        term = _dot(alpha, emat) * o_g
        mixed_b = term if mixed_b is None else mixed_b + term
    y_ref[...] = (h_ref[...] + _dot(a_ref[...], w_ref[:A_W, :]) + _dot(mixed_b.astype(BF16), w_ref[A_W:, :]))


def _out_even(h, out_a, os_, lses, w):
    m = h.shape[0]
    emat = _lse_expansion()
    row = lambda n: pl.BlockSpec((ROW_TILE, n), lambda i: (i, 0))
    grp = lambda width: [pl.BlockSpec((ROW_TILE // dil, dil * width), lambda i: (i, 0)) for _, dil in B_GROUPS]
    return pl.pallas_call(
        _out_even_body,
        grid=(m // ROW_TILE,),
        in_specs=[row(D_MODEL), row(A_W)] + grp(A_W) + grp(LSE_W) + [_resident(emat.shape), _resident(w.shape)],
        out_specs=row(D_MODEL),
        out_shape=jax.ShapeDtypeStruct((m, D_MODEL), F32),
        scratch_shapes=[pltpu.VMEM((HEADS, ROW_TILE, HD), F32), pltpu.VMEM((HEADS, ROW_TILE, HD), F32),
                        pltpu.VMEM((ROW_TILE, LSE_W), F32), pltpu.VMEM((ROW_TILE, LSE_W), F32)],
        compiler_params=_cparams("parallel"),
        name="out_proj_even",
    )(h, out_a, *os_, *lses, emat, w)


def _out_odd_body(h_ref, a_ref, w_ref, y_ref):
    y_ref[...] = h_ref[...] + _dot(a_ref[...], w_ref[...])


def _out_odd(h, mixed, w):
    m = h.shape[0]
    row = lambda n: pl.BlockSpec((ROW_TILE, n), lambda i: (i, 0))
    return pl.pallas_call(
        _out_odd_body,
        grid=(m // ROW_TILE,),
        in_specs=[row(D_MODEL), row(C_VD), _resident(w.shape)],
        out_specs=row(D_MODEL),
        out_shape=jax.ShapeDtypeStruct((m, D_MODEL), F32),
        compiler_params=_cparams("parallel"),
        name="out_proj_odd",
    )(h, mixed, w)


def _ffn_body(hp_ref, h_ref, hn_ref, p_ref, gf_ref, wup_ref, cw_ref, cb_ref, wdn_ref, gp_ref, wg_ref, wp_ref,
              go_ref, y_ref, xe_ref, act_ref, *, tiles_per_seq, final):
    i = pl.program_id(0) % tiles_per_seq
    gf = gf_ref[...]
    h = h_ref[...]
    keep_prev = jnp.where(i > 0, 1.0, 0.0).astype(F32)
    keep_next = jnp.where(i < tiles_per_seq - 1, 1.0, 0.0).astype(F32)
    xe_ref[0:HALO, :] = _rms(hp_ref[...], gf) * keep_prev
    xe_ref[HALO:HALO + ROW_TILE, :] = _rms(h, gf)
    xe_ref[HALO + ROW_TILE:, :] = _rms(hn_ref[...], gf) * keep_next
    xe = xe_ref[...].astype(BF16)
    rows = ROW_TILE + 2 * HALO

    def conv(u, c):
        w = cw_ref[:, c:c + FF_CHUNK]
        prev = pltpu.roll(u, 1, 0)[HALO:HALO + ROW_TILE]
        nxt = pltpu.roll(u, rows - 1, 0)[HALO:HALO + ROW_TILE]
        return (prev * w[0:1] + u[HALO:HALO + ROW_TILE] * w[1:2] + nxt * w[2:3] + cb_ref[:, c:c + FF_CHUNK])

    for c in range(0, D_FF, FF_CHUNK):
        a = conv(_dot(xe, wup_ref[:, c:c + FF_CHUNK]), c)
        gt = conv(_dot(xe, wup_ref[:, D_FF + c:D_FF + c + FF_CHUNK]), D_FF + c)
        act_ref[:, c:c + FF_CHUNK] = (a * (0.5 * gt * (1.0 + lax.erf(gt * (2.0 ** -0.5))))).astype(BF16)
    h2 = h + _dot(act_ref[...], wdn_ref[...])
    sig = _sigmoid(_dot(_rms(h2, gp_ref[...]).astype(BF16), wg_ref[...]))
    h3 = h2 + sig * _dot(p_ref[...].astype(BF16), wp_ref[...])
    y_ref[...] = _rms(h3, go_ref[...]) if final else h3


def _ffn_ple(h, p, t, gf, wup, cw, cb, wdn, gp, wg, wp, go, final):
    m = h.shape[0]
    per_tile = ROW_TILE // HALO
    last_blk = m // HALO - 1
    vec = lambda n: pl.BlockSpec((1, n), lambda i: (0, 0))
    return pl.pallas_call(
        functools.partial(_ffn_body, tiles_per_seq=t // ROW_TILE, final=final),
        grid=(m // ROW_TILE,),
        in_specs=[pl.BlockSpec((HALO, D_MODEL), lambda i: (jnp.maximum(i * per_tile - 1, 0), 0)),
                  pl.BlockSpec((ROW_TILE, D_MODEL), lambda i: (i, 0)),
                  pl.BlockSpec((HALO, D_MODEL), lambda i: (jnp.minimum((i + 1) * per_tile, last_blk), 0)),
                  pl.BlockSpec((ROW_TILE, PLE_DIM), lambda i: (i, 0)),
                  vec(D_MODEL), _resident(wup.shape), _resident(cw.shape), vec(2 * D_FF), _resident(wdn.shape),
                  vec(D_MODEL), _resident(wg.shape), _resident(wp.shape), vec(D_MODEL)],
        out_specs=pl.BlockSpec((ROW_TILE, D_MODEL), lambda i: (i, 0)),
        out_shape=jax.ShapeDtypeStruct((m, D_MODEL), F32),
        scratch_shapes=[pltpu.VMEM((ROW_TILE + 2 * HALO, D_MODEL), F32), pltpu.VMEM((ROW_TILE, D_FF), BF16)],
        compiler_params=_cparams("parallel"),
        name="conv_ffn_ple",
    )(h, h, h, p, gf, wup, cw, cb, wdn, gp, wg, wp, go)


def _prep_weights(ev_w_in, od_w_in, gla_w_gate_up, gla_b_gate):
    aq, zf, zb, ai, ag, bqkv = jnp.split(ev_w_in, [A_W, 2 * A_W, 3 * A_W, 4 * A_W, 5 * A_W], axis=-1)
    ev_w = jnp.concatenate([aq, ai, ag, zf, zb, bqkv], axis=-1).astype(BF16)
    od_w = jnp.pad(od_w_in, ((0, 0), (0, 0), (0, LR_PAD - 2 * C_RANK))).astype(BF16)
    n_odd = od_w_in.shape[0]
    w2 = jnp.zeros((n_odd, LR_PAD, 2 * C_KD), F32)
    w2 = w2.at[:, :C_RANK, :C_KD].set(gla_w_gate_up[:, 0]).at[:, C_RANK:2 * C_RANK, C_KD:].set(gla_w_gate_up[:, 1])
    b2 = gla_b_gate.reshape(n_odd, 1, 2 * C_KD)
    return ev_w, od_w, w2.astype(BF16), b2


def _trunk(x, p, prm):
    bsz, t, _ = x.shape
    m = bsz * t
    h = x.reshape(m, D_MODEL)
    for l in range(DEPTH):
        g_mix = prm["norm_mix_g"][l][None]
        if l % 2 == 0:
            e = l // 2
            lb = jnp.concatenate([prm["lb"][0, e], prm["lb"][1, e]])[None]
            a16, a32, *qkv = _in_proj_even(h, g_mix, lb, prm["ev_w"][e])
            out_a = _hgrn_mixer(a16.reshape(bsz, t, -1), a32.reshape(bsz, t, -1), prm["hgrn_norm_g"][e][None])
            att = [_dilated_attention(x_g.reshape(bsz, t // dil, dil * QKV_W), gi)
                   for gi, (x_g, (_, dil)) in enumerate(zip(qkv, B_GROUPS))]
            h = _out_even(h, out_a.reshape(m, A_W),
                          [o.reshape(m // dil, dil * A_W) for (o, _), (_, dil) in zip(att, B_GROUPS)],
                          [l_.reshape(m // dil, dil * LSE_W) for (_, l_), (_, dil) in zip(att, B_GROUPS)],
                          prm["ev_w_out"][e])
        else:
            o = l // 2
            c16, lg = _in_proj_odd(h, g_mix, prm["od_w"][o], prm["od_w2"][o], prm["od_b2"][o])
            mixed = _gla_mixer(c16.reshape(bsz, t, -1), lg.reshape(bsz, t, -1), prm["gla_norm_g"][o][None])
            h = _out_odd(h, mixed.reshape(m, C_VD), prm["od_w_out"][o])
        h = _ffn_ple(h, p[l].reshape(m, PLE_DIM), t, prm["norm_ffn_g"][l][None], prm["ffn_w_up"][l],
                     prm["ffn_conv_w"][l], prm["ffn_conv_b"][l][None], prm["ffn_w_down"][l],
                     prm["norm_ple_g"][l][None], prm["ple_w_gate"][l], prm["ple_w_proj"][l],
                     prm["norm_out_g"][None], final=(l == DEPTH - 1))
    return h.reshape(bsz, t, D_MODEL)


def kernel(x_prompt, x_sample, p_prompt, p_sample, norm_mix_g, ev_w_in, hgrn_lb_logits, hgrn_norm_g, ev_w_out, od_w_in, gla_w_gate_up, gla_b_gate, gla_norm_g, od_w_out, norm_ffn_g, ffn_w_up, ffn_conv_w, ffn_conv_b, ffn_w_down, norm_ple_g, ple_w_gate, ple_w_proj, norm_out_g):
    lb = jnp.cumsum(jax.nn.softmax(hgrn_lb_logits.astype(F32), axis=1), axis=1)
    lb = lb - lb[:, :1]
    ev_w, od_w, od_w2, od_b2 = _prep_weights(ev_w_in, od_w_in, gla_w_gate_up, gla_b_gate)
    prm = dict(norm_mix_g=norm_mix_g, ev_w=ev_w, lb=lb, hgrn_norm_g=hgrn_norm_g, ev_w_out=ev_w_out.astype(BF16),
               od_w=od_w, od_w2=od_w2, od_b2=od_b2, gla_norm_g=gla_norm_g, od_w_out=od_w_out.astype(BF16),
               norm_ffn_g=norm_ffn_g, ffn_w_up=ffn_w_up.astype(BF16), ffn_conv_w=ffn_conv_w,
               ffn_conv_b=ffn_conv_b, ffn_w_down=ffn_w_down.astype(BF16), norm_ple_g=norm_ple_g,
               ple_w_gate=ple_w_gate.astype(BF16), ple_w_proj=ple_w_proj.astype(BF16), norm_out_g=norm_out_g)
    return _trunk(x_prompt, p_prompt, prm), _trunk(x_sample, p_sample, prm)
```

```python
import functools

import numpy as np
import jax
import jax.numpy as jnp
from jax import lax
from jax.experimental import pallas as pl
from jax.experimental.pallas import tpu as pltpu

F32 = jnp.float32
BF16 = jnp.bfloat16

D_MODEL = 1024
DEPTH = 4
PLE_DIM = 256
EPS = 1e-6
HEADS = 4
HD = 128
A_W = HEADS * HD
B_GROUPS = ((128, 1), (512, 4), (2048, 16))
N_GROUPS = len(B_GROUPS)
QKV_W = 3 * A_W
ATT_Q = 128
ATT_CLASSES = 4
ALIBI_MAX_EXP = 8.0
C_KD = 512
C_VD = 1024
C_HDV = C_VD // HEADS
C_RANK = 16
GATE_NORMALIZER = 16.0
CHUNK = 64
SUB = 16
N_SUB = CHUNK // SUB
SUB8 = 8
CHUNKS_PER_STEP = {"chunk": 8, "block": 8, "pair": 4}
LOG_DECAY_MIN = -30.0
LOG2E = 1.4426950408889634
SAFE_LOG2_RANGE = {"chunk": 116.0, "block": 96.0}
NEG_INF = -1e30
D_FF = 2816
LSE_W = 128
LSE_REP = LSE_W // HEADS
LR_PAD = 128

ROW_TILE = 512
HALO = 8
FF_CHUNK = 256
VMEM_LIMIT = 56 * 1024 * 1024


def _cparams(*sem):
    return pltpu.CompilerParams(dimension_semantics=sem, vmem_limit_bytes=VMEM_LIMIT)


def _resident(shape):
    nd = len(shape)
    return pl.BlockSpec(shape, lambda *_: (0,) * nd, pipeline_mode=pl.Buffered(1))


def _resident_layer(stacked, layer):
    nd = stacked.ndim
    return pl.BlockSpec((None,) + stacked.shape[1:], lambda *_: (layer,) + (0,) * (nd - 1),
                        pipeline_mode=pl.Buffered(1))


def _rms(x, g):
    ms = jnp.mean(x * x, axis=-1, keepdims=True)
    return x * lax.rsqrt(ms + EPS) * g


def _dot(a, b):
    return jnp.dot(a, b, preferred_element_type=F32)


def _dot_nt(a, b):
    return lax.dot_general(a, b, (((1,), (1,)), ((), ())), preferred_element_type=F32)


def _dot_tn(a, b):
    return lax.dot_general(a, b, (((0,), (0,)), ((), ())), preferred_element_type=F32)


def _store_cols(xn, w_ref, out_refs, c0=0, col_chunk=512):
    for o_ref in out_refs:
        n = o_ref.shape[-1]
        for j in range(0, n, col_chunk):
            wj = min(col_chunk, n - j)
            o_ref[:, j:j + wj] = _dot(xn, w_ref[:, c0 + j:c0 + j + wj]).astype(o_ref.dtype)
        c0 += n
    return c0


def _clamp_log2_decay(log2_f):
    return jnp.maximum(log2_f, LOG_DECAY_MIN * LOG2E)


def _in_proj_even_body(x_ref, g_ref, lb_ref, w_ref, a16_ref, a32_ref, g0_ref, g1_ref, g2_ref, u_ref):
    xn = _rms(x_ref[...], g_ref[...]).astype(BF16)
    _store_cols(xn, w_ref, (a16_ref.at[:, :A_W],))
    _store_cols(xn, w_ref, (a16_ref.at[:, A_W:3 * A_W],), c0=3 * A_W)
    c0 = 5 * A_W
    for d in range(2):
        lb = lb_ref[:, d * A_W:(d + 1) * A_W]
        sig, nsig = _sigmoid_pair(_dot(xn, w_ref[:, (1 + d) * A_W:(2 + d) * A_W]))
        a32_ref[:, d * A_W:(d + 1) * A_W] = _clamp_log2_decay(jnp.log2(lb + (1.0 - lb) * sig))
        a16_ref[:, (3 + d) * A_W:(4 + d) * A_W] = ((1.0 - lb) * nsig).astype(BF16)
    for (_, dil), o_ref in zip(B_GROUPS, (g0_ref, g1_ref, g2_ref)):
        n = ROW_TILE // dil
        for j in range(3):
            u = _dot(xn, w_ref[:, c0:c0 + A_W])
            c0 += A_W
            if j == 0:
                u = u * (HD ** -0.5)
            if dil == 1:
                o_ref[:, j * A_W:(j + 1) * A_W] = u.astype(BF16)
            else:
                for c in range(HEADS):
                    u_ref[c] = u[:, c * HD:(c + 1) * HD]
                for r in range(dil):
                    for c in range(HEADS):
                        col = (3 * r + j) * A_W + c * HD
                        o_ref[:, col:col + HD] = u_ref[c, pl.ds(r, n, stride=dil), :].astype(BF16)


def _in_proj_odd_body(x_ref, g_ref, w_ref, w2_ref, b2_ref, c16_ref, g2_ref):
    xn = _rms(x_ref[...], g_ref[...]).astype(BF16)
    c16_ref[:, :C_KD] = (_dot(xn, w_ref[:, :C_KD]) * (HD ** -0.5)).astype(BF16)
    c0 = _store_cols(xn, w_ref, (c16_ref.at[:, C_KD:],), c0=C_KD)
    lr = _dot(xn, w_ref[:, c0:c0 + LR_PAD]).astype(BF16)
    n = g2_ref.shape[-1]
    for j in range(0, n, 512):
        logit = _dot(lr, w2_ref[:, j:j + 512]) + b2_ref[:, j:j + 512]
        g2_ref[:, j:j + 512] = _clamp_log2_decay(_log_sigmoid(logit) * (LOG2E / GATE_NORMALIZER))


def _in_proj_even(x, g, lb, w, layer):
    m = x.shape[0]
    outs = [((m, 5 * A_W), (ROW_TILE, 5 * A_W), BF16), ((m, 2 * A_W), (ROW_TILE, 2 * A_W), F32)]
    for _, dil in B_GROUPS:
        outs.append(((m // dil, dil * QKV_W), (ROW_TILE // dil, dil * QKV_W), BF16))
    return pl.pallas_call(
        _in_proj_even_body,
        grid=(m // ROW_TILE,),
        in_specs=[pl.BlockSpec((ROW_TILE, D_MODEL), lambda i: (i, 0)),
                  pl.BlockSpec((1, D_MODEL), lambda i: (0, 0)),
                  pl.BlockSpec((1, 2 * A_W), lambda i: (0, 0)),
                  _resident_layer(w, layer)],
        out_specs=[pl.BlockSpec(blk, lambda i: (i, 0)) for _, blk, _ in outs],
        out_shape=[jax.ShapeDtypeStruct(shp, dt) for shp, _, dt in outs],
        scratch_shapes=[pltpu.VMEM((HEADS, ROW_TILE, HD), F32)],
        compiler_params=_cparams("parallel"),
        name="in_proj_even",
    )(x, g, lb, w)


def _in_proj_odd(x, g, w, w2, b2, layer):
    m = x.shape[0]
    n16 = 2 * C_KD + 2 * C_VD
    return pl.pallas_call(
        _in_proj_odd_body,
        grid=(m // ROW_TILE,),
        in_specs=[pl.BlockSpec((ROW_TILE, D_MODEL), lambda i: (i, 0)),
                  pl.BlockSpec((1, D_MODEL), lambda i: (0, 0)),
                  _resident_layer(w, layer), _resident_layer(w2, layer),
                  pl.BlockSpec((1, 2 * C_KD), lambda i: (0, 0))],
        out_specs=[pl.BlockSpec((ROW_TILE, n16), lambda i: (i, 0)),
                   pl.BlockSpec((ROW_TILE, 2 * C_KD), lambda i: (i, 0))],
        out_shape=[jax.ShapeDtypeStruct((m, n16), BF16), jax.ShapeDtypeStruct((m, 2 * C_KD), F32)],
        compiler_params=_cparams("parallel"),
        name="in_proj_odd",
    )(x, g, w, w2, b2)


def _off_ranges(rev):
    out = []
    for j in range(N_SUB):
        lo, hi = (0, SUB * j) if rev else (SUB * (j + 1), CHUNK)
        if hi > lo:
            out.append((j, lo, hi))
    return out


def _gla_constants(rev):
    t = np.arange(CHUNK)
    tri = (t[None, :] >= t[:, None]) if rev else (t[None, :] <= t[:, None])
    causal = tri
    same8 = (t[None, :] // SUB8) == (t[:, None] // SUB8)
    same16 = (t[None, :] // SUB) == (t[:, None] // SUB)
    f = lambda a: jnp.asarray(a.astype(np.float32))
    return (jnp.asarray(np.concatenate([tri, tri], axis=1).astype(np.float32), BF16),
            f(causal), f(same16 & causal), f(same8 & causal), f(same16 & ~same8 & causal))


def _gla_shared_constants():
    e = np.zeros((SUB8 * HD, CHUNK), np.float32)
    for s in range(SUB8):
        e[s * HD:(s + 1) * HD, s::SUB8] = 1.0
    r = np.arange(ROW_TILE)
    sel = [(r[None, :] // n == np.arange(ROW_TILE // n)[:, None]).astype(np.float32) for n in (SUB, CHUNK)]
    return jnp.asarray(e, BF16), jnp.asarray(np.concatenate(sel, axis=0), BF16)


def _rows(ref, r, n):
    return jnp.broadcast_to(ref[pl.ds(r, 1), :], (n, HD))


def _gla_cumsum(k, g2, b_ref, k_ref, tri_ref, form):
    g_hi = g2.astype(BF16)
    g_lo = (g2 - g_hi.astype(F32)).astype(BF16)
    b = _dot(tri_ref[...], jnp.concatenate([g_hi, g_lo], axis=0))
    b_ref[...] = b
    if form == "pair":
        k_ref[...] = k
    return b


def _gla_scores(qs, k, v, b, st_ref, b_ref, k_ref, emat_ref, rev, form):
    b_tot = b_ref[pl.ds(0 if rev else CHUNK - 1, 1), :]

    st = st_ref[...]
    qd = qs * jnp.exp2(b)
    kd = (k * jnp.exp2(b_tot - b)).astype(BF16)
    o = _dot_nt(qd.astype(BF16), st.astype(BF16))
    st_ref[...] = st * jnp.exp2(b_tot) + _dot_tn(v, kd)
    if form == "chunk":
        return o, _dot_nt((qd * jnp.exp2(-b_tot)).astype(BF16), kd)

    edge16 = [SUB * i + (0 if rev else SUB - 1) for i in range(N_SUB)]
    e_ko = jnp.exp2(jnp.concatenate([_rows(b_ref, e, SUB) for e in edge16], axis=0) - b)
    k_off = k * e_ko
    zq = jnp.zeros((SUB, HD), F32)
    q_parts, k_parts = [], []
    for j, lo, hi in _off_ranges(rev):
        q_off = qs[lo:hi] * jnp.exp2(jnp.minimum(b[lo:hi] - _rows(b_ref, edge16[j], 1), 0.0))
        q_parts.append(jnp.concatenate([zq] * (lo // SUB) + [q_off] + [zq] * ((CHUNK - hi) // SUB), axis=0))
        k_parts.append(jnp.concatenate([zq] * j + [k_off[SUB * j:SUB * (j + 1)]] + [zq] * (N_SUB - 1 - j),
                                       axis=0))
    a_16 = _dot_nt(jnp.concatenate(q_parts, axis=1).astype(BF16),
                   jnp.concatenate(k_parts, axis=1).astype(BF16))

    if form == "block":
        return o, a_16, _dot_nt((qs * (1.0 / e_ko)).astype(BF16), k_off.astype(BF16))

    edge8 = [SUB * i + (SUB8 if rev else SUB8 - 1) for i in range(N_SUB)]
    d8 = b - jnp.concatenate([_rows(b_ref, e, SUB) for e in edge8], axis=0)
    second = (lax.broadcasted_iota(jnp.int32, (CHUNK, HD), 0) % SUB) >= SUB8
    q_side = jnp.logical_not(second) if rev else second
    w8 = jnp.exp2(jnp.minimum(jnp.where(q_side, d8, -d8), 0.0))
    a_8 = _dot_nt(jnp.where(q_side, qs * w8, 0.0).astype(BF16), jnp.where(q_side, 0.0, k * w8).astype(BF16))

    slabs = []
    for s in range(SUB8):
        rows = [SUB8 * i + s for i in range(CHUNK // SUB8)]
        b_s = jnp.concatenate([_rows(b_ref, r, SUB8) for r in rows], axis=0)
        k_s = jnp.concatenate([_rows(k_ref, r, SUB8) for r in rows], axis=0)
        slabs.append((qs * k_s * jnp.exp2(jnp.minimum(b - b_s, 0.0))).astype(BF16))
    a_dg = _dot(jnp.concatenate(slabs, axis=1), emat_ref[...])
    return o, a_16, a_8, a_dg


def _gla_output(scores, v, m_c_ref, m_16_ref, m_dg_ref, m_8_ref, form):
    if form == "chunk":
        o, a_c = scores
        a = jnp.where(m_c_ref[...] > 0.0, a_c, 0.0)
    elif form == "block":
        o, a_16, a_in = scores
        a = a_in * m_16_ref[...] + a_16
    else:
        o, a_16, a_8, a_dg = scores
        a = a_dg * m_dg_ref[...] + a_8 * m_8_ref[...] + a_16
    return o + _dot(a.astype(BF16), v)


def _decay_ranges_ok(g_ref, sel_ref, n_tiles):
    sums = [_dot(sel_ref[...], g_ref[i * ROW_TILE:(i + 1) * ROW_TILE, :].astype(BF16)) for i in range(n_tiles)]
    low = functools.reduce(jnp.minimum, sums)
    n16 = ROW_TILE // SUB
    return (jnp.min(low[n16:]) > -SAFE_LOG2_RANGE["chunk"], jnp.min(low[:n16]) > -SAFE_LOG2_RANGE["block"])


def _sigmoid_pair(z):
    e = jnp.exp(-jnp.abs(z))
    r = 1.0 / (1.0 + e)
    er = e * r
    pos = z >= 0.0
    return jnp.where(pos, r, er), jnp.where(pos, er, r)


def _sigmoid(x):
    return 1.0 / (1.0 + jnp.exp(-x))


def _log_sigmoid(x):
    return jnp.minimum(x, 0.0) - jnp.log1p(jnp.exp(-jnp.abs(x)))


def _bidir_body(q_ref, kf_ref, kb_ref, g2f_ref, g2b_ref, v_ref, gate_ref, ng_ref,
                trif_ref, mcf_ref, m16f_ref, mdf_ref, m8f_ref, trib_ref, mcb_ref, m16b_ref, mdb_ref, m8b_ref,
                e_ref, sel_ref,
                o_ref, accf_ref, accb_ref, sf_ref, sb_ref, stage_ref, *, nc):
    def run(form):
        per = CHUNKS_PER_STEP[form]
        sf_ref[...] = jnp.zeros_like(sf_ref)
        sb_ref[...] = jnp.zeros_like(sb_ref)

        def step(c, carry):
            streams = []
            for u, rev in [(u, rev) for u in range(per) for rev in (False, True)]:
                cf = c * per + u
                sl = pl.ds(pl.multiple_of((nc - 1 - cf if rev else cf) * CHUNK, CHUNK), CHUNK)
                b_ref = stage_ref.at[4 * u + 2 * int(rev)]
                kst_ref = stage_ref.at[4 * u + 2 * int(rev) + 1]
                k = (kb_ref if rev else kf_ref)[sl, :].astype(F32)
                qs = q_ref[sl, :].astype(F32)
                b = _gla_cumsum(k, (g2b_ref if rev else g2f_ref)[sl, :], b_ref, kst_ref,
                                trib_ref if rev else trif_ref, form)
                streams.append((rev, sl, qs, k, b, b_ref, kst_ref))
            scores = [_gla_scores(qs, k, v_ref[sl, :], b, sb_ref if rev else sf_ref, b_ref, kst_ref, e_ref,
                                  rev, form)
                      for rev, sl, qs, k, b, b_ref, kst_ref in streams]
            for (rev, sl, *_), sc in zip(streams, scores):
                masks = (mcb_ref, m16b_ref, mdb_ref, m8b_ref) if rev else (mcf_ref, m16f_ref, mdf_ref, m8f_ref)
                (accb_ref if rev else accf_ref)[sl, :] = _gla_output(sc, v_ref[sl, :], *masks, form)
            return carry

        lax.fori_loop(0, nc // per, step, 0)

    n_tiles = (nc * CHUNK) // ROW_TILE
    chunk_f, block_f = _decay_ranges_ok(g2f_ref, sel_ref, n_tiles)
    chunk_b, block_b = _decay_ranges_ok(g2b_ref, sel_ref, n_tiles)
    chunk_ok = jnp.logical_and(chunk_f, chunk_b)
    block_ok = jnp.logical_and(jnp.logical_and(block_f, block_b), jnp.logical_not(chunk_ok))
    pl.when(chunk_ok)(functools.partial(run, "chunk"))
    pl.when(block_ok)(functools.partial(run, "block"))
    pl.when(jnp.logical_not(jnp.logical_or(chunk_ok, block_ok)))(functools.partial(run, "pair"))

    ng = ng_ref[...]

    def finish(i, carry):
        sl = pl.ds(pl.multiple_of(i * ROW_TILE, ROW_TILE), ROW_TILE)
        o = accf_ref[sl, :] + accb_ref[sl, :]
        gt = gate_ref[sl, :].astype(F32)
        o_ref[sl, :] = (_rms(o, ng) * (gt * _sigmoid(gt))).astype(o_ref.dtype)
        return carry

    lax.fori_loop(0, (nc * CHUNK) // ROW_TILE, finish, 0)


def _bidir_call(name, bsz, t, dv, args, in_specs):
    consts = (*_gla_constants(False), *_gla_constants(True), *_gla_shared_constants())
    const = lambda shape: pl.BlockSpec(shape, lambda b, h: (0,) * len(shape))
    return pl.pallas_call(
        functools.partial(_bidir_body, nc=t // CHUNK),
        grid=(bsz, HEADS),
        in_specs=in_specs + [const(c.shape) for c in consts],
        out_specs=pl.BlockSpec((None, t, dv), lambda b, h: (b, 0, h)),
        out_shape=jax.ShapeDtypeStruct((bsz, t, HEADS * dv), BF16),
        scratch_shapes=[pltpu.VMEM((t, dv), F32), pltpu.VMEM((t, dv), F32),
                        pltpu.VMEM((dv, HD), F32), pltpu.VMEM((dv, HD), F32)]
                       + [pltpu.VMEM((4 * max(CHUNKS_PER_STEP.values()), CHUNK, HD), F32)],
        compiler_params=_cparams("parallel", "parallel"),
        name=name,
    )(*args, *consts)


def _hgrn_mixer(a16, a32, norm_g):
    bsz, t, _ = a16.shape
    col = lambda off: pl.BlockSpec((None, t, HD), lambda b, h: (b, 0, off * HEADS + h))
    return _bidir_call("hgrn2_bidir", bsz, t, HD,
                       (a16, a16, a16, a32, a32, a16, a16, norm_g),
                       [col(0), col(3), col(4), col(0), col(1), col(1), col(2),
                        pl.BlockSpec((1, HD), lambda b, h: (0, h))])


def _gla_mixer(c16, g2, norm_g):
    bsz, t, _ = c16.shape
    col = lambda off: pl.BlockSpec((None, t, HD), lambda b, h: (b, 0, off * HEADS + h))
    wide = lambda off: pl.BlockSpec((None, t, C_HDV), lambda b, h: (b, 0, off + h))
    v_off = 2 * C_KD // C_HDV
    return _bidir_call("gla_bidir", bsz, t, C_HDV,
                       (c16, c16, c16, g2, g2, c16, c16, norm_g),
                       [col(0), col(1), col(1), col(0), col(1), wide(v_off), wide(v_off + HEADS),
                        pl.BlockSpec((1, C_HDV), lambda b, h: (0, h))])


def _attn_window(length, radius):
    win = min(ATT_Q + 2 * radius, length)
    return win, (0, -radius, ATT_Q - win)


def _attn_bias(gi, length):
    wsize, dil = B_GROUPS[gi]
    radius = wsize // (2 * dil)
    win, offsets = _attn_window(length, radius)
    n = N_GROUPS * HEADS
    slopes = np.array([2.0 ** (-ALIBI_MAX_EXP * (gi * HEADS + h + 1) / n) for h in range(HEADS)])
    dist = np.abs(np.array(offsets)[:, None, None] + np.arange(win)[None, None, :] - np.arange(ATT_Q)[None, :, None])
    bias = np.where(dist[:, None] <= radius, -slopes[None, :, None, None] * (dil * dist[:, None]), NEG_INF)
    return jnp.asarray(bias, F32)


def _attn_body(x_ref, bias_ref, o_ref, l_ref, *, length, radius, classes):
    win, _ = _attn_window(length, radius)
    n_tiles = length // ATT_Q
    per = 2 if (classes == 1 and n_tiles % 2 == 0) else 1

    def step(i, carry):
        probs = []
        for u in range(per):
            tile = i * per + u
            m0 = pl.multiple_of(tile * ATT_Q, ATT_Q)
            k0 = pl.multiple_of(jnp.clip(m0 - radius, 0, length - win), radius)
            place = jnp.where(tile == 0, 0, jnp.where(tile == n_tiles - 1, 2, 1))
            probs += [(m0, k0, place, r, h) for r in range(classes) for h in range(HEADS)]
        col = lambda r, j, h: pl.ds((3 * r + j) * A_W + h * HD, HD)
        s = [_dot_nt(x_ref[pl.ds(m0, ATT_Q), col(r, 0, h)], x_ref[pl.ds(k0, win), col(r, 1, h)])
             for m0, k0, _, r, h in probs]
        s = [x + bias_ref[place, h] for x, (_, _, place, _, h) in zip(s, probs)]
        mx = [jnp.max(x, axis=-1, keepdims=True) for x in s]
        p = [jnp.exp(x - m) for x, m in zip(s, mx)]
        den = [jnp.sum(x, axis=-1, keepdims=True) for x in p]
        o = [_dot(x.astype(BF16), x_ref[pl.ds(k0, win), col(r, 2, h)]) for x, (_, k0, _, r, h) in zip(p, probs)]
        for x, d, m, (m0, _, _, r, h) in zip(o, den, mx, probs):
            o_ref[pl.ds(m0, ATT_Q), pl.ds(r * A_W + h * HD, HD)] = (x * (1.0 / d)).astype(o_ref.dtype)
            l_ref[pl.ds(m0, ATT_Q), pl.ds(r * LSE_W + h * LSE_REP, LSE_REP)] = jnp.broadcast_to(
                m + jnp.log(d), (ATT_Q, LSE_REP))
        return carry

    lax.fori_loop(0, n_tiles // per, step, 0)


def _dilated_attention(qkv, gi):
    bsz, length, _ = qkv.shape
    wsize, dil = B_GROUPS[gi]
    classes = min(dil, ATT_CLASSES)
    bias = _attn_bias(gi, length)
    return pl.pallas_call(
        functools.partial(_attn_body, length=length, radius=wsize // (2 * dil), classes=classes),
        grid=(bsz, dil // classes),
        in_specs=[pl.BlockSpec((None, length, classes * QKV_W), lambda b, r: (b, 0, r)),
                  pl.BlockSpec(bias.shape, lambda b, r: (0, 0, 0, 0))],
        out_specs=[pl.BlockSpec((None, length, classes * A_W), lambda b, r: (b, 0, r)),
                   pl.BlockSpec((None, length, classes * LSE_W), lambda b, r: (b, 0, r))],
        out_shape=[jax.ShapeDtypeStruct((bsz, length, dil * A_W), BF16),
                   jax.ShapeDtypeStruct((bsz, length, dil * LSE_W), F32)],
        compiler_params=_cparams("parallel", "parallel"),
        name=f"dilated_attn_g{gi}",
    )(qkv, bias)


def _lse_expansion():
    e = np.zeros((LSE_W, A_W), np.float32)
    for h in range(HEADS):
        e[h * LSE_REP, h * HD:(h + 1) * HD] = 1.0
    return jnp.asarray(e, BF16)


def _out_even_body(h_ref, a_ref, o0_ref, o1_ref, o2_ref, l0_ref, l1_ref, l2_ref, e_ref, w_ref, y_ref,
                   ob1_ref, ob2_ref, lb1_ref, lb2_ref):
    def natural(o_ref, l_ref, ob_ref, lb_ref, dil):
        if dil == 1:
            return o_ref[...].astype(F32), l_ref[...]
        n = ROW_TILE // dil
        for r in range(dil):
            for c in range(HEADS):
                col = r * A_W + c * HD
                ob_ref[c, pl.ds(r, n, stride=dil), :] = o_ref[:, col:col + HD].astype(F32)
            lb_ref[pl.ds(r, n, stride=dil), :] = l_ref[:, r * LSE_W:(r + 1) * LSE_W]
        return jnp.concatenate([ob_ref[c] for c in range(HEADS)], axis=1), lb_ref[...]

    parts = [natural(o0_ref, l0_ref, None, None, B_GROUPS[0][1]),
             natural(o1_ref, l1_ref, ob1_ref, lb1_ref, B_GROUPS[1][1]),
             natural(o2_ref, l2_ref, ob2_ref, lb2_ref, B_GROUPS[2][1])]
    lses = [l for _, l in parts]
    mx = jnp.maximum(jnp.maximum(lses[0], lses[1]), lses[2])
    es = [jnp.exp(l - mx) for l in lses]
    inv = 1.0 / (es[0] + es[1] + es[2])
    emat = e_ref[...]
    mixed_b = None
    for e_g, (o_g, _) in zip(es, parts):
        alpha = (e_g * inv).astype(BF16)
        term = _dot(alpha, emat) * o_g
        mixed_b = term if mixed_b is None else mixed_b + term
    y_ref[...] = (h_ref[...] + _dot(a_ref[...], w_ref[:A_W, :]) + _dot(mixed_b.astype(BF16), w_ref[A_W:, :]))


def _out_even(h, out_a, os_, lses, w, layer):
    m = h.shape[0]
    emat = _lse_expansion()
    row = lambda n: pl.BlockSpec((ROW_TILE, n), lambda i: (i, 0))
    grp = lambda width: [pl.BlockSpec((ROW_TILE // dil, dil * width), lambda i: (i, 0)) for _, dil in B_GROUPS]
    return pl.pallas_call(
        _out_even_body,
        grid=(m // ROW_TILE,),
        in_specs=[row(D_MODEL), row(A_W)] + grp(A_W) + grp(LSE_W) + [_resident(emat.shape), _resident_layer(w, layer)],
        out_specs=row(D_MODEL),
        out_shape=jax.ShapeDtypeStruct((m, D_MODEL), F32),
        scratch_shapes=[pltpu.VMEM((HEADS, ROW_TILE, HD), F32), pltpu.VMEM((HEADS, ROW_TILE, HD), F32),
                        pltpu.VMEM((ROW_TILE, LSE_W), F32), pltpu.VMEM((ROW_TILE, LSE_W), F32)],
        compiler_params=_cparams("parallel"),
        name="out_proj_even",
    )(h, out_a, *os_, *lses, emat, w)


def _out_odd_body(h_ref, a_ref, w_ref, y_ref):
    y_ref[...] = h_ref[...] + _dot(a_ref[...], w_ref[...])


def _out_odd(h, mixed, w, layer):
    m = h.shape[0]
    row = lambda n: pl.BlockSpec((ROW_TILE, n), lambda i: (i, 0))
    return pl.pallas_call(
        _out_odd_body,
        grid=(m // ROW_TILE,),
        in_specs=[row(D_MODEL), row(C_VD), _resident_layer(w, layer)],
        out_specs=row(D_MODEL),
        out_shape=jax.ShapeDtypeStruct((m, D_MODEL), F32),
        compiler_params=_cparams("parallel"),
        name="out_proj_odd",
    )(h, mixed, w)


def _ffn_body(hp_ref, h_ref, hn_ref, p_ref, gf_ref, wup_ref, cw_ref, cb_ref, wdn_ref, gp_ref, wg_ref, wp_ref,
              go_ref, y_ref, xe_ref, act_ref, *, tiles_per_seq, final):
    i = pl.program_id(0) % tiles_per_seq
    gf = gf_ref[...]
    h = h_ref[...]
    keep_prev = jnp.where(i > 0, 1.0, 0.0).astype(F32)
    keep_next = jnp.where(i < tiles_per_seq - 1, 1.0, 0.0).astype(F32)
    xe_ref[0:HALO, :] = _rms(hp_ref[...], gf) * keep_prev
    xe_ref[HALO:HALO + ROW_TILE, :] = _rms(h, gf)
    xe_ref[HALO + ROW_TILE:, :] = _rms(hn_ref[...], gf) * keep_next
    xe = xe_ref[...].astype(BF16)
    rows = ROW_TILE + 2 * HALO

    def conv(u, c):
        w = cw_ref[:, c:c + FF_CHUNK]
        prev = pltpu.roll(u, 1, 0)[HALO:HALO + ROW_TILE]
        nxt = pltpu.roll(u, rows - 1, 0)[HALO:HALO + ROW_TILE]
        return (prev * w[0:1] + u[HALO:HALO + ROW_TILE] * w[1:2] + nxt * w[2:3] + cb_ref[:, c:c + FF_CHUNK])

    for c in range(0, D_FF, FF_CHUNK):
        a = conv(_dot(xe, wup_ref[:, c:c + FF_CHUNK]), c)
        gt = conv(_dot(xe, wup_ref[:, D_FF + c:D_FF + c + FF_CHUNK]), D_FF + c)
        act_ref[:, c:c + FF_CHUNK] = (a * (0.5 * gt * (1.0 + lax.erf(gt * (2.0 ** -0.5))))).astype(BF16)
    h2 = h + _dot(act_ref[...], wdn_ref[...])
    sig = _sigmoid(_dot(_rms(h2, gp_ref[...]).astype(BF16), wg_ref[...]))
    h3 = h2 + sig * _dot(p_ref[...].astype(BF16), wp_ref[...])
    y_ref[...] = _rms(h3, go_ref[...]) if final else h3


def _ffn_ple(h, p, t, gf, wup, cw, cb, wdn, gp, wg, wp, go, layer, final):
    m = h.shape[0]
    per_tile = ROW_TILE // HALO
    last_blk = m // HALO - 1
    vec = lambda n: pl.BlockSpec((1, n), lambda i: (0, 0))
    return pl.pallas_call(
        functools.partial(_ffn_body, tiles_per_seq=t // ROW_TILE, final=final),
        grid=(m // ROW_TILE,),
        in_specs=[pl.BlockSpec((HALO, D_MODEL), lambda i: (jnp.maximum(i * per_tile - 1, 0), 0)),
                  pl.BlockSpec((ROW_TILE, D_MODEL), lambda i: (i, 0)),
                  pl.BlockSpec((HALO, D_MODEL), lambda i: (jnp.minimum((i + 1) * per_tile, last_blk), 0)),
                  pl.BlockSpec((None, ROW_TILE, PLE_DIM), lambda i: (layer, i, 0)),
                  vec(D_MODEL), _resident_layer(wup, layer), _resident_layer(cw, layer), vec(2 * D_FF),
                  _resident_layer(wdn, layer), vec(D_MODEL), _resident_layer(wg, layer), _resident_layer(wp, layer),
                  vec(D_MODEL)],
        out_specs=pl.BlockSpec((ROW_TILE, D_MODEL), lambda i: (i, 0)),
        out_shape=jax.ShapeDtypeStruct((m, D_MODEL), F32),
        scratch_shapes=[pltpu.VMEM((ROW_TILE + 2 * HALO, D_MODEL), F32), pltpu.VMEM((ROW_TILE, D_FF), BF16)],
        compiler_params=_cparams("parallel"),
        name="conv_ffn_ple",
    )(h, h, h, p, gf, wup, cw, cb, wdn, gp, wg, wp, go)


def _prep_weights(ev_w_in, od_w_in, gla_w_gate_up, gla_b_gate):
    ev_w = ev_w_in.astype(BF16)
    od_w = jnp.pad(od_w_in, ((0, 0), (0, 0), (0, LR_PAD - 2 * C_RANK))).astype(BF16)
    n_odd = od_w_in.shape[0]
    w2 = jnp.zeros((n_odd, LR_PAD, 2 * C_KD), F32)
    w2 = w2.at[:, :C_RANK, :C_KD].set(gla_w_gate_up[:, 0]).at[:, C_RANK:2 * C_RANK, C_KD:].set(gla_w_gate_up[:, 1])
    b2 = gla_b_gate.reshape(n_odd, 1, 2 * C_KD)
    return ev_w, od_w, w2.astype(BF16), b2


def _trunk(x, p, prm):
    bsz, t, _ = x.shape
    m = bsz * t
    h = x.reshape(m, D_MODEL)
    for l in range(DEPTH):
        g_mix = prm["norm_mix_g"][l][None]
        if l % 2 == 0:
            e = l // 2
            lb = jnp.concatenate([prm["lb"][0, e], prm["lb"][1, e]])[None]
            a16, a32, *qkv = _in_proj_even(h, g_mix, lb, prm["ev_w"], e)
            out_a = _hgrn_mixer(a16.reshape(bsz, t, -1), a32.reshape(bsz, t, -1), prm["hgrn_norm_g"][e][None])
            att = [_dilated_attention(x_g.reshape(bsz, t // dil, dil * QKV_W), gi)
                   for gi, (x_g, (_, dil)) in enumerate(zip(qkv, B_GROUPS))]
            h = _out_even(h, out_a.reshape(m, A_W),
                          [o.reshape(m // dil, dil * A_W) for (o, _), (_, dil) in zip(att, B_GROUPS)],
                          [l_.reshape(m // dil, dil * LSE_W) for (_, l_), (_, dil) in zip(att, B_GROUPS)],
                          prm["ev_w_out"], e)
        else:
            o = l // 2
            c16, lg = _in_proj_odd(h, g_mix, prm["od_w"], prm["od_w2"], prm["od_b2"][o], o)
            mixed = _gla_mixer(c16.reshape(bsz, t, -1), lg.reshape(bsz, t, -1), prm["gla_norm_g"][o][None])
            h = _out_odd(h, mixed.reshape(m, C_VD), prm["od_w_out"], o)
        h = _ffn_ple(h, p.reshape(DEPTH, m, PLE_DIM), t, prm["norm_ffn_g"][l][None], prm["ffn_w_up"],
                     prm["ffn_conv_w"], prm["ffn_conv_b"][l][None], prm["ffn_w_down"],
                     prm["norm_ple_g"][l][None], prm["ple_w_gate"], prm["ple_w_proj"],
                     prm["norm_out_g"][None], layer=l, final=(l == DEPTH - 1))
    return h.reshape(bsz, t, D_MODEL)


def kernel(x_prompt, x_sample, p_prompt, p_sample, norm_mix_g, ev_w_in, hgrn_lb_logits, hgrn_norm_g, ev_w_out, od_w_in, gla_w_gate_up, gla_b_gate, gla_norm_g, od_w_out, norm_ffn_g, ffn_w_up, ffn_conv_w, ffn_conv_b, ffn_w_down, norm_ple_g, ple_w_gate, ple_w_proj, norm_out_g):
    lb = jnp.cumsum(jax.nn.softmax(hgrn_lb_logits.astype(F32), axis=1), axis=1)
    lb = lb - lb[:, :1]
    ev_w, od_w, od_w2, od_b2 = _prep_weights(ev_w_in, od_w_in, gla_w_gate_up, gla_b_gate)
    prm = dict(norm_mix_g=norm_mix_g, ev_w=ev_w, lb=lb, hgrn_norm_g=hgrn_norm_g, ev_w_out=ev_w_out.astype(BF16),
               od_w=od_w, od_w2=od_w2, od_b2=od_b2, gla_norm_g=gla_norm_g, od_w_out=od_w_out.astype(BF16),
               norm_ffn_g=norm_ffn_g, ffn_w_up=ffn_w_up.astype(BF16), ffn_conv_w=ffn_conv_w,
               ffn_conv_b=ffn_conv_b, ffn_w_down=ffn_w_down.astype(BF16), norm_ple_g=norm_ple_g,
               ple_w_gate=ple_w_gate.astype(BF16), ple_w_proj=ple_w_proj.astype(BF16), norm_out_g=norm_out_g)
    return _trunk(x_prompt, p_prompt, prm), _trunk(x_sample, p_sample, prm)
```

```python
import functools

import numpy as np
import jax
import jax.numpy as jnp
from jax import lax
from jax.experimental import pallas as pl
from jax.experimental.pallas import tpu as pltpu

F32 = jnp.float32
BF16 = jnp.bfloat16

D_MODEL = 1024
DEPTH = 4
PLE_DIM = 256
EPS = 1e-6
HEADS = 4
HD = 128
A_W = HEADS * HD
B_GROUPS = ((128, 1), (512, 4), (2048, 16))
N_GROUPS = len(B_GROUPS)
QKV_W = 3 * A_W
ATT_Q = 128
ATT_CLASSES = 4
ALIBI_MAX_EXP = 8.0
C_KD = 512
C_VD = 1024
C_HDV = C_VD // HEADS
C_RANK = 16
GATE_NORMALIZER = 16.0
CHUNK = 64
SUB = 16
N_SUB = CHUNK // SUB
SUB8 = 8
CHUNKS_PER_STEP = {"chunk": 16, "block": 8, "pair": 4}
LOG_DECAY_MIN = -30.0
LOG2E = 1.4426950408889634
SAFE_LOG2_RANGE = {"chunk": 116.0, "block": 96.0}
NEG_INF = -1e30
D_FF = 2816
LSE_W = 128
LSE_REP = LSE_W // HEADS
LR_PAD = 128

ROW_TILE = 512
HALO = 8
FF_CHUNK = 256
VMEM_LIMIT = 56 * 1024 * 1024


def _cparams(*sem):
    return pltpu.CompilerParams(dimension_semantics=sem, vmem_limit_bytes=VMEM_LIMIT)


def _resident(shape):
    nd = len(shape)
    return pl.BlockSpec(shape, lambda *_: (0,) * nd, pipeline_mode=pl.Buffered(1))


def _resident_layer(stacked, layer):
    nd = stacked.ndim
    return pl.BlockSpec((None,) + stacked.shape[1:], lambda *_: (layer,) + (0,) * (nd - 1),
                        pipeline_mode=pl.Buffered(1))


def _rms(x, g):
    ms = jnp.mean(x * x, axis=-1, keepdims=True)
    return x * lax.rsqrt(ms + EPS) * g


def _dot(a, b):
    return jnp.dot(a, b, preferred_element_type=F32)


def _dot_nt(a, b):
    return lax.dot_general(a, b, (((1,), (1,)), ((), ())), preferred_element_type=F32)


def _dot_tn(a, b):
    return lax.dot_general(a, b, (((0,), (0,)), ((), ())), preferred_element_type=F32)


def _store_cols(xn, w_ref, out_refs, c0=0, col_chunk=512):
    for o_ref in out_refs:
        n = o_ref.shape[-1]
        for j in range(0, n, col_chunk):
            wj = min(col_chunk, n - j)
            o_ref[:, j:j + wj] = _dot(xn, w_ref[:, c0 + j:c0 + j + wj]).astype(o_ref.dtype)
        c0 += n
    return c0


def _clamp_log2_decay(log2_f):
    return jnp.maximum(log2_f, LOG_DECAY_MIN * LOG2E)


def _in_proj_even_body(x_ref, g_ref, lb_ref, w_ref, a16_ref, a32_ref, g0_ref, g1_ref, g2_ref, u_ref):
    xn = _rms(x_ref[...], g_ref[...]).astype(BF16)
    _store_cols(xn, w_ref, (a16_ref.at[:, :A_W],))
    _store_cols(xn, w_ref, (a16_ref.at[:, A_W:3 * A_W],), c0=3 * A_W)
    c0 = 5 * A_W
    for d in range(2):
        lb = lb_ref[:, d * A_W:(d + 1) * A_W]
        f = lb + (1.0 - lb) * _sigmoid(_dot(xn, w_ref[:, (1 + d) * A_W:(2 + d) * A_W]))
        a32_ref[:, d * A_W:(d + 1) * A_W] = _clamp_log2_decay(jnp.log2(f))
        a16_ref[:, (3 + d) * A_W:(4 + d) * A_W] = (1.0 - f).astype(BF16)
    for (_, dil), o_ref in zip(B_GROUPS, (g0_ref, g1_ref, g2_ref)):
        n = ROW_TILE // dil
        for j in range(3):
            u = _dot(xn, w_ref[:, c0:c0 + A_W])
            c0 += A_W
            if j == 0:
                u = u * (HD ** -0.5)
            if dil == 1:
                o_ref[:, j * A_W:(j + 1) * A_W] = u.astype(BF16)
            else:
                for c in range(HEADS):
                    u_ref[c] = u[:, c * HD:(c + 1) * HD]
                for r in range(dil):
                    for c in range(HEADS):
                        col = (3 * r + j) * A_W + c * HD
                        o_ref[:, col:col + HD] = u_ref[c, pl.ds(r, n, stride=dil), :].astype(BF16)


def _in_proj_odd_body(x_ref, g_ref, w_ref, w2_ref, b2_ref, c16_ref, g2_ref):
    xn = _rms(x_ref[...], g_ref[...]).astype(BF16)
    c16_ref[:, :C_KD] = (_dot(xn, w_ref[:, :C_KD]) * (HD ** -0.5)).astype(BF16)
    c0 = _store_cols(xn, w_ref, (c16_ref.at[:, C_KD:],), c0=C_KD)
    lr = _dot(xn, w_ref[:, c0:c0 + LR_PAD]).astype(BF16)
    n = g2_ref.shape[-1]
    for j in range(0, n, 512):
        logit = _dot(lr, w2_ref[:, j:j + 512]) + b2_ref[:, j:j + 512]
        g2_ref[:, j:j + 512] = _clamp_log2_decay(_log2_sigmoid(logit) * (1.0 / GATE_NORMALIZER))


def _in_proj_even(x, g, lb, w, layer):
    m = x.shape[0]
    outs = [((m, 5 * A_W), (ROW_TILE, 5 * A_W), BF16), ((m, 2 * A_W), (ROW_TILE, 2 * A_W), F32)]
    for _, dil in B_GROUPS:
        outs.append(((m // dil, dil * QKV_W), (ROW_TILE // dil, dil * QKV_W), BF16))
    return pl.pallas_call(
        _in_proj_even_body,
        grid=(m // ROW_TILE,),
        in_specs=[pl.BlockSpec((ROW_TILE, D_MODEL), lambda i: (i, 0)),
                  pl.BlockSpec((1, D_MODEL), lambda i: (0, 0)),
                  pl.BlockSpec((1, 2 * A_W), lambda i: (0, 0)),
                  _resident_layer(w, layer)],
        out_specs=[pl.BlockSpec(blk, lambda i: (i, 0)) for _, blk, _ in outs],
        out_shape=[jax.ShapeDtypeStruct(shp, dt) for shp, _, dt in outs],
        scratch_shapes=[pltpu.VMEM((HEADS, ROW_TILE, HD), F32)],
        compiler_params=_cparams("parallel"),
        name="in_proj_even",
    )(x, g, lb, w)


def _in_proj_odd(x, g, w, w2, b2, layer):
    m = x.shape[0]
    n16 = 2 * C_KD + 2 * C_VD
    return pl.pallas_call(
        _in_proj_odd_body,
        grid=(m // ROW_TILE,),
        in_specs=[pl.BlockSpec((ROW_TILE, D_MODEL), lambda i: (i, 0)),
                  pl.BlockSpec((1, D_MODEL), lambda i: (0, 0)),
                  _resident_layer(w, layer), _resident_layer(w2, layer),
                  pl.BlockSpec((1, 2 * C_KD), lambda i: (0, 0))],
        out_specs=[pl.BlockSpec((ROW_TILE, n16), lambda i: (i, 0)),
                   pl.BlockSpec((ROW_TILE, 2 * C_KD), lambda i: (i, 0))],
        out_shape=[jax.ShapeDtypeStruct((m, n16), BF16), jax.ShapeDtypeStruct((m, 2 * C_KD), F32)],
        compiler_params=_cparams("parallel"),
        name="in_proj_odd",
    )(x, g, w, w2, b2)


def _off_ranges(rev):
    out = []
    for j in range(N_SUB):
        lo, hi = (0, SUB * j) if rev else (SUB * (j + 1), CHUNK)
        if hi > lo:
            out.append((j, lo, hi))
    return out


def _gla_constants(rev):
    t = np.arange(CHUNK)
    tri = (t[None, :] >= t[:, None]) if rev else (t[None, :] <= t[:, None])
    causal = tri
    same8 = (t[None, :] // SUB8) == (t[:, None] // SUB8)
    same16 = (t[None, :] // SUB) == (t[:, None] // SUB)
    f = lambda a: jnp.asarray(a.astype(np.float32))
    return (jnp.asarray(np.concatenate([tri, tri], axis=1).astype(np.float32), BF16),
            f(causal), f(same16 & causal), f(same8 & causal), f(same16 & ~same8 & causal))


def _gla_shared_constants():
    e = np.zeros((SUB8 * HD, CHUNK), np.float32)
    for s in range(SUB8):
        e[s * HD:(s + 1) * HD, s::SUB8] = 1.0
    r = np.arange(ROW_TILE)
    sel = [(r[None, :] // n == np.arange(ROW_TILE // n)[:, None]).astype(np.float32) for n in (SUB, CHUNK)]
    return jnp.asarray(e, BF16), jnp.asarray(np.concatenate(sel, axis=0), BF16)


def _rows(ref, r, n):
    return jnp.broadcast_to(ref[pl.ds(r, 1), :], (n, HD))


def _gla_cumsum(k, g2, b_ref, k_ref, tri_ref, form):
    g_hi = g2.astype(BF16)
    g_lo = (g2 - g_hi.astype(F32)).astype(BF16)
    b = _dot(tri_ref[...], jnp.concatenate([g_hi, g_lo], axis=0))
    b_ref[...] = b
    if form == "pair":
        k_ref[...] = k
    return b


def _gla_scores(qs, k, v, b, st_ref, b_ref, k_ref, emat_ref, rev, form):
    b_tot = b_ref[pl.ds(0 if rev else CHUNK - 1, 1), :]

    st = st_ref[...]
    qd = qs * jnp.exp2(b)
    kd = (k * jnp.exp2(b_tot - b)).astype(BF16)
    o = _dot_nt(qd.astype(BF16), st.astype(BF16))
    st_ref[...] = st * jnp.exp2(b_tot) + _dot_tn(v, kd)
    if form == "chunk":
        return o, _dot_nt((qd * jnp.exp2(-b_tot)).astype(BF16), kd)

    edge16 = [SUB * i + (0 if rev else SUB - 1) for i in range(N_SUB)]
    e_ko = jnp.exp2(jnp.concatenate([_rows(b_ref, e, SUB) for e in edge16], axis=0) - b)
    k_off = k * e_ko
    zq = jnp.zeros((SUB, HD), F32)
    q_parts, k_parts = [], []
    for j, lo, hi in _off_ranges(rev):
        q_off = qs[lo:hi] * jnp.exp2(jnp.minimum(b[lo:hi] - _rows(b_ref, edge16[j], 1), 0.0))
        q_parts.append(jnp.concatenate([zq] * (lo // SUB) + [q_off] + [zq] * ((CHUNK - hi) // SUB), axis=0))
        k_parts.append(jnp.concatenate([zq] * j + [k_off[SUB * j:SUB * (j + 1)]] + [zq] * (N_SUB - 1 - j),
                                       axis=0))
    a_16 = _dot_nt(jnp.concatenate(q_parts, axis=1).astype(BF16),
                   jnp.concatenate(k_parts, axis=1).astype(BF16))

    if form == "block":
        return o, a_16, _dot_nt((qs * (1.0 / e_ko)).astype(BF16), k_off.astype(BF16))

    edge8 = [SUB * i + (SUB8 if rev else SUB8 - 1) for i in range(N_SUB)]
    d8 = b - jnp.concatenate([_rows(b_ref, e, SUB) for e in edge8], axis=0)
    second = (lax.broadcasted_iota(jnp.int32, (CHUNK, HD), 0) % SUB) >= SUB8
    q_side = jnp.logical_not(second) if rev else second
    w8 = jnp.exp2(jnp.minimum(jnp.where(q_side, d8, -d8), 0.0))
    a_8 = _dot_nt(jnp.where(q_side, qs * w8, 0.0).astype(BF16), jnp.where(q_side, 0.0, k * w8).astype(BF16))

    slabs = []
    for s in range(SUB8):
        rows = [SUB8 * i + s for i in range(CHUNK // SUB8)]
        b_s = jnp.concatenate([_rows(b_ref, r, SUB8) for r in rows], axis=0)
        k_s = jnp.concatenate([_rows(k_ref, r, SUB8) for r in rows], axis=0)
        slabs.append((qs * k_s * jnp.exp2(jnp.minimum(b - b_s, 0.0))).astype(BF16))
    a_dg = _dot(jnp.concatenate(slabs, axis=1), emat_ref[...])
    return o, a_16, a_8, a_dg


def _gla_output(scores, v, m_c_ref, m_16_ref, m_dg_ref, m_8_ref, form):
    if form == "chunk":
        o, a_c = scores
        a = jnp.where(m_c_ref[...] > 0.0, a_c, 0.0)
    elif form == "block":
        o, a_16, a_in = scores
        a = a_in * m_16_ref[...] + a_16
    else:
        o, a_16, a_8, a_dg = scores
        a = a_dg * m_dg_ref[...] + a_8 * m_8_ref[...] + a_16
    return o + _dot(a.astype(BF16), v)


def _decay_ranges_ok(g_ref, sel_ref, n_tiles):
    sums = [_dot(sel_ref[...], g_ref[i * ROW_TILE:(i + 1) * ROW_TILE, :].astype(BF16)) for i in range(n_tiles)]
    low = functools.reduce(jnp.minimum, sums)
    n16 = ROW_TILE // SUB
    return (jnp.min(low[n16:]) > -SAFE_LOG2_RANGE["chunk"], jnp.min(low[:n16]) > -SAFE_LOG2_RANGE["block"])


def _sigmoid(x):
    return 1.0 / (1.0 + jnp.exp(-x))


def _log2_sigmoid(x):
    return jnp.minimum(x, 0.0) * LOG2E - jnp.log2(1.0 + jnp.exp2(jnp.abs(x) * -LOG2E))


def _bidir_body(q_ref, kf_ref, kb_ref, g2f_ref, g2b_ref, v_ref, gate_ref, ng_ref,
                trif_ref, mcf_ref, m16f_ref, mdf_ref, m8f_ref, trib_ref, mcb_ref, m16b_ref, mdb_ref, m8b_ref,
                e_ref, sel_ref,
                o_ref, accf_ref, accb_ref, sf_ref, sb_ref, stage_ref, *, nc):
    def run(form):
        per = CHUNKS_PER_STEP[form]
        sf_ref[...] = jnp.zeros_like(sf_ref)
        sb_ref[...] = jnp.zeros_like(sb_ref)

        def step(c, carry):
            streams = []
            for u, rev in [(u, rev) for u in range(per) for rev in (False, True)]:
                cf = c * per + u
                sl = pl.ds(pl.multiple_of((nc - 1 - cf if rev else cf) * CHUNK, CHUNK), CHUNK)
                b_ref = stage_ref.at[4 * u + 2 * int(rev)]
                kst_ref = stage_ref.at[4 * u + 2 * int(rev) + 1]
                k = (kb_ref if rev else kf_ref)[sl, :].astype(F32)
                qs = q_ref[sl, :].astype(F32)
                b = _gla_cumsum(k, (g2b_ref if rev else g2f_ref)[sl, :], b_ref, kst_ref,
                                trib_ref if rev else trif_ref, form)
                streams.append((rev, sl, qs, k, b, b_ref, kst_ref))
            scores = [_gla_scores(qs, k, v_ref[sl, :], b, sb_ref if rev else sf_ref, b_ref, kst_ref, e_ref,
                                  rev, form)
                      for rev, sl, qs, k, b, b_ref, kst_ref in streams]
            for (rev, sl, *_), sc in zip(streams, scores):
                masks = (mcb_ref, m16b_ref, mdb_ref, m8b_ref) if rev else (mcf_ref, m16f_ref, mdf_ref, m8f_ref)
                (accb_ref if rev else accf_ref)[sl, :] = _gla_output(sc, v_ref[sl, :], *masks, form)
            return carry

        lax.fori_loop(0, nc // per, step, 0)

    n_tiles = (nc * CHUNK) // ROW_TILE
    chunk_f, block_f = _decay_ranges_ok(g2f_ref, sel_ref, n_tiles)
    chunk_b, block_b = _decay_ranges_ok(g2b_ref, sel_ref, n_tiles)
    chunk_ok = jnp.logical_and(chunk_f, chunk_b)
    block_ok = jnp.logical_and(jnp.logical_and(block_f, block_b), jnp.logical_not(chunk_ok))
    pl.when(chunk_ok)(functools.partial(run, "chunk"))
    pl.when(block_ok)(functools.partial(run, "block"))
    pl.when(jnp.logical_not(jnp.logical_or(chunk_ok, block_ok)))(functools.partial(run, "pair"))

    ng = ng_ref[...]

    def finish(i, carry):
        sl = pl.ds(pl.multiple_of(i * ROW_TILE, ROW_TILE), ROW_TILE)
        o = accf_ref[sl, :] + accb_ref[sl, :]
        gt = gate_ref[sl, :].astype(F32)
        o_ref[sl, :] = (_rms(o, ng) * (gt * _sigmoid(gt))).astype(o_ref.dtype)
        return carry

    lax.fori_loop(0, (nc * CHUNK) // ROW_TILE, finish, 0)


def _bidir_call(name, bsz, t, dv, args, in_specs):
    consts = (*_gla_constants(False), *_gla_constants(True), *_gla_shared_constants())
    const = lambda shape: pl.BlockSpec(shape, lambda b, h: (0,) * len(shape))
    return pl.pallas_call(
        functools.partial(_bidir_body, nc=t // CHUNK),
        grid=(bsz, HEADS),
        in_specs=in_specs + [const(c.shape) for c in consts],
        out_specs=pl.BlockSpec((None, t, dv), lambda b, h: (b, 0, h)),
        out_shape=jax.ShapeDtypeStruct((bsz, t, HEADS * dv), BF16),
        scratch_shapes=[pltpu.VMEM((t, dv), F32), pltpu.VMEM((t, dv), F32),
                        pltpu.VMEM((dv, HD), F32), pltpu.VMEM((dv, HD), F32)]
                       + [pltpu.VMEM((4 * max(CHUNKS_PER_STEP.values()), CHUNK, HD), F32)],
        compiler_params=_cparams("parallel", "parallel"),
        name=name,
    )(*args, *consts)


def _hgrn_mixer(a16, a32, norm_g):
    bsz, t, _ = a16.shape
    col = lambda off: pl.BlockSpec((None, t, HD), lambda b, h: (b, 0, off * HEADS + h))
    return _bidir_call("hgrn2_bidir", bsz, t, HD,
                       (a16, a16, a16, a32, a32, a16, a16, norm_g),
                       [col(0), col(3), col(4), col(0), col(1), col(1), col(2),
                        pl.BlockSpec((1, HD), lambda b, h: (0, h))])


def _gla_mixer(c16, g2, norm_g):
    bsz, t, _ = c16.shape
    col = lambda off: pl.BlockSpec((None, t, HD), lambda b, h: (b, 0, off * HEADS + h))
    wide = lambda off: pl.BlockSpec((None, t, C_HDV), lambda b, h: (b, 0, off + h))
    v_off = 2 * C_KD // C_HDV
    return _bidir_call("gla_bidir", bsz, t, C_HDV,
                       (c16, c16, c16, g2, g2, c16, c16, norm_g),
                       [col(0), col(1), col(1), col(0), col(1), wide(v_off), wide(v_off + HEADS),
                        pl.BlockSpec((1, C_HDV), lambda b, h: (0, h))])


def _attn_window(length, radius):
    win = min(ATT_Q + 2 * radius, length)
    return win, (0, -radius, ATT_Q - win)


def _attn_bias(gi, length):
    wsize, dil = B_GROUPS[gi]
    radius = wsize // (2 * dil)
    win, offsets = _attn_window(length, radius)
    n = N_GROUPS * HEADS
    slopes = np.array([2.0 ** (-ALIBI_MAX_EXP * (gi * HEADS + h + 1) / n) for h in range(HEADS)])
    dist = np.abs(np.array(offsets)[:, None, None] + np.arange(win)[None, None, :] - np.arange(ATT_Q)[None, :, None])
    bias = np.where(dist[:, None] <= radius, -slopes[None, :, None, None] * (dil * dist[:, None]), NEG_INF)
    return jnp.asarray(bias, F32)


def _attn_body(x_ref, bias_ref, o_ref, l_ref, *, length, radius, classes):
    win, _ = _attn_window(length, radius)
    n_tiles = length // ATT_Q
    per = max(n for n in (4, 2, 1) if n * classes <= ATT_CLASSES and n_tiles % n == 0)

    def step(i, carry):
        probs = []
        for u in range(per):
            tile = i * per + u
            m0 = pl.multiple_of(tile * ATT_Q, ATT_Q)
            k0 = pl.multiple_of(jnp.clip(m0 - radius, 0, length - win), radius)
            place = jnp.where(tile == 0, 0, jnp.where(tile == n_tiles - 1, 2, 1))
            probs += [(m0, k0, place, r, h) for r in range(classes) for h in range(HEADS)]
        col = lambda r, j, h: pl.ds((3 * r + j) * A_W + h * HD, HD)
        s = [_dot_nt(x_ref[pl.ds(m0, ATT_Q), col(r, 0, h)], x_ref[pl.ds(k0, win), col(r, 1, h)])
             for m0, k0, _, r, h in probs]
        s = [x + bias_ref[place, h] for x, (_, _, place, _, h) in zip(s, probs)]
        mx = [jnp.max(x, axis=-1, keepdims=True) for x in s]
        p = [jnp.exp(x - m) for x, m in zip(s, mx)]
        den = [jnp.sum(x, axis=-1, keepdims=True) for x in p]
        o = [_dot(x.astype(BF16), x_ref[pl.ds(k0, win), col(r, 2, h)]) for x, (_, k0, _, r, h) in zip(p, probs)]
        for x, d, m, (m0, _, _, r, h) in zip(o, den, mx, probs):
            o_ref[pl.ds(m0, ATT_Q), pl.ds(r * A_W + h * HD, HD)] = (x * (1.0 / d)).astype(o_ref.dtype)
            l_ref[pl.ds(m0, ATT_Q), pl.ds(r * LSE_W + h * LSE_REP, LSE_REP)] = jnp.broadcast_to(
                m + jnp.log(d), (ATT_Q, LSE_REP))
        return carry

    lax.fori_loop(0, n_tiles // per, step, 0)


def _dilated_attention(qkv, gi):
    bsz, length, _ = qkv.shape
    wsize, dil = B_GROUPS[gi]
    classes = min(dil, ATT_CLASSES)
    bias = _attn_bias(gi, length)
    return pl.pallas_call(
        functools.partial(_attn_body, length=length, radius=wsize // (2 * dil), classes=classes),
        grid=(bsz, dil // classes),
        in_specs=[pl.BlockSpec((None, length, classes * QKV_W), lambda b, r: (b, 0, r)),
                  pl.BlockSpec(bias.shape, lambda b, r: (0, 0, 0, 0))],
        out_specs=[pl.BlockSpec((None, length, classes * A_W), lambda b, r: (b, 0, r)),
                   pl.BlockSpec((None, length, classes * LSE_W), lambda b, r: (b, 0, r))],
        out_shape=[jax.ShapeDtypeStruct((bsz, length, dil * A_W), BF16),
                   jax.ShapeDtypeStruct((bsz, length, dil * LSE_W), F32)],
        compiler_params=_cparams("parallel", "parallel"),
        name=f"dilated_attn_g{gi}",
    )(qkv, bias)


def _lse_expansion():
    e = np.zeros((LSE_W, A_W), np.float32)
    for h in range(HEADS):
        e[h * LSE_REP, h * HD:(h + 1) * HD] = 1.0
    return jnp.asarray(e, BF16)


def _out_even_body(h_ref, a_ref, o0_ref, o1_ref, o2_ref, l0_ref, l1_ref, l2_ref, e_ref, w_ref, y_ref,
                   ob1_ref, ob2_ref, lb1_ref, lb2_ref):
    def natural(o_ref, l_ref, ob_ref, lb_ref, dil):
        if dil == 1:
            return o_ref[...].astype(F32), l_ref[...]
        n = ROW_TILE // dil
        for r in range(dil):
            for c in range(HEADS):
                col = r * A_W + c * HD
                ob_ref[c, pl.ds(r, n, stride=dil), :] = o_ref[:, col:col + HD].astype(F32)
            lb_ref[pl.ds(r, n, stride=dil), :] = l_ref[:, r * LSE_W:(r + 1) * LSE_W]
        return jnp.concatenate([ob_ref[c] for c in range(HEADS)], axis=1), lb_ref[...]

    parts = [natural(o0_ref, l0_ref, None, None, B_GROUPS[0][1]),
             natural(o1_ref, l1_ref, ob1_ref, lb1_ref, B_GROUPS[1][1]),
             natural(o2_ref, l2_ref, ob2_ref, lb2_ref, B_GROUPS[2][1])]
    lses = [l for _, l in parts]
    mx = jnp.maximum(jnp.maximum(lses[0], lses[1]), lses[2])
    es = [jnp.exp(l - mx) for l in lses]
    inv = 1.0 / (es[0] + es[1] + es[2])
    emat = e_ref[...]
    mixed_b = None
    for e_g, (o_g, _) in zip(es, parts):
        alpha = (e_g * inv).astype(BF16)
        term = _dot(alpha, emat) * o_g
        mixed_b = term if mixed_b is None else mixed_b + term
    y_ref[...] = (h_ref[...] + _dot(a_ref[...], w_ref[:A_W, :]) + _dot(mixed_b.astype(BF16), w_ref[A_W:, :]))


def _out_even(h, out_a, os_, lses, w, layer):
    m = h.shape[0]
    emat = _lse_expansion()
    row = lambda n: pl.BlockSpec((ROW_TILE, n), lambda i: (i, 0))
    grp = lambda width: [pl.BlockSpec((ROW_TILE // dil, dil * width), lambda i: (i, 0)) for _, dil in B_GROUPS]
    return pl.pallas_call(
        _out_even_body,
        grid=(m // ROW_TILE,),
        in_specs=[row(D_MODEL), row(A_W)] + grp(A_W) + grp(LSE_W) + [_resident(emat.shape), _resident_layer(w, layer)],
        out_specs=row(D_MODEL),
        out_shape=jax.ShapeDtypeStruct((m, D_MODEL), F32),
        scratch_shapes=[pltpu.VMEM((HEADS, ROW_TILE, HD), F32), pltpu.VMEM((HEADS, ROW_TILE, HD), F32),
                        pltpu.VMEM((ROW_TILE, LSE_W), F32), pltpu.VMEM((ROW_TILE, LSE_W), F32)],
        compiler_params=_cparams("parallel"),
        name="out_proj_even",
    )(h, out_a, *os_, *lses, emat, w)


def _out_odd_body(h_ref, a_ref, w_ref, y_ref):
    y_ref[...] = h_ref[...] + _dot(a_ref[...], w_ref[...])


def _out_odd(h, mixed, w, layer):
    m = h.shape[0]
    row = lambda n: pl.BlockSpec((ROW_TILE, n), lambda i: (i, 0))
    return pl.pallas_call(
        _out_odd_body,
        grid=(m // ROW_TILE,),
        in_specs=[row(D_MODEL), row(C_VD), _resident_layer(w, layer)],
        out_specs=row(D_MODEL),
        out_shape=jax.ShapeDtypeStruct((m, D_MODEL), F32),
        compiler_params=_cparams("parallel"),
        name="out_proj_odd",
    )(h, mixed, w)


def _ffn_body(hp_ref, h_ref, hn_ref, p_ref, gf_ref, wup_ref, cw_ref, cb_ref, wdn_ref, gp_ref, wg_ref, wp_ref,
              go_ref, y_ref, xe_ref, act_ref, *, tiles_per_seq, final):
    i = pl.program_id(0) % tiles_per_seq
    gf = gf_ref[...]
    h = h_ref[...]
    keep_prev = jnp.where(i > 0, 1.0, 0.0).astype(F32)
    keep_next = jnp.where(i < tiles_per_seq - 1, 1.0, 0.0).astype(F32)
    xe_ref[0:HALO, :] = _rms(hp_ref[...], gf) * keep_prev
    xe_ref[HALO:HALO + ROW_TILE, :] = _rms(h, gf)
    xe_ref[HALO + ROW_TILE:, :] = _rms(hn_ref[...], gf) * keep_next
    xe = xe_ref[...].astype(BF16)
    rows = ROW_TILE + 2 * HALO

    def conv(u, c):
        w = cw_ref[:, c:c + FF_CHUNK]
        prev = pltpu.roll(u, 1, 0)[HALO:HALO + ROW_TILE]
        nxt = pltpu.roll(u, rows - 1, 0)[HALO:HALO + ROW_TILE]
        return (prev * w[0:1] + u[HALO:HALO + ROW_TILE] * w[1:2] + nxt * w[2:3] + cb_ref[:, c:c + FF_CHUNK])

    for c in range(0, D_FF, FF_CHUNK):
        a = conv(_dot(xe, wup_ref[:, c:c + FF_CHUNK]), c)
        gt = conv(_dot(xe, wup_ref[:, D_FF + c:D_FF + c + FF_CHUNK]), D_FF + c)
        act_ref[:, c:c + FF_CHUNK] = (a * (0.5 * gt * (1.0 + lax.erf(gt * (2.0 ** -0.5))))).astype(BF16)
    h2 = h + _dot(act_ref[...], wdn_ref[...])
    sig = _sigmoid(_dot(_rms(h2, gp_ref[...]).astype(BF16), wg_ref[...]))
    h3 = h2 + sig * _dot(p_ref[...].astype(BF16), wp_ref[...])
    y_ref[...] = _rms(h3, go_ref[...]) if final else h3


def _ffn_ple(h, p, t, gf, wup, cw, cb, wdn, gp, wg, wp, go, layer, final):
    m = h.shape[0]
    per_tile = ROW_TILE // HALO
    last_blk = m // HALO - 1
    vec = lambda n: pl.BlockSpec((1, n), lambda i: (0, 0))
    return pl.pallas_call(
        functools.partial(_ffn_body, tiles_per_seq=t // ROW_TILE, final=final),
        grid=(m // ROW_TILE,),
        in_specs=[pl.BlockSpec((HALO, D_MODEL), lambda i: (jnp.maximum(i * per_tile - 1, 0), 0)),
                  pl.BlockSpec((ROW_TILE, D_MODEL), lambda i: (i, 0)),
                  pl.BlockSpec((HALO, D_MODEL), lambda i: (jnp.minimum((i + 1) * per_tile, last_blk), 0)),
                  pl.BlockSpec((None, ROW_TILE, PLE_DIM), lambda i: (layer, i, 0)),
                  vec(D_MODEL), _resident_layer(wup, layer), _resident_layer(cw, layer), vec(2 * D_FF),
                  _resident_layer(wdn, layer), vec(D_MODEL), _resident_layer(wg, layer), _resident_layer(wp, layer),
                  vec(D_MODEL)],
        out_specs=pl.BlockSpec((ROW_TILE, D_MODEL), lambda i: (i, 0)),
        out_shape=jax.ShapeDtypeStruct((m, D_MODEL), F32),
        scratch_shapes=[pltpu.VMEM((ROW_TILE + 2 * HALO, D_MODEL), F32), pltpu.VMEM((ROW_TILE, D_FF), BF16)],
        compiler_params=_cparams("parallel"),
        name="conv_ffn_ple",
    )(h, h, h, p, gf, wup, cw, cb, wdn, gp, wg, wp, go)


def _prep_weights(ev_w_in, od_w_in, gla_w_gate_up, gla_b_gate):
    ev_w = ev_w_in.astype(BF16)
    od_w = jnp.pad(od_w_in, ((0, 0), (0, 0), (0, LR_PAD - 2 * C_RANK))).astype(BF16)
    n_odd = od_w_in.shape[0]
    w2 = jnp.zeros((n_odd, LR_PAD, 2 * C_KD), F32)
    w2 = w2.at[:, :C_RANK, :C_KD].set(gla_w_gate_up[:, 0]).at[:, C_RANK:2 * C_RANK, C_KD:].set(gla_w_gate_up[:, 1])
    b2 = gla_b_gate.reshape(n_odd, 1, 2 * C_KD)
    return ev_w, od_w, w2.astype(BF16), b2


def _trunk(x, p, prm):
    bsz, t, _ = x.shape
    m = bsz * t
    h = x.reshape(m, D_MODEL)
    for l in range(DEPTH):
        g_mix = prm["norm_mix_g"][l][None]
        if l % 2 == 0:
            e = l // 2
            lb = jnp.concatenate([prm["lb"][0, e], prm["lb"][1, e]])[None]
            a16, a32, *qkv = _in_proj_even(h, g_mix, lb, prm["ev_w"], e)
            out_a = _hgrn_mixer(a16.reshape(bsz, t, -1), a32.reshape(bsz, t, -1), prm["hgrn_norm_g"][e][None])
            att = [_dilated_attention(x_g.reshape(bsz, t // dil, dil * QKV_W), gi)
                   for gi, (x_g, (_, dil)) in enumerate(zip(qkv, B_GROUPS))]
            h = _out_even(h, out_a.reshape(m, A_W),
                          [o.reshape(m // dil, dil * A_W) for (o, _), (_, dil) in zip(att, B_GROUPS)],
                          [l_.reshape(m // dil, dil * LSE_W) for (_, l_), (_, dil) in zip(att, B_GROUPS)],
                          prm["ev_w_out"], e)
        else:
            o = l // 2
            c16, lg = _in_proj_odd(h, g_mix, prm["od_w"], prm["od_w2"], prm["od_b2"][o], o)
            mixed = _gla_mixer(c16.reshape(bsz, t, -1), lg.reshape(bsz, t, -1), prm["gla_norm_g"][o][None])
            h = _out_odd(h, mixed.reshape(m, C_VD), prm["od_w_out"], o)
        h = _ffn_ple(h, p.reshape(DEPTH, m, PLE_DIM), t, prm["norm_ffn_g"][l][None], prm["ffn_w_up"],
                     prm["ffn_conv_w"], prm["ffn_conv_b"][l][None], prm["ffn_w_down"],
                     prm["norm_ple_g"][l][None], prm["ple_w_gate"], prm["ple_w_proj"],
                     prm["norm_out_g"][None], layer=l, final=(l == DEPTH - 1))
    return h.reshape(bsz, t, D_MODEL)


def kernel(x_prompt, x_sample, p_prompt, p_sample, norm_mix_g, ev_w_in, hgrn_lb_logits, hgrn_norm_g, ev_w_out, od_w_in, gla_w_gate_up, gla_b_gate, gla_norm_g, od_w_out, norm_ffn_g, ffn_w_up, ffn_conv_w, ffn_conv_b, ffn_w_down, norm_ple_g, ple_w_gate, ple_w_proj, norm_out_g):
    lb = jnp.cumsum(jax.nn.softmax(hgrn_lb_logits.astype(F32), axis=1), axis=1)
    lb = lb - lb[:, :1]
    ev_w, od_w, od_w2, od_b2 = _prep_weights(ev_w_in, od_w_in, gla_w_gate_up, gla_b_gate)
    prm = dict(norm_mix_g=norm_mix_g, ev_w=ev_w, lb=lb, hgrn_norm_g=hgrn_norm_g, ev_w_out=ev_w_out.astype(BF16),
               od_w=od_w, od_w2=od_w2, od_b2=od_b2, gla_norm_g=gla_norm_g, od_w_out=od_w_out.astype(BF16),
               norm_ffn_g=norm_ffn_g, ffn_w_up=ffn_w_up.astype(BF16), ffn_conv_w=ffn_conv_w,
               ffn_conv_b=ffn_conv_b, ffn_w_down=ffn_w_down.astype(BF16), norm_ple_g=norm_ple_g,
               ple_w_gate=ple_w_gate.astype(BF16), ple_w_proj=ple_w_proj.astype(BF16), norm_out_g=norm_out_g)
    return _trunk(x_prompt, p_prompt, prm), _trunk(x_sample, p_sample, prm)
```

```python
import functools

import numpy as np
import jax
import jax.numpy as jnp
from jax import lax
from jax.experimental import pallas as pl
from jax.experimental.pallas import tpu as pltpu

F32 = jnp.float32
BF16 = jnp.bfloat16

D_MODEL = 1024
DEPTH = 4
PLE_DIM = 256
EPS = 1e-6
HEADS = 4
HD = 128
A_W = HEADS * HD
B_GROUPS = ((128, 1), (512, 4), (2048, 16))
N_GROUPS = len(B_GROUPS)
QKV_W = 3 * A_W
ATT_Q = 128
ATT_CLASSES = 4
ALIBI_MAX_EXP = 8.0
C_KD = 512
C_VD = 1024
C_HDV = C_VD // HEADS
C_RANK = 16
GATE_NORMALIZER = 16.0
CHUNK = 64
SUB = 16
N_SUB = CHUNK // SUB
SUB8 = 8
CHUNKS_PER_STEP = {"chunk": 16, "block": 8, "pair": 4}
LOG_DECAY_MIN = -30.0
LOG2E = 1.4426950408889634
SAFE_LOG2_RANGE = {"chunk": 116.0, "block": 96.0}
NEG_INF = -1e30
D_FF = 2816
LSE_W = 128
LSE_REP = LSE_W // HEADS
LR_PAD = 128

ROW_TILE = 512
WIDE_TILE = 1024
HALO = 8
FF_CHUNK = 256
VMEM_LIMIT = 56 * 1024 * 1024


def _cparams(*sem):
    return pltpu.CompilerParams(dimension_semantics=sem, vmem_limit_bytes=VMEM_LIMIT)


def _resident(shape):
    nd = len(shape)
    return pl.BlockSpec(shape, lambda *_: (0,) * nd, pipeline_mode=pl.Buffered(1))


def _resident_layer(stacked, layer):
    nd = stacked.ndim
    return pl.BlockSpec((None,) + stacked.shape[1:], lambda *_: (layer,) + (0,) * (nd - 1),
                        pipeline_mode=pl.Buffered(1))


def _rms(x, g):
    ms = jnp.mean(x * x, axis=-1, keepdims=True)
    return x * lax.rsqrt(ms + EPS) * g


def _dot(a, b):
    return jnp.dot(a, b, preferred_element_type=F32)


def _dot_nt(a, b):
    return lax.dot_general(a, b, (((1,), (1,)), ((), ())), preferred_element_type=F32)


def _dot_tn(a, b):
    return lax.dot_general(a, b, (((0,), (0,)), ((), ())), preferred_element_type=F32)


def _store_cols(xn, w_ref, out_refs, c0=0, col_chunk=512):
    for o_ref in out_refs:
        n = o_ref.shape[-1]
        for j in range(0, n, col_chunk):
            wj = min(col_chunk, n - j)
            o_ref[:, j:j + wj] = _dot(xn, w_ref[:, c0 + j:c0 + j + wj]).astype(o_ref.dtype)
        c0 += n
    return c0


def _clamp_log2_decay(log2_f):
    return jnp.maximum(log2_f, LOG_DECAY_MIN * LOG2E)


def _in_proj_even_body(x_ref, g_ref, lb_ref, w_ref, a16_ref, a32_ref, g0_ref, g1_ref, g2_ref, u_ref):
    xn = _rms(x_ref[...], g_ref[...]).astype(BF16)
    _store_cols(xn, w_ref, (a16_ref.at[:, :A_W],))
    _store_cols(xn, w_ref, (a16_ref.at[:, A_W:3 * A_W],), c0=3 * A_W)
    c0 = 5 * A_W
    for d in range(2):
        lb = lb_ref[:, d * A_W:(d + 1) * A_W]
        f = lb + (1.0 - lb) * _sigmoid(_dot(xn, w_ref[:, (1 + d) * A_W:(2 + d) * A_W]))
        a32_ref[:, d * A_W:(d + 1) * A_W] = _clamp_log2_decay(jnp.log2(f))
        a16_ref[:, (3 + d) * A_W:(4 + d) * A_W] = (1.0 - f).astype(BF16)
    for (_, dil), o_ref in zip(B_GROUPS, (g0_ref, g1_ref, g2_ref)):
        n = ROW_TILE // dil
        for j in range(3):
            u = _dot(xn, w_ref[:, c0:c0 + A_W])
            c0 += A_W
            if j == 0:
                u = u * (HD ** -0.5)
            if dil == 1:
                o_ref[:, j * A_W:(j + 1) * A_W] = u.astype(BF16)
            else:
                for c in range(HEADS):
                    u_ref[c] = u[:, c * HD:(c + 1) * HD]
                for r in range(dil):
                    for c in range(HEADS):
                        col = (3 * r + j) * A_W + c * HD
                        o_ref[:, col:col + HD] = u_ref[c, pl.ds(r, n, stride=dil), :].astype(BF16)


def _in_proj_odd_body(x_ref, g_ref, w_ref, w2_ref, b2_ref, c16_ref, g2_ref):
    xn = _rms(x_ref[...], g_ref[...]).astype(BF16)
    c16_ref[:, :C_KD] = (_dot(xn, w_ref[:, :C_KD]) * (HD ** -0.5)).astype(BF16)
    c0 = _store_cols(xn, w_ref, (c16_ref.at[:, C_KD:],), c0=C_KD)
    lr = _dot(xn, w_ref[:, c0:c0 + LR_PAD]).astype(BF16)
    n = g2_ref.shape[-1]
    for j in range(0, n, 512):
        logit = _dot(lr, w2_ref[:, j:j + 512]) + b2_ref[:, j:j + 512]
        g2_ref[:, j:j + 512] = _clamp_log2_decay(_log2_sigmoid(logit) * (1.0 / GATE_NORMALIZER))


def _in_proj_even(x, g, lb, w, layer):
    m = x.shape[0]
    outs = [((m, 5 * A_W), (ROW_TILE, 5 * A_W), BF16), ((m, 2 * A_W), (ROW_TILE, 2 * A_W), F32)]
    for _, dil in B_GROUPS:
        outs.append(((m // dil, dil * QKV_W), (ROW_TILE // dil, dil * QKV_W), BF16))
    return pl.pallas_call(
        _in_proj_even_body,
        grid=(m // ROW_TILE,),
        in_specs=[pl.BlockSpec((ROW_TILE, D_MODEL), lambda i: (i, 0)),
                  pl.BlockSpec((1, D_MODEL), lambda i: (0, 0)),
                  pl.BlockSpec((1, 2 * A_W), lambda i: (0, 0)),
                  _resident_layer(w, layer)],
        out_specs=[pl.BlockSpec(blk, lambda i: (i, 0)) for _, blk, _ in outs],
        out_shape=[jax.ShapeDtypeStruct(shp, dt) for shp, _, dt in outs],
        scratch_shapes=[pltpu.VMEM((HEADS, ROW_TILE, HD), F32)],
        compiler_params=_cparams("parallel"),
        name="in_proj_even",
    )(x, g, lb, w)


def _in_proj_odd(x, g, w, w2, b2, layer):
    m = x.shape[0]
    n16 = 2 * C_KD + 2 * C_VD
    return pl.pallas_call(
        _in_proj_odd_body,
        grid=(m // WIDE_TILE,),
        in_specs=[pl.BlockSpec((WIDE_TILE, D_MODEL), lambda i: (i, 0)),
                  pl.BlockSpec((1, D_MODEL), lambda i: (0, 0)),
                  _resident_layer(w, layer), _resident_layer(w2, layer),
                  pl.BlockSpec((1, 2 * C_KD), lambda i: (0, 0))],
        out_specs=[pl.BlockSpec((WIDE_TILE, n16), lambda i: (i, 0)),
                   pl.BlockSpec((WIDE_TILE, 2 * C_KD), lambda i: (i, 0))],
        out_shape=[jax.ShapeDtypeStruct((m, n16), BF16), jax.ShapeDtypeStruct((m, 2 * C_KD), F32)],
        compiler_params=_cparams("parallel"),
        name="in_proj_odd",
    )(x, g, w, w2, b2)


def _off_ranges(rev):
    out = []
    for j in range(N_SUB):
        lo, hi = (0, SUB * j) if rev else (SUB * (j + 1), CHUNK)
        if hi > lo:
            out.append((j, lo, hi))
    return out


def _gla_constants(rev):
    t = np.arange(CHUNK)
    tri = (t[None, :] >= t[:, None]) if rev else (t[None, :] <= t[:, None])
    causal = tri
    same8 = (t[None, :] // SUB8) == (t[:, None] // SUB8)
    same16 = (t[None, :] // SUB) == (t[:, None] // SUB)
    f = lambda a: jnp.asarray(a.astype(np.float32))
    return (jnp.asarray(np.concatenate([tri, tri], axis=1).astype(np.float32), BF16),
            f(causal), f(same16 & causal), f(same8 & causal), f(same16 & ~same8 & causal))


def _gla_shared_constants():
    e = np.zeros((SUB8 * HD, CHUNK), np.float32)
    for s in range(SUB8):
        e[s * HD:(s + 1) * HD, s::SUB8] = 1.0
    r = np.arange(ROW_TILE)
    sel = [(r[None, :] // n == np.arange(ROW_TILE // n)[:, None]).astype(np.float32) for n in (SUB, CHUNK)]
    return jnp.asarray(e, BF16), jnp.asarray(np.concatenate(sel, axis=0), BF16)


def _rows(ref, r, n):
    return jnp.broadcast_to(ref[pl.ds(r, 1), :], (n, HD))


def _gla_cumsum(k, g2, b_ref, k_ref, tri_ref, form):
    g_hi = g2.astype(BF16)
    g_lo = (g2 - g_hi.astype(F32)).astype(BF16)
    b = _dot(tri_ref[...], jnp.concatenate([g_hi, g_lo], axis=0))
    b_ref[...] = b
    if form == "pair":
        k_ref[...] = k
    return b


def _gla_scores(qs, k, v, b, st_ref, b_ref, k_ref, emat_ref, rev, form):
    b_tot = b_ref[pl.ds(0 if rev else CHUNK - 1, 1), :]

    st = st_ref[...]
    qd = qs * jnp.exp2(b)
    kd = (k * jnp.exp2(b_tot - b)).astype(BF16)
    o = _dot_nt(qd.astype(BF16), st.astype(BF16))
    st_ref[...] = st * jnp.exp2(b_tot) + _dot_tn(v, kd)
    if form == "chunk":
        return o, _dot_nt((qd * jnp.exp2(-b_tot)).astype(BF16), kd)

    edge16 = [SUB * i + (0 if rev else SUB - 1) for i in range(N_SUB)]
    e_ko = jnp.exp2(jnp.concatenate([_rows(b_ref, e, SUB) for e in edge16], axis=0) - b)
    k_off = k * e_ko
    zq = jnp.zeros((SUB, HD), F32)
    q_parts, k_parts = [], []
    for j, lo, hi in _off_ranges(rev):
        q_off = qs[lo:hi] * jnp.exp2(jnp.minimum(b[lo:hi] - _rows(b_ref, edge16[j], 1), 0.0))
        q_parts.append(jnp.concatenate([zq] * (lo // SUB) + [q_off] + [zq] * ((CHUNK - hi) // SUB), axis=0))
        k_parts.append(jnp.concatenate([zq] * j + [k_off[SUB * j:SUB * (j + 1)]] + [zq] * (N_SUB - 1 - j),
                                       axis=0))
    a_16 = _dot_nt(jnp.concatenate(q_parts, axis=1).astype(BF16),
                   jnp.concatenate(k_parts, axis=1).astype(BF16))

    if form == "block":
        return o, a_16, _dot_nt((qs * (1.0 / e_ko)).astype(BF16), k_off.astype(BF16))

    edge8 = [SUB * i + (SUB8 if rev else SUB8 - 1) for i in range(N_SUB)]
    d8 = b - jnp.concatenate([_rows(b_ref, e, SUB) for e in edge8], axis=0)
    second = (lax.broadcasted_iota(jnp.int32, (CHUNK, HD), 0) % SUB) >= SUB8
    q_side = jnp.logical_not(second) if rev else second
    w8 = jnp.exp2(jnp.minimum(jnp.where(q_side, d8, -d8), 0.0))
    a_8 = _dot_nt(jnp.where(q_side, qs * w8, 0.0).astype(BF16), jnp.where(q_side, 0.0, k * w8).astype(BF16))

    slabs = []
    for s in range(SUB8):
        rows = [SUB8 * i + s for i in range(CHUNK // SUB8)]
        b_s = jnp.concatenate([_rows(b_ref, r, SUB8) for r in rows], axis=0)
        k_s = jnp.concatenate([_rows(k_ref, r, SUB8) for r in rows], axis=0)
        slabs.append((qs * k_s * jnp.exp2(jnp.minimum(b - b_s, 0.0))).astype(BF16))
    a_dg = _dot(jnp.concatenate(slabs, axis=1), emat_ref[...])
    return o, a_16, a_8, a_dg


def _gla_output(scores, v, m_c_ref, m_16_ref, m_dg_ref, m_8_ref, form):
    if form == "chunk":
        o, a_c = scores
        a = jnp.where(m_c_ref[...] > 0.0, a_c, 0.0)
    elif form == "block":
        o, a_16, a_in = scores
        a = a_in * m_16_ref[...] + a_16
    else:
        o, a_16, a_8, a_dg = scores
        a = a_dg * m_dg_ref[...] + a_8 * m_8_ref[...] + a_16
    return o + _dot(a.astype(BF16), v)


def _decay_ranges_ok(g_ref, sel_ref, n_tiles):
    sums = [_dot(sel_ref[...], g_ref[i * ROW_TILE:(i + 1) * ROW_TILE, :].astype(BF16)) for i in range(n_tiles)]
    low = functools.reduce(jnp.minimum, sums)
    n16 = ROW_TILE // SUB
    return (jnp.min(low[n16:]) > -SAFE_LOG2_RANGE["chunk"], jnp.min(low[:n16]) > -SAFE_LOG2_RANGE["block"])


def _sigmoid(x):
    return 1.0 / (1.0 + jnp.exp(-x))


def _log2_sigmoid(x):
    return jnp.minimum(x, 0.0) * LOG2E - jnp.log2(1.0 + jnp.exp2(jnp.abs(x) * -LOG2E))


def _bidir_body(q_ref, kf_ref, kb_ref, g2f_ref, g2b_ref, v_ref, gate_ref, ng_ref,
                trif_ref, mcf_ref, m16f_ref, mdf_ref, m8f_ref, trib_ref, mcb_ref, m16b_ref, mdb_ref, m8b_ref,
                e_ref, sel_ref,
                o_ref, accf_ref, accb_ref, sf_ref, sb_ref, stage_ref, *, nc):
    def run(form):
        per = CHUNKS_PER_STEP[form]
        sf_ref[...] = jnp.zeros_like(sf_ref)
        sb_ref[...] = jnp.zeros_like(sb_ref)

        def step(c, carry):
            streams = []
            for u, rev in [(u, rev) for u in range(per) for rev in (False, True)]:
                cf = c * per + u
                sl = pl.ds(pl.multiple_of((nc - 1 - cf if rev else cf) * CHUNK, CHUNK), CHUNK)
                b_ref = stage_ref.at[4 * u + 2 * int(rev)]
                kst_ref = stage_ref.at[4 * u + 2 * int(rev) + 1]
                k = (kb_ref if rev else kf_ref)[sl, :].astype(F32)
                qs = q_ref[sl, :].astype(F32)
                b = _gla_cumsum(k, (g2b_ref if rev else g2f_ref)[sl, :], b_ref, kst_ref,
                                trib_ref if rev else trif_ref, form)
                streams.append((rev, sl, qs, k, b, b_ref, kst_ref))
            scores = [_gla_scores(qs, k, v_ref[sl, :], b, sb_ref if rev else sf_ref, b_ref, kst_ref, e_ref,
                                  rev, form)
                      for rev, sl, qs, k, b, b_ref, kst_ref in streams]
            for (rev, sl, *_), sc in zip(streams, scores):
                masks = (mcb_ref, m16b_ref, mdb_ref, m8b_ref) if rev else (mcf_ref, m16f_ref, mdf_ref, m8f_ref)
                (accb_ref if rev else accf_ref)[sl, :] = _gla_output(sc, v_ref[sl, :], *masks, form)
            return carry

        lax.fori_loop(0, nc // per, step, 0)

    n_tiles = (nc * CHUNK) // ROW_TILE
    chunk_f, block_f = _decay_ranges_ok(g2f_ref, sel_ref, n_tiles)
    chunk_b, block_b = _decay_ranges_ok(g2b_ref, sel_ref, n_tiles)
    chunk_ok = jnp.logical_and(chunk_f, chunk_b)
    block_ok = jnp.logical_and(jnp.logical_and(block_f, block_b), jnp.logical_not(chunk_ok))
    pl.when(chunk_ok)(functools.partial(run, "chunk"))
    pl.when(block_ok)(functools.partial(run, "block"))
    pl.when(jnp.logical_not(jnp.logical_or(chunk_ok, block_ok)))(functools.partial(run, "pair"))

    ng = ng_ref[...]

    def finish(i, carry):
        sl = pl.ds(pl.multiple_of(i * ROW_TILE, ROW_TILE), ROW_TILE)
        o = accf_ref[sl, :] + accb_ref[sl, :]
        gt = gate_ref[sl, :].astype(F32)
        o_ref[sl, :] = (_rms(o, ng) * (gt * _sigmoid(gt))).astype(o_ref.dtype)
        return carry

    lax.fori_loop(0, (nc * CHUNK) // ROW_TILE, finish, 0)


def _bidir_call(name, bsz, t, dv, args, in_specs):
    consts = (*_gla_constants(False), *_gla_constants(True), *_gla_shared_constants())
    const = lambda shape: pl.BlockSpec(shape, lambda b, h: (0,) * len(shape))
    return pl.pallas_call(
        functools.partial(_bidir_body, nc=t // CHUNK),
        grid=(bsz, HEADS),
        in_specs=in_specs + [const(c.shape) for c in consts],
        out_specs=pl.BlockSpec((None, t, dv), lambda b, h: (b, 0, h)),
        out_shape=jax.ShapeDtypeStruct((bsz, t, HEADS * dv), BF16),
        scratch_shapes=[pltpu.VMEM((t, dv), F32), pltpu.VMEM((t, dv), F32),
                        pltpu.VMEM((dv, HD), F32), pltpu.VMEM((dv, HD), F32)]
                       + [pltpu.VMEM((4 * max(CHUNKS_PER_STEP.values()), CHUNK, HD), F32)],
        compiler_params=_cparams("parallel", "parallel"),
        name=name,
    )(*args, *consts)


def _hgrn_mixer(a16, a32, norm_g):
    bsz, t, _ = a16.shape
    col = lambda off: pl.BlockSpec((None, t, HD), lambda b, h: (b, 0, off * HEADS + h))
    return _bidir_call("hgrn2_bidir", bsz, t, HD,
                       (a16, a16, a16, a32, a32, a16, a16, norm_g),
                       [col(0), col(3), col(4), col(0), col(1), col(1), col(2),
                        pl.BlockSpec((1, HD), lambda b, h: (0, h))])


def _gla_mixer(c16, g2, norm_g):
    bsz, t, _ = c16.shape
    col = lambda off: pl.BlockSpec((None, t, HD), lambda b, h: (b, 0, off * HEADS + h))
    wide = lambda off: pl.BlockSpec((None, t, C_HDV), lambda b, h: (b, 0, off + h))
    v_off = 2 * C_KD // C_HDV
    return _bidir_call("gla_bidir", bsz, t, C_HDV,
                       (c16, c16, c16, g2, g2, c16, c16, norm_g),
                       [col(0), col(1), col(1), col(0), col(1), wide(v_off), wide(v_off + HEADS),
                        pl.BlockSpec((1, C_HDV), lambda b, h: (0, h))])


def _attn_window(length, radius):
    win = min(ATT_Q + 2 * radius, length)
    return win, (0, -radius, ATT_Q - win)


def _attn_bias(gi, length):
    wsize, dil = B_GROUPS[gi]
    radius = wsize // (2 * dil)
    win, offsets = _attn_window(length, radius)
    n = N_GROUPS * HEADS
    slopes = np.array([2.0 ** (-ALIBI_MAX_EXP * (gi * HEADS + h + 1) / n) for h in range(HEADS)])
    dist = np.abs(np.array(offsets)[:, None, None] + np.arange(win)[None, None, :] - np.arange(ATT_Q)[None, :, None])
    bias = np.where(dist[:, None] <= radius, -slopes[None, :, None, None] * (dil * dist[:, None]), NEG_INF)
    return jnp.asarray(bias, F32)


def _attn_body(x_ref, bias_ref, o_ref, l_ref, *, length, radius, classes):
    win, _ = _attn_window(length, radius)
    n_tiles = length // ATT_Q
    per = max(n for n in (4, 2, 1) if n * classes <= ATT_CLASSES and n_tiles % n == 0)

    def step(i, carry):
        probs = []
        for u in range(per):
            tile = i * per + u
            m0 = pl.multiple_of(tile * ATT_Q, ATT_Q)
            k0 = pl.multiple_of(jnp.clip(m0 - radius, 0, length - win), radius)
            place = jnp.where(tile == 0, 0, jnp.where(tile == n_tiles - 1, 2, 1))
            probs += [(m0, k0, place, r, h) for r in range(classes) for h in range(HEADS)]
        col = lambda r, j, h: pl.ds((3 * r + j) * A_W + h * HD, HD)
        s = [_dot_nt(x_ref[pl.ds(m0, ATT_Q), col(r, 0, h)], x_ref[pl.ds(k0, win), col(r, 1, h)])
             for m0, k0, _, r, h in probs]
        s = [x + bias_ref[place, h] for x, (_, _, place, _, h) in zip(s, probs)]
        mx = [jnp.max(x, axis=-1, keepdims=True) for x in s]
        p = [jnp.exp(x - m) for x, m in zip(s, mx)]
        den = [jnp.sum(x, axis=-1, keepdims=True) for x in p]
        o = [_dot(x.astype(BF16), x_ref[pl.ds(k0, win), col(r, 2, h)]) for x, (_, k0, _, r, h) in zip(p, probs)]
        for x, d, m, (m0, _, _, r, h) in zip(o, den, mx, probs):
            o_ref[pl.ds(m0, ATT_Q), pl.ds(r * A_W + h * HD, HD)] = (x * (1.0 / d)).astype(o_ref.dtype)
            l_ref[pl.ds(m0, ATT_Q), pl.ds(r * LSE_W + h * LSE_REP, LSE_REP)] = jnp.broadcast_to(
                m + jnp.log(d), (ATT_Q, LSE_REP))
        return carry

    lax.fori_loop(0, n_tiles // per, step, 0)


def _dilated_attention(qkv, gi):
    bsz, length, _ = qkv.shape
    wsize, dil = B_GROUPS[gi]
    classes = min(dil, ATT_CLASSES)
    bias = _attn_bias(gi, length)
    return pl.pallas_call(
        functools.partial(_attn_body, length=length, radius=wsize // (2 * dil), classes=classes),
        grid=(bsz, dil // classes),
        in_specs=[pl.BlockSpec((None, length, classes * QKV_W), lambda b, r: (b, 0, r)),
                  pl.BlockSpec(bias.shape, lambda b, r: (0, 0, 0, 0))],
        out_specs=[pl.BlockSpec((None, length, classes * A_W), lambda b, r: (b, 0, r)),
                   pl.BlockSpec((None, length, classes * LSE_W), lambda b, r: (b, 0, r))],
        out_shape=[jax.ShapeDtypeStruct((bsz, length, dil * A_W), BF16),
                   jax.ShapeDtypeStruct((bsz, length, dil * LSE_W), F32)],
        compiler_params=_cparams("parallel", "parallel"),
        name=f"dilated_attn_g{gi}",
    )(qkv, bias)


def _lse_expansion():
    e = np.zeros((LSE_W, A_W), np.float32)
    for h in range(HEADS):
        e[h * LSE_REP, h * HD:(h + 1) * HD] = 1.0
    return jnp.asarray(e, BF16)


def _out_even_body(h_ref, a_ref, o0_ref, o1_ref, o2_ref, l0_ref, l1_ref, l2_ref, e_ref, w_ref, y_ref,
                   ob1_ref, ob2_ref, lb1_ref, lb2_ref):
    def natural(o_ref, l_ref, ob_ref, lb_ref, dil):
        if dil == 1:
            return o_ref[...].astype(F32), l_ref[...]
        n = WIDE_TILE // dil
        for r in range(dil):
            for c in range(HEADS):
                col = r * A_W + c * HD
                ob_ref[c, pl.ds(r, n, stride=dil), :] = o_ref[:, col:col + HD].astype(F32)
            lb_ref[pl.ds(r, n, stride=dil), :] = l_ref[:, r * LSE_W:(r + 1) * LSE_W]
        return jnp.concatenate([ob_ref[c] for c in range(HEADS)], axis=1), lb_ref[...]

    parts = [natural(o0_ref, l0_ref, None, None, B_GROUPS[0][1]),
             natural(o1_ref, l1_ref, ob1_ref, lb1_ref, B_GROUPS[1][1]),
             natural(o2_ref, l2_ref, ob2_ref, lb2_ref, B_GROUPS[2][1])]
    lses = [l for _, l in parts]
    mx = jnp.maximum(jnp.maximum(lses[0], lses[1]), lses[2])
    es = [jnp.exp(l - mx) for l in lses]
    inv = 1.0 / (es[0] + es[1] + es[2])
    emat = e_ref[...]
    mixed_b = None
    for e_g, (o_g, _) in zip(es, parts):
        alpha = (e_g * inv).astype(BF16)
        term = _dot(alpha, emat) * o_g
        mixed_b = term if mixed_b is None else mixed_b + term
    y_ref[...] = (h_ref[...] + _dot(a_ref[...], w_ref[:A_W, :]) + _dot(mixed_b.astype(BF16), w_ref[A_W:, :]))


def _out_even(h, out_a, os_, lses, w, layer):
    m = h.shape[0]
    emat = _lse_expansion()
    row = lambda n: pl.BlockSpec((WIDE_TILE, n), lambda i: (i, 0))
    grp = lambda width: [pl.BlockSpec((WIDE_TILE // dil, dil * width), lambda i: (i, 0)) for _, dil in B_GROUPS]
    return pl.pallas_call(
        _out_even_body,
        grid=(m // WIDE_TILE,),
        in_specs=[row(D_MODEL), row(A_W)] + grp(A_W) + grp(LSE_W) + [_resident(emat.shape), _resident_layer(w, layer)],
        out_specs=row(D_MODEL),
        out_shape=jax.ShapeDtypeStruct((m, D_MODEL), F32),
        scratch_shapes=[pltpu.VMEM((HEADS, WIDE_TILE, HD), F32), pltpu.VMEM((HEADS, WIDE_TILE, HD), F32),
                        pltpu.VMEM((WIDE_TILE, LSE_W), F32), pltpu.VMEM((WIDE_TILE, LSE_W), F32)],
        compiler_params=_cparams("parallel"),
        name="out_proj_even",
    )(h, out_a, *os_, *lses, emat, w)


def _out_odd_body(h_ref, a_ref, w_ref, y_ref):
    y_ref[...] = h_ref[...] + _dot(a_ref[...], w_ref[...])


def _out_odd(h, mixed, w, layer):
    m = h.shape[0]
    row = lambda n: pl.BlockSpec((WIDE_TILE, n), lambda i: (i, 0))
    return pl.pallas_call(
        _out_odd_body,
        grid=(m // WIDE_TILE,),
        in_specs=[row(D_MODEL), row(C_VD), _resident_layer(w, layer)],
        out_specs=row(D_MODEL),
        out_shape=jax.ShapeDtypeStruct((m, D_MODEL), F32),
        compiler_params=_cparams("parallel"),
        name="out_proj_odd",
    )(h, mixed, w)


def _ffn_body(hp_ref, h_ref, hn_ref, p_ref, gf_ref, wup_ref, cw_ref, cb_ref, wdn_ref, gp_ref, wg_ref, wp_ref,
              go_ref, y_ref, xe_ref, act_ref, *, tiles_per_seq, final):
    i = pl.program_id(0) % tiles_per_seq
    gf = gf_ref[...]
    h = h_ref[...]
    keep_prev = jnp.where(i > 0, 1.0, 0.0).astype(F32)
    keep_next = jnp.where(i < tiles_per_seq - 1, 1.0, 0.0).astype(F32)
    xe_ref[0:HALO, :] = _rms(hp_ref[...], gf) * keep_prev
    xe_ref[HALO:HALO + ROW_TILE, :] = _rms(h, gf)
    xe_ref[HALO + ROW_TILE:, :] = _rms(hn_ref[...], gf) * keep_next
    xe = xe_ref[...].astype(BF16)
    rows = ROW_TILE + 2 * HALO

    def conv(u, c):
        w = cw_ref[:, c:c + FF_CHUNK]
        prev = pltpu.roll(u, 1, 0)[HALO:HALO + ROW_TILE]
        nxt = pltpu.roll(u, rows - 1, 0)[HALO:HALO + ROW_TILE]
        return (prev * w[0:1] + u[HALO:HALO + ROW_TILE] * w[1:2] + nxt * w[2:3] + cb_ref[:, c:c + FF_CHUNK])

    for c in range(0, D_FF, FF_CHUNK):
        a = conv(_dot(xe, wup_ref[:, c:c + FF_CHUNK]), c)
        gt = conv(_dot(xe, wup_ref[:, D_FF + c:D_FF + c + FF_CHUNK]), D_FF + c)
        act_ref[:, c:c + FF_CHUNK] = (a * (0.5 * gt * (1.0 + lax.erf(gt * (2.0 ** -0.5))))).astype(BF16)
    h2 = h + _dot(act_ref[...], wdn_ref[...])
    sig = _sigmoid(_dot(_rms(h2, gp_ref[...]).astype(BF16), wg_ref[...]))
    h3 = h2 + sig * _dot(p_ref[...].astype(BF16), wp_ref[...])
    y_ref[...] = _rms(h3, go_ref[...]) if final else h3


def _ffn_ple(h, p, t, gf, wup, cw, cb, wdn, gp, wg, wp, go, layer, final):
    m = h.shape[0]
    per_tile = ROW_TILE // HALO
    last_blk = m // HALO - 1
    vec = lambda n: pl.BlockSpec((1, n), lambda i: (0, 0))
    return pl.pallas_call(
        functools.partial(_ffn_body, tiles_per_seq=t // ROW_TILE, final=final),
        grid=(m // ROW_TILE,),
        in_specs=[pl.BlockSpec((HALO, D_MODEL), lambda i: (jnp.maximum(i * per_tile - 1, 0), 0)),
                  pl.BlockSpec((ROW_TILE, D_MODEL), lambda i: (i, 0)),
                  pl.BlockSpec((HALO, D_MODEL), lambda i: (jnp.minimum((i + 1) * per_tile, last_blk), 0)),
                  pl.BlockSpec((None, ROW_TILE, PLE_DIM), lambda i: (layer, i, 0)),
                  vec(D_MODEL), _resident_layer(wup, layer), _resident_layer(cw, layer), vec(2 * D_FF),
                  _resident_layer(wdn, layer), vec(D_MODEL), _resident_layer(wg, layer), _resident_layer(wp, layer),
                  vec(D_MODEL)],
        out_specs=pl.BlockSpec((ROW_TILE, D_MODEL), lambda i: (i, 0)),
        out_shape=jax.ShapeDtypeStruct((m, D_MODEL), F32),
        scratch_shapes=[pltpu.VMEM((ROW_TILE + 2 * HALO, D_MODEL), F32), pltpu.VMEM((ROW_TILE, D_FF), BF16)],
        compiler_params=_cparams("parallel"),
        name="conv_ffn_ple",
    )(h, h, h, p, gf, wup, cw, cb, wdn, gp, wg, wp, go)


def _prep_weights(ev_w_in, od_w_in, gla_w_gate_up, gla_b_gate):
    ev_w = ev_w_in.astype(BF16)
    od_w = jnp.pad(od_w_in, ((0, 0), (0, 0), (0, LR_PAD - 2 * C_RANK))).astype(BF16)
    n_odd = od_w_in.shape[0]
    w2 = jnp.zeros((n_odd, LR_PAD, 2 * C_KD), F32)
    w2 = w2.at[:, :C_RANK, :C_KD].set(gla_w_gate_up[:, 0]).at[:, C_RANK:2 * C_RANK, C_KD:].set(gla_w_gate_up[:, 1])
    b2 = gla_b_gate.reshape(n_odd, 1, 2 * C_KD)
    return ev_w, od_w, w2.astype(BF16), b2


def _trunk(x, p, prm):
    bsz, t, _ = x.shape
    m = bsz * t
    h = x.reshape(m, D_MODEL)
    for l in range(DEPTH):
        g_mix = prm["norm_mix_g"][l][None]
        if l % 2 == 0:
            e = l // 2
            lb = jnp.concatenate([prm["lb"][0, e], prm["lb"][1, e]])[None]
            a16, a32, *qkv = _in_proj_even(h, g_mix, lb, prm["ev_w"], e)
            out_a = _hgrn_mixer(a16.reshape(bsz, t, -1), a32.reshape(bsz, t, -1), prm["hgrn_norm_g"][e][None])
            att = [_dilated_attention(x_g.reshape(bsz, t // dil, dil * QKV_W), gi)
                   for gi, (x_g, (_, dil)) in enumerate(zip(qkv, B_GROUPS))]
            h = _out_even(h, out_a.reshape(m, A_W),
                          [o.reshape(m // dil, dil * A_W) for (o, _), (_, dil) in zip(att, B_GROUPS)],
                          [l_.reshape(m // dil, dil * LSE_W) for (_, l_), (_, dil) in zip(att, B_GROUPS)],
                          prm["ev_w_out"], e)
        else:
            o = l // 2
            c16, lg = _in_proj_odd(h, g_mix, prm["od_w"], prm["od_w2"], prm["od_b2"][o], o)
            mixed = _gla_mixer(c16.reshape(bsz, t, -1), lg.reshape(bsz, t, -1), prm["gla_norm_g"][o][None])
            h = _out_odd(h, mixed.reshape(m, C_VD), prm["od_w_out"], o)
        h = _ffn_ple(h, p.reshape(DEPTH, m, PLE_DIM), t, prm["norm_ffn_g"][l][None], prm["ffn_w_up"],
                     prm["ffn_conv_w"], prm["ffn_conv_b"][l][None], prm["ffn_w_down"],
                     prm["norm_ple_g"][l][None], prm["ple_w_gate"], prm["ple_w_proj"],
                     prm["norm_out_g"][None], layer=l, final=(l == DEPTH - 1))
    return h.reshape(bsz, t, D_MODEL)


def kernel(x_prompt, x_sample, p_prompt, p_sample, norm_mix_g, ev_w_in, hgrn_lb_logits, hgrn_norm_g, ev_w_out, od_w_in, gla_w_gate_up, gla_b_gate, gla_norm_g, od_w_out, norm_ffn_g, ffn_w_up, ffn_conv_w, ffn_conv_b, ffn_w_down, norm_ple_g, ple_w_gate, ple_w_proj, norm_out_g):
    lb = jnp.cumsum(jax.nn.softmax(hgrn_lb_logits.astype(F32), axis=1), axis=1)
    lb = lb - lb[:, :1]
    ev_w, od_w, od_w2, od_b2 = _prep_weights(ev_w_in, od_w_in, gla_w_gate_up, gla_b_gate)
    prm = dict(norm_mix_g=norm_mix_g, ev_w=ev_w, lb=lb, hgrn_norm_g=hgrn_norm_g, ev_w_out=ev_w_out.astype(BF16),
               od_w=od_w, od_w2=od_w2, od_b2=od_b2, gla_norm_g=gla_norm_g, od_w_out=od_w_out.astype(BF16),
               norm_ffn_g=norm_ffn_g, ffn_w_up=ffn_w_up.astype(BF16), ffn_conv_w=ffn_conv_w,
               ffn_conv_b=ffn_conv_b, ffn_w_down=ffn_w_down.astype(BF16), norm_ple_g=norm_ple_g,
               ple_w_gate=ple_w_gate.astype(BF16), ple_w_proj=ple_w_proj.astype(BF16), norm_out_g=norm_out_g)
    return _trunk(x_prompt, p_prompt, prm), _trunk(x_sample, p_sample, prm)
```

```python
import functools

import numpy as np
import jax
import jax.numpy as jnp
from jax import lax
from jax.experimental import pallas as pl
from jax.experimental.pallas import tpu as pltpu

F32 = jnp.float32
BF16 = jnp.bfloat16

D_MODEL = 1024
DEPTH = 4
PLE_DIM = 256
EPS = 1e-6
HEADS = 4
HD = 128
A_W = HEADS * HD
B_GROUPS = ((128, 1), (512, 4), (2048, 16))
N_GROUPS = len(B_GROUPS)
QKV_W = 3 * A_W
ATT_Q = 128
ATT_CLASSES = 4
ALIBI_MAX_EXP = 8.0
C_KD = 512
C_VD = 1024
C_HDV = C_VD // HEADS
C_RANK = 16
GATE_NORMALIZER = 16.0
CHUNK = 64
SUB = 16
N_SUB = CHUNK // SUB
SUB8 = 8
CHUNKS_PER_STEP = {"chunk": 16, "block": 8, "pair": 4}
LOG_DECAY_MIN = -30.0
LOG2E = 1.4426950408889634
SAFE_LOG2_RANGE = {"chunk": 116.0, "block": 96.0}
NEG_INF = -1e30
D_FF = 2816
LSE_W = 128
LSE_REP = LSE_W // HEADS
LR_PAD = 128

ROW_TILE = 512
WIDE_TILE = 1024
HALO = 8
FF_CHUNK = 256
VMEM_LIMIT = 56 * 1024 * 1024
VMEM_LIMIT_FFN = 60 * 1024 * 1024


def _cparams(*sem):
    return pltpu.CompilerParams(dimension_semantics=sem, vmem_limit_bytes=VMEM_LIMIT)


def _resident(shape):
    nd = len(shape)
    return pl.BlockSpec(shape, lambda *_: (0,) * nd, pipeline_mode=pl.Buffered(1))


def _resident_layer(stacked, layer):
    nd = stacked.ndim
    return pl.BlockSpec((None,) + stacked.shape[1:], lambda *_: (layer,) + (0,) * (nd - 1),
                        pipeline_mode=pl.Buffered(1))


def _rms(x, g):
    ms = jnp.mean(x * x, axis=-1, keepdims=True)
    return x * lax.rsqrt(ms + EPS) * g


def _dot(a, b):
    return jnp.dot(a, b, preferred_element_type=F32)


def _dot_nt(a, b):
    return lax.dot_general(a, b, (((1,), (1,)), ((), ())), preferred_element_type=F32)


def _dot_tn(a, b):
    return lax.dot_general(a, b, (((0,), (0,)), ((), ())), preferred_element_type=F32)


def _store_cols(xn, w_ref, out_refs, c0=0, col_chunk=512):
    for o_ref in out_refs:
        n = o_ref.shape[-1]
        for j in range(0, n, col_chunk):
            wj = min(col_chunk, n - j)
            o_ref[:, j:j + wj] = _dot(xn, w_ref[:, c0 + j:c0 + j + wj]).astype(o_ref.dtype)
        c0 += n
    return c0


def _clamp_log2_decay(log2_f):
    return jnp.maximum(log2_f, LOG_DECAY_MIN * LOG2E)


def _in_proj_even_body(x_ref, g_ref, lb_ref, w_ref, a16_ref, a32_ref, g0_ref, g1_ref, g2_ref, u_ref):
    xn = _rms(x_ref[...], g_ref[...]).astype(BF16)
    _store_cols(xn, w_ref, (a16_ref.at[:, :A_W],))
    _store_cols(xn, w_ref, (a16_ref.at[:, A_W:3 * A_W],), c0=3 * A_W)
    c0 = 5 * A_W
    for d in range(2):
        lb = lb_ref[:, d * A_W:(d + 1) * A_W]
        f = lb + (1.0 - lb) * _sigmoid(_dot(xn, w_ref[:, (1 + d) * A_W:(2 + d) * A_W]))
        a32_ref[:, d * A_W:(d + 1) * A_W] = _clamp_log2_decay(jnp.log2(f))
        a16_ref[:, (3 + d) * A_W:(4 + d) * A_W] = (1.0 - f).astype(BF16)
    for (_, dil), o_ref in zip(B_GROUPS, (g0_ref, g1_ref, g2_ref)):
        n = ROW_TILE // dil
        for j in range(3):
            u = _dot(xn, w_ref[:, c0:c0 + A_W])
            c0 += A_W
            if j == 0:
                u = u * (HD ** -0.5)
            if dil == 1:
                o_ref[:, j * A_W:(j + 1) * A_W] = u.astype(BF16)
            else:
                for c in range(HEADS):
                    u_ref[c] = u[:, c * HD:(c + 1) * HD]
                for r in range(dil):
                    for c in range(HEADS):
                        col = (3 * r + j) * A_W + c * HD
                        o_ref[:, col:col + HD] = u_ref[c, pl.ds(r, n, stride=dil), :].astype(BF16)


def _in_proj_odd_body(x_ref, g_ref, w_ref, w2_ref, b2_ref, c16_ref, g2_ref):
    xn = _rms(x_ref[...], g_ref[...]).astype(BF16)
    c16_ref[:, :C_KD] = (_dot(xn, w_ref[:, :C_KD]) * (HD ** -0.5)).astype(BF16)
    c0 = _store_cols(xn, w_ref, (c16_ref.at[:, C_KD:],), c0=C_KD)
    lr = _dot(xn, w_ref[:, c0:c0 + LR_PAD]).astype(BF16)
    n = g2_ref.shape[-1]
    for j in range(0, n, 512):
        logit = _dot(lr, w2_ref[:, j:j + 512]) + b2_ref[:, j:j + 512]
        g2_ref[:, j:j + 512] = _clamp_log2_decay(_log2_sigmoid(logit) * (1.0 / GATE_NORMALIZER))


def _in_proj_even(x, g, lb, w, layer):
    m = x.shape[0]
    outs = [((m, 5 * A_W), (ROW_TILE, 5 * A_W), BF16), ((m, 2 * A_W), (ROW_TILE, 2 * A_W), F32)]
    for _, dil in B_GROUPS:
        outs.append(((m // dil, dil * QKV_W), (ROW_TILE // dil, dil * QKV_W), BF16))
    return pl.pallas_call(
        _in_proj_even_body,
        grid=(m // ROW_TILE,),
        in_specs=[pl.BlockSpec((ROW_TILE, D_MODEL), lambda i: (i, 0)),
                  pl.BlockSpec((1, D_MODEL), lambda i: (0, 0)),
                  pl.BlockSpec((1, 2 * A_W), lambda i: (0, 0)),
                  _resident_layer(w, layer)],
        out_specs=[pl.BlockSpec(blk, lambda i: (i, 0)) for _, blk, _ in outs],
        out_shape=[jax.ShapeDtypeStruct(shp, dt) for shp, _, dt in outs],
        scratch_shapes=[pltpu.VMEM((HEADS, ROW_TILE, HD), F32)],
        compiler_params=_cparams("parallel"),
        name="in_proj_even",
    )(x, g, lb, w)


def _in_proj_odd(x, g, w, w2, b2, layer):
    m = x.shape[0]
    n16 = 2 * C_KD + 2 * C_VD
    return pl.pallas_call(
        _in_proj_odd_body,
        grid=(m // WIDE_TILE,),
        in_specs=[pl.BlockSpec((WIDE_TILE, D_MODEL), lambda i: (i, 0)),
                  pl.BlockSpec((1, D_MODEL), lambda i: (0, 0)),
                  _resident_layer(w, layer), _resident_layer(w2, layer),
                  pl.BlockSpec((1, 2 * C_KD), lambda i: (0, 0))],
        out_specs=[pl.BlockSpec((WIDE_TILE, n16), lambda i: (i, 0)),
                   pl.BlockSpec((WIDE_TILE, 2 * C_KD), lambda i: (i, 0))],
        out_shape=[jax.ShapeDtypeStruct((m, n16), BF16), jax.ShapeDtypeStruct((m, 2 * C_KD), F32)],
        compiler_params=_cparams("parallel"),
        name="in_proj_odd",
    )(x, g, w, w2, b2)


def _off_ranges(rev):
    out = []
    for j in range(N_SUB):
        lo, hi = (0, SUB * j) if rev else (SUB * (j + 1), CHUNK)
        if hi > lo:
            out.append((j, lo, hi))
    return out


def _gla_constants(rev):
    t = np.arange(CHUNK)
    tri = (t[None, :] >= t[:, None]) if rev else (t[None, :] <= t[:, None])
    causal = tri
    same8 = (t[None, :] // SUB8) == (t[:, None] // SUB8)
    same16 = (t[None, :] // SUB) == (t[:, None] // SUB)
    f = lambda a: jnp.asarray(a.astype(np.float32))
    return (jnp.asarray(np.concatenate([tri, tri], axis=1).astype(np.float32), BF16),
            f(causal), f(same16 & causal), f(same8 & causal), f(same16 & ~same8 & causal))


def _gla_shared_constants():
    e = np.zeros((SUB8 * HD, CHUNK), np.float32)
    for s in range(SUB8):
        e[s * HD:(s + 1) * HD, s::SUB8] = 1.0
    r = np.arange(ROW_TILE)
    sel = [(r[None, :] // n == np.arange(ROW_TILE // n)[:, None]).astype(np.float32) for n in (SUB, CHUNK)]
    return jnp.asarray(e, BF16), jnp.asarray(np.concatenate(sel, axis=0), BF16)


def _rows(ref, r, n):
    return jnp.broadcast_to(ref[pl.ds(r, 1), :], (n, HD))


def _gla_cumsum(k, g2, b_ref, k_ref, tri_ref, form):
    g_hi = g2.astype(BF16)
    g_lo = (g2 - g_hi.astype(F32)).astype(BF16)
    b = _dot(tri_ref[...], jnp.concatenate([g_hi, g_lo], axis=0))
    b_ref[...] = b
    if form == "pair":
        k_ref[...] = k
    return b


def _gla_scores(qs, k, v, b, st_ref, b_ref, k_ref, emat_ref, rev, form):
    b_tot = b_ref[pl.ds(0 if rev else CHUNK - 1, 1), :]

    st = st_ref[...]
    qd = qs * jnp.exp2(b)
    kd = (k * jnp.exp2(b_tot - b)).astype(BF16)
    o = _dot_nt(qd.astype(BF16), st.astype(BF16))
    st_ref[...] = st * jnp.exp2(b_tot) + _dot_tn(v, kd)
    if form == "chunk":
        return o, _dot_nt((qd * jnp.exp2(-b_tot)).astype(BF16), kd)

    edge16 = [SUB * i + (0 if rev else SUB - 1) for i in range(N_SUB)]
    e_ko = jnp.exp2(jnp.concatenate([_rows(b_ref, e, SUB) for e in edge16], axis=0) - b)
    k_off = k * e_ko
    zq = jnp.zeros((SUB, HD), F32)
    q_parts, k_parts = [], []
    for j, lo, hi in _off_ranges(rev):
        q_off = qs[lo:hi] * jnp.exp2(jnp.minimum(b[lo:hi] - _rows(b_ref, edge16[j], 1), 0.0))
        q_parts.append(jnp.concatenate([zq] * (lo // SUB) + [q_off] + [zq] * ((CHUNK - hi) // SUB), axis=0))
        k_parts.append(jnp.concatenate([zq] * j + [k_off[SUB * j:SUB * (j + 1)]] + [zq] * (N_SUB - 1 - j),
                                       axis=0))
    a_16 = _dot_nt(jnp.concatenate(q_parts, axis=1).astype(BF16),
                   jnp.concatenate(k_parts, axis=1).astype(BF16))

    if form == "block":
        return o, a_16, _dot_nt((qs * (1.0 / e_ko)).astype(BF16), k_off.astype(BF16))

    edge8 = [SUB * i + (SUB8 if rev else SUB8 - 1) for i in range(N_SUB)]
    d8 = b - jnp.concatenate([_rows(b_ref, e, SUB) for e in edge8], axis=0)
    second = (lax.broadcasted_iota(jnp.int32, (CHUNK, HD), 0) % SUB) >= SUB8
    q_side = jnp.logical_not(second) if rev else second
    w8 = jnp.exp2(jnp.minimum(jnp.where(q_side, d8, -d8), 0.0))
    a_8 = _dot_nt(jnp.where(q_side, qs * w8, 0.0).astype(BF16), jnp.where(q_side, 0.0, k * w8).astype(BF16))

    slabs = []
    for s in range(SUB8):
        rows = [SUB8 * i + s for i in range(CHUNK // SUB8)]
        b_s = jnp.concatenate([_rows(b_ref, r, SUB8) for r in rows], axis=0)
        k_s = jnp.concatenate([_rows(k_ref, r, SUB8) for r in rows], axis=0)
        slabs.append((qs * k_s * jnp.exp2(jnp.minimum(b - b_s, 0.0))).astype(BF16))
    a_dg = _dot(jnp.concatenate(slabs, axis=1), emat_ref[...])
    return o, a_16, a_8, a_dg


def _gla_output(scores, v, m_c_ref, m_16_ref, m_dg_ref, m_8_ref, form):
    if form == "chunk":
        o, a_c = scores
        a = jnp.where(m_c_ref[...] > 0.0, a_c, 0.0)
    elif form == "block":
        o, a_16, a_in = scores
        a = a_in * m_16_ref[...] + a_16
    else:
        o, a_16, a_8, a_dg = scores
        a = a_dg * m_dg_ref[...] + a_8 * m_8_ref[...] + a_16
    return o + _dot(a.astype(BF16), v)


def _decay_ranges_ok(g_ref, sel_ref, n_tiles):
    sums = [_dot(sel_ref[...], g_ref[i * ROW_TILE:(i + 1) * ROW_TILE, :].astype(BF16)) for i in range(n_tiles)]
    low = functools.reduce(jnp.minimum, sums)
    n16 = ROW_TILE // SUB
    return (jnp.min(low[n16:]) > -SAFE_LOG2_RANGE["chunk"], jnp.min(low[:n16]) > -SAFE_LOG2_RANGE["block"])


def _sigmoid(x):
    return 1.0 / (1.0 + jnp.exp(-x))


def _log2_sigmoid(x):
    return jnp.minimum(x, 0.0) * LOG2E - jnp.log2(1.0 + jnp.exp2(jnp.abs(x) * -LOG2E))


def _bidir_body(q_ref, kf_ref, kb_ref, g2f_ref, g2b_ref, v_ref, gate_ref, ng_ref,
                trif_ref, mcf_ref, m16f_ref, mdf_ref, m8f_ref, trib_ref, mcb_ref, m16b_ref, mdb_ref, m8b_ref,
                e_ref, sel_ref,
                o_ref, accf_ref, accb_ref, sf_ref, sb_ref, stage_ref, *, nc):
    def run(form):
        per = CHUNKS_PER_STEP[form]
        sf_ref[...] = jnp.zeros_like(sf_ref)
        sb_ref[...] = jnp.zeros_like(sb_ref)

        def step(c, carry):
            streams = []
            for u, rev in [(u, rev) for u in range(per) for rev in (False, True)]:
                cf = c * per + u
                sl = pl.ds(pl.multiple_of((nc - 1 - cf if rev else cf) * CHUNK, CHUNK), CHUNK)
                b_ref = stage_ref.at[4 * u + 2 * int(rev)]
                kst_ref = stage_ref.at[4 * u + 2 * int(rev) + 1]
                k = (kb_ref if rev else kf_ref)[sl, :].astype(F32)
                qs = q_ref[sl, :].astype(F32)
                b = _gla_cumsum(k, (g2b_ref if rev else g2f_ref)[sl, :], b_ref, kst_ref,
                                trib_ref if rev else trif_ref, form)
                streams.append((rev, sl, qs, k, b, b_ref, kst_ref))
            scores = [_gla_scores(qs, k, v_ref[sl, :], b, sb_ref if rev else sf_ref, b_ref, kst_ref, e_ref,
                                  rev, form)
                      for rev, sl, qs, k, b, b_ref, kst_ref in streams]
            for (rev, sl, *_), sc in zip(streams, scores):
                masks = (mcb_ref, m16b_ref, mdb_ref, m8b_ref) if rev else (mcf_ref, m16f_ref, mdf_ref, m8f_ref)
                (accb_ref if rev else accf_ref)[sl, :] = _gla_output(sc, v_ref[sl, :], *masks, form)
            return carry

        lax.fori_loop(0, nc // per, step, 0)

    n_tiles = (nc * CHUNK) // ROW_TILE
    chunk_f, block_f = _decay_ranges_ok(g2f_ref, sel_ref, n_tiles)
    chunk_b, block_b = _decay_ranges_ok(g2b_ref, sel_ref, n_tiles)
    chunk_ok = jnp.logical_and(chunk_f, chunk_b)
    block_ok = jnp.logical_and(jnp.logical_and(block_f, block_b), jnp.logical_not(chunk_ok))
    pl.when(chunk_ok)(functools.partial(run, "chunk"))
    pl.when(block_ok)(functools.partial(run, "block"))
    pl.when(jnp.logical_not(jnp.logical_or(chunk_ok, block_ok)))(functools.partial(run, "pair"))

    ng = ng_ref[...]

    def finish(i, carry):
        sl = pl.ds(pl.multiple_of(i * ROW_TILE, ROW_TILE), ROW_TILE)
        o = accf_ref[sl, :] + accb_ref[sl, :]
        gt = gate_ref[sl, :].astype(F32)
        o_ref[sl, :] = (_rms(o, ng) * (gt * _sigmoid(gt))).astype(o_ref.dtype)
        return carry

    lax.fori_loop(0, (nc * CHUNK) // ROW_TILE, finish, 0)


def _bidir_call(name, bsz, t, dv, args, in_specs):
    consts = (*_gla_constants(False), *_gla_constants(True), *_gla_shared_constants())
    const = lambda shape: pl.BlockSpec(shape, lambda b, h: (0,) * len(shape))
    return pl.pallas_call(
        functools.partial(_bidir_body, nc=t // CHUNK),
        grid=(bsz, HEADS),
        in_specs=in_specs + [const(c.shape) for c in consts],
        out_specs=pl.BlockSpec((None, t, dv), lambda b, h: (b, 0, h)),
        out_shape=jax.ShapeDtypeStruct((bsz, t, HEADS * dv), BF16),
        scratch_shapes=[pltpu.VMEM((t, dv), F32), pltpu.VMEM((t, dv), F32),
                        pltpu.VMEM((dv, HD), F32), pltpu.VMEM((dv, HD), F32)]
                       + [pltpu.VMEM((4 * max(CHUNKS_PER_STEP.values()), CHUNK, HD), F32)],
        compiler_params=_cparams("parallel", "parallel"),
        name=name,
    )(*args, *consts)


def _hgrn_mixer(a16, a32, norm_g):
    bsz, t, _ = a16.shape
    col = lambda off: pl.BlockSpec((None, t, HD), lambda b, h: (b, 0, off * HEADS + h))
    return _bidir_call("hgrn2_bidir", bsz, t, HD,
                       (a16, a16, a16, a32, a32, a16, a16, norm_g),
                       [col(0), col(3), col(4), col(0), col(1), col(1), col(2),
                        pl.BlockSpec((1, HD), lambda b, h: (0, h))])


def _gla_mixer(c16, g2, norm_g):
    bsz, t, _ = c16.shape
    col = lambda off: pl.BlockSpec((None, t, HD), lambda b, h: (b, 0, off * HEADS + h))
    wide = lambda off: pl.BlockSpec((None, t, C_HDV), lambda b, h: (b, 0, off + h))
    v_off = 2 * C_KD // C_HDV
    return _bidir_call("gla_bidir", bsz, t, C_HDV,
                       (c16, c16, c16, g2, g2, c16, c16, norm_g),
                       [col(0), col(1), col(1), col(0), col(1), wide(v_off), wide(v_off + HEADS),
                        pl.BlockSpec((1, C_HDV), lambda b, h: (0, h))])


def _attn_window(length, radius):
    win = min(ATT_Q + 2 * radius, length)
    return win, (0, -radius, ATT_Q - win)


def _attn_bias(gi, length):
    wsize, dil = B_GROUPS[gi]
    radius = wsize // (2 * dil)
    win, offsets = _attn_window(length, radius)
    n = N_GROUPS * HEADS
    slopes = np.array([2.0 ** (-ALIBI_MAX_EXP * (gi * HEADS + h + 1) / n) for h in range(HEADS)])
    dist = np.abs(np.array(offsets)[:, None, None] + np.arange(win)[None, None, :] - np.arange(ATT_Q)[None, :, None])
    bias = np.where(dist[:, None] <= radius, -slopes[None, :, None, None] * (dil * dist[:, None]), NEG_INF)
    return jnp.asarray(bias, F32)


def _attn_body(x_ref, bias_ref, o_ref, l_ref, *, length, radius, classes):
    win, _ = _attn_window(length, radius)
    n_tiles = length // ATT_Q
    per = max(n for n in (4, 2, 1) if n * classes <= ATT_CLASSES and n_tiles % n == 0)

    def step(i, carry):
        probs = []
        for u in range(per):
            tile = i * per + u
            m0 = pl.multiple_of(tile * ATT_Q, ATT_Q)
            k0 = pl.multiple_of(jnp.clip(m0 - radius, 0, length - win), radius)
            place = jnp.where(tile == 0, 0, jnp.where(tile == n_tiles - 1, 2, 1))
            probs += [(m0, k0, place, r, h) for r in range(classes) for h in range(HEADS)]
        col = lambda r, j, h: pl.ds((3 * r + j) * A_W + h * HD, HD)
        s = [_dot_nt(x_ref[pl.ds(m0, ATT_Q), col(r, 0, h)], x_ref[pl.ds(k0, win), col(r, 1, h)])
             for m0, k0, _, r, h in probs]
        s = [x + bias_ref[place, h] for x, (_, _, place, _, h) in zip(s, probs)]
        mx = [jnp.max(x, axis=-1, keepdims=True) for x in s]
        p = [jnp.exp(x - m) for x, m in zip(s, mx)]
        den = [jnp.sum(x, axis=-1, keepdims=True) for x in p]
        o = [_dot(x.astype(BF16), x_ref[pl.ds(k0, win), col(r, 2, h)]) for x, (_, k0, _, r, h) in zip(p, probs)]
        for x, d, m, (m0, _, _, r, h) in zip(o, den, mx, probs):
            o_ref[pl.ds(m0, ATT_Q), pl.ds(r * A_W + h * HD, HD)] = (x * (1.0 / d)).astype(o_ref.dtype)
            l_ref[pl.ds(m0, ATT_Q), pl.ds(r * LSE_W + h * LSE_REP, LSE_REP)] = jnp.broadcast_to(
                m + jnp.log(d), (ATT_Q, LSE_REP))
        return carry

    lax.fori_loop(0, n_tiles // per, step, 0)


def _dilated_attention(qkv, gi):
    bsz, length, _ = qkv.shape
    wsize, dil = B_GROUPS[gi]
    classes = min(dil, ATT_CLASSES)
    bias = _attn_bias(gi, length)
    return pl.pallas_call(
        functools.partial(_attn_body, length=length, radius=wsize // (2 * dil), classes=classes),
        grid=(bsz, dil // classes),
        in_specs=[pl.BlockSpec((None, length, classes * QKV_W), lambda b, r: (b, 0, r)),
                  pl.BlockSpec(bias.shape, lambda b, r: (0, 0, 0, 0))],
        out_specs=[pl.BlockSpec((None, length, classes * A_W), lambda b, r: (b, 0, r)),
                   pl.BlockSpec((None, length, classes * LSE_W), lambda b, r: (b, 0, r))],
        out_shape=[jax.ShapeDtypeStruct((bsz, length, dil * A_W), BF16),
                   jax.ShapeDtypeStruct((bsz, length, dil * LSE_W), F32)],
        compiler_params=_cparams("parallel", "parallel"),
        name=f"dilated_attn_g{gi}",
    )(qkv, bias)


def _lse_expansion():
    e = np.zeros((LSE_W, A_W), np.float32)
    for h in range(HEADS):
        e[h * LSE_REP, h * HD:(h + 1) * HD] = 1.0
    return jnp.asarray(e, BF16)


def _out_even_body(h_ref, a_ref, o0_ref, o1_ref, o2_ref, l0_ref, l1_ref, l2_ref, e_ref, w_ref, y_ref,
                   ob1_ref, ob2_ref, lb1_ref, lb2_ref):
    def natural(o_ref, l_ref, ob_ref, lb_ref, dil):
        if dil == 1:
            return o_ref[...].astype(F32), l_ref[...]
        n = WIDE_TILE // dil
        for r in range(dil):
            for c in range(HEADS):
                col = r * A_W + c * HD
                ob_ref[c, pl.ds(r, n, stride=dil), :] = o_ref[:, col:col + HD].astype(F32)
            lb_ref[pl.ds(r, n, stride=dil), :] = l_ref[:, r * LSE_W:(r + 1) * LSE_W]
        return jnp.concatenate([ob_ref[c] for c in range(HEADS)], axis=1), lb_ref[...]

    parts = [natural(o0_ref, l0_ref, None, None, B_GROUPS[0][1]),
             natural(o1_ref, l1_ref, ob1_ref, lb1_ref, B_GROUPS[1][1]),
             natural(o2_ref, l2_ref, ob2_ref, lb2_ref, B_GROUPS[2][1])]
    lses = [l for _, l in parts]
    mx = jnp.maximum(jnp.maximum(lses[0], lses[1]), lses[2])
    es = [jnp.exp(l - mx) for l in lses]
    inv = 1.0 / (es[0] + es[1] + es[2])
    emat = e_ref[...]
    mixed_b = None
    for e_g, (o_g, _) in zip(es, parts):
        alpha = (e_g * inv).astype(BF16)
        term = _dot(alpha, emat) * o_g
        mixed_b = term if mixed_b is None else mixed_b + term
    y_ref[...] = (h_ref[...] + _dot(a_ref[...], w_ref[:A_W, :]) + _dot(mixed_b.astype(BF16), w_ref[A_W:, :]))


def _out_even(h, out_a, os_, lses, w, layer):
    m = h.shape[0]
    emat = _lse_expansion()
    row = lambda n: pl.BlockSpec((WIDE_TILE, n), lambda i: (i, 0))
    grp = lambda width: [pl.BlockSpec((WIDE_TILE // dil, dil * width), lambda i: (i, 0)) for _, dil in B_GROUPS]
    return pl.pallas_call(
        _out_even_body,
        grid=(m // WIDE_TILE,),
        in_specs=[row(D_MODEL), row(A_W)] + grp(A_W) + grp(LSE_W) + [_resident(emat.shape), _resident_layer(w, layer)],
        out_specs=row(D_MODEL),
        out_shape=jax.ShapeDtypeStruct((m, D_MODEL), F32),
        scratch_shapes=[pltpu.VMEM((HEADS, WIDE_TILE, HD), F32), pltpu.VMEM((HEADS, WIDE_TILE, HD), F32),
                        pltpu.VMEM((WIDE_TILE, LSE_W), F32), pltpu.VMEM((WIDE_TILE, LSE_W), F32)],
        compiler_params=_cparams("parallel"),
        name="out_proj_even",
    )(h, out_a, *os_, *lses, emat, w)


def _out_odd_body(h_ref, a_ref, w_ref, y_ref):
    y_ref[...] = h_ref[...] + _dot(a_ref[...], w_ref[...])


def _out_odd(h, mixed, w, layer):
    m = h.shape[0]
    row = lambda n: pl.BlockSpec((WIDE_TILE, n), lambda i: (i, 0))
    return pl.pallas_call(
        _out_odd_body,
        grid=(m // WIDE_TILE,),
        in_specs=[row(D_MODEL), row(C_VD), _resident_layer(w, layer)],
        out_specs=row(D_MODEL),
        out_shape=jax.ShapeDtypeStruct((m, D_MODEL), F32),
        compiler_params=_cparams("parallel"),
        name="out_proj_odd",
    )(h, mixed, w)


def _ffn_body(hp_ref, h_ref, hn_ref, p_ref, gf_ref, wup_ref, cw_ref, cb_ref, wdn_ref, gp_ref, wg_ref, wp_ref,
              go_ref, y_ref, xe_ref, act_ref, *, tiles_per_seq, final):
    i = pl.program_id(0) % tiles_per_seq
    gf = gf_ref[...]
    h = h_ref[...]
    keep_prev = jnp.where(i > 0, 1.0, 0.0).astype(F32)
    keep_next = jnp.where(i < tiles_per_seq - 1, 1.0, 0.0).astype(F32)
    xe_ref[0:HALO, :] = _rms(hp_ref[...], gf) * keep_prev
    xe_ref[HALO:HALO + WIDE_TILE, :] = _rms(h, gf)
    xe_ref[HALO + WIDE_TILE:, :] = _rms(hn_ref[...], gf) * keep_next
    xe = xe_ref[...].astype(BF16)
    rows = WIDE_TILE + 2 * HALO

    def conv(u, c):
        w = cw_ref[:, c:c + FF_CHUNK]
        prev = pltpu.roll(u, 1, 0)[HALO:HALO + WIDE_TILE]
        nxt = pltpu.roll(u, rows - 1, 0)[HALO:HALO + WIDE_TILE]
        return (prev * w[0:1] + u[HALO:HALO + WIDE_TILE] * w[1:2] + nxt * w[2:3] + cb_ref[:, c:c + FF_CHUNK])

    for c in range(0, D_FF, FF_CHUNK):
        a = conv(_dot(xe, wup_ref[:, c:c + FF_CHUNK]), c)
        gt = conv(_dot(xe, wup_ref[:, D_FF + c:D_FF + c + FF_CHUNK]), D_FF + c)
        act_ref[:, c:c + FF_CHUNK] = (a * (0.5 * gt * (1.0 + lax.erf(gt * (2.0 ** -0.5))))).astype(BF16)
    h2 = h + _dot(act_ref[...], wdn_ref[...])
    sig = _sigmoid(_dot(_rms(h2, gp_ref[...]).astype(BF16), wg_ref[...]))
    h3 = h2 + sig * _dot(p_ref[...].astype(BF16), wp_ref[...])
    y_ref[...] = _rms(h3, go_ref[...]) if final else h3


def _ffn_ple(h, p, t, gf, wup, cw, cb, wdn, gp, wg, wp, go, layer, final):
    m = h.shape[0]
    per_tile = WIDE_TILE // HALO
    last_blk = m // HALO - 1
    vec = lambda n: pl.BlockSpec((1, n), lambda i: (0, 0))
    return pl.pallas_call(
        functools.partial(_ffn_body, tiles_per_seq=t // WIDE_TILE, final=final),
        grid=(m // WIDE_TILE,),
        in_specs=[pl.BlockSpec((HALO, D_MODEL), lambda i: (jnp.maximum(i * per_tile - 1, 0), 0)),
                  pl.BlockSpec((WIDE_TILE, D_MODEL), lambda i: (i, 0)),
                  pl.BlockSpec((HALO, D_MODEL), lambda i: (jnp.minimum((i + 1) * per_tile, last_blk), 0)),
                  pl.BlockSpec((None, WIDE_TILE, PLE_DIM), lambda i: (layer, i, 0)),
                  vec(D_MODEL), _resident_layer(wup, layer), _resident_layer(cw, layer), vec(2 * D_FF),
                  _resident_layer(wdn, layer), vec(D_MODEL), _resident_layer(wg, layer), _resident_layer(wp, layer),
                  vec(D_MODEL)],
        out_specs=pl.BlockSpec((WIDE_TILE, D_MODEL), lambda i: (i, 0)),
        out_shape=jax.ShapeDtypeStruct((m, D_MODEL), F32),
        scratch_shapes=[pltpu.VMEM((WIDE_TILE + 2 * HALO, D_MODEL), F32), pltpu.VMEM((WIDE_TILE, D_FF), BF16)],
        compiler_params=pltpu.CompilerParams(dimension_semantics=("parallel",), vmem_limit_bytes=VMEM_LIMIT_FFN),
        name="conv_ffn_ple",
    )(h, h, h, p, gf, wup, cw, cb, wdn, gp, wg, wp, go)


def _prep_weights(ev_w_in, od_w_in, gla_w_gate_up, gla_b_gate):
    ev_w = ev_w_in.astype(BF16)
    od_w = jnp.pad(od_w_in, ((0, 0), (0, 0), (0, LR_PAD - 2 * C_RANK))).astype(BF16)
    n_odd = od_w_in.shape[0]
    w2 = jnp.zeros((n_odd, LR_PAD, 2 * C_KD), F32)
    w2 = w2.at[:, :C_RANK, :C_KD].set(gla_w_gate_up[:, 0]).at[:, C_RANK:2 * C_RANK, C_KD:].set(gla_w_gate_up[:, 1])
    b2 = gla_b_gate.reshape(n_odd, 1, 2 * C_KD)
    return ev_w, od_w, w2.astype(BF16), b2


def _trunk(x, p, prm):
    bsz, t, _ = x.shape
    m = bsz * t
    h = x.reshape(m, D_MODEL)
    for l in range(DEPTH):
        g_mix = prm["norm_mix_g"][l][None]
        if l % 2 == 0:
            e = l // 2
            lb = jnp.concatenate([prm["lb"][0, e], prm["lb"][1, e]])[None]
            a16, a32, *qkv = _in_proj_even(h, g_mix, lb, prm["ev_w"], e)
            out_a = _hgrn_mixer(a16.reshape(bsz, t, -1), a32.reshape(bsz, t, -1), prm["hgrn_norm_g"][e][None])
            att = [_dilated_attention(x_g.reshape(bsz, t // dil, dil * QKV_W), gi)
                   for gi, (x_g, (_, dil)) in enumerate(zip(qkv, B_GROUPS))]
            h = _out_even(h, out_a.reshape(m, A_W),
                          [o.reshape(m // dil, dil * A_W) for (o, _), (_, dil) in zip(att, B_GROUPS)],
                          [l_.reshape(m // dil, dil * LSE_W) for (_, l_), (_, dil) in zip(att, B_GROUPS)],
                          prm["ev_w_out"], e)
        else:
            o = l // 2
            c16, lg = _in_proj_odd(h, g_mix, prm["od_w"], prm["od_w2"], prm["od_b2"][o], o)
            mixed = _gla_mixer(c16.reshape(bsz, t, -1), lg.reshape(bsz, t, -1), prm["gla_norm_g"][o][None])
            h = _out_odd(h, mixed.reshape(m, C_VD), prm["od_w_out"], o)
        h = _ffn_ple(h, p.reshape(DEPTH, m, PLE_DIM), t, prm["norm_ffn_g"][l][None], prm["ffn_w_up"],
                     prm["ffn_conv_w"], prm["ffn_conv_b"][l][None], prm["ffn_w_down"],
                     prm["norm_ple_g"][l][None], prm["ple_w_gate"], prm["ple_w_proj"],
                     prm["norm_out_g"][None], layer=l, final=(l == DEPTH - 1))
    return h.reshape(bsz, t, D_MODEL)


def kernel(x_prompt, x_sample, p_prompt, p_sample, norm_mix_g, ev_w_in, hgrn_lb_logits, hgrn_norm_g, ev_w_out, od_w_in, gla_w_gate_up, gla_b_gate, gla_norm_g, od_w_out, norm_ffn_g, ffn_w_up, ffn_conv_w, ffn_conv_b, ffn_w_down, norm_ple_g, ple_w_gate, ple_w_proj, norm_out_g):
    lb = jnp.cumsum(jax.nn.softmax(hgrn_lb_logits.astype(F32), axis=1), axis=1)
    lb = lb - lb[:, :1]
    ev_w, od_w, od_w2, od_b2 = _prep_weights(ev_w_in, od_w_in, gla_w_gate_up, gla_b_gate)
    prm = dict(norm_mix_g=norm_mix_g, ev_w=ev_w, lb=lb, hgrn_norm_g=hgrn_norm_g, ev_w_out=ev_w_out.astype(BF16),
               od_w=od_w, od_w2=od_w2, od_b2=od_b2, gla_norm_g=gla_norm_g, od_w_out=od_w_out.astype(BF16),
               norm_ffn_g=norm_ffn_g, ffn_w_up=ffn_w_up.astype(BF16), ffn_conv_w=ffn_conv_w,
               ffn_conv_b=ffn_conv_b, ffn_w_down=ffn_w_down.astype(BF16), norm_ple_g=norm_ple_g,
               ple_w_gate=ple_w_gate.astype(BF16), ple_w_proj=ple_w_proj.astype(BF16), norm_out_g=norm_out_g)
    return _trunk(x_prompt, p_prompt, prm), _trunk(x_sample, p_sample, prm)
```

```python
import functools

import numpy as np
import jax
import jax.numpy as jnp
from jax import lax
from jax.experimental import pallas as pl
from jax.experimental.pallas import tpu as pltpu

F32 = jnp.float32
BF16 = jnp.bfloat16

D_MODEL = 1024
DEPTH = 4
PLE_DIM = 256
EPS = 1e-6
HEADS = 4
HD = 128
A_W = HEADS * HD
B_GROUPS = ((128, 1), (512, 4), (2048, 16))
N_GROUPS = len(B_GROUPS)
QKV_W = 3 * A_W
ATT_Q = 128
ATT_CLASSES = 4
ALIBI_MAX_EXP = 8.0
C_KD = 512
C_VD = 1024
C_HDV = C_VD // HEADS
C_RANK = 16
GATE_NORMALIZER = 16.0
CHUNK = 64
SUB = 16
N_SUB = CHUNK // SUB
SUB8 = 8
CHUNKS_PER_STEP = {"chunk": 16, "block": 8, "pair": 4}
LOG_DECAY_MIN = -30.0
LOG2E = 1.4426950408889634
SAFE_LOG2_RANGE = {"chunk": 116.0, "block": 96.0}
NEG_INF = -1e30
D_FF = 2816
LSE_W = 128
LSE_REP = LSE_W // HEADS
LR_PAD = 128

ROW_TILE = 512
WIDE_TILE = 1024
HALO = 8
FF_CHUNK = 256
VMEM_LIMIT = 56 * 1024 * 1024
VMEM_LIMIT_FFN = 60 * 1024 * 1024


def _cparams(*sem):
    return pltpu.CompilerParams(dimension_semantics=sem, vmem_limit_bytes=VMEM_LIMIT)


def _resident(shape):
    nd = len(shape)
    return pl.BlockSpec(shape, lambda *_: (0,) * nd, pipeline_mode=pl.Buffered(1))


def _resident_layer(stacked, layer):
    nd = stacked.ndim
    return pl.BlockSpec((None,) + stacked.shape[1:], lambda *_: (layer,) + (0,) * (nd - 1),
                        pipeline_mode=pl.Buffered(1))


def _rms(x, g):
    ms = jnp.mean(x * x, axis=-1, keepdims=True)
    return x * lax.rsqrt(ms + EPS) * g


def _dot(a, b):
    return jnp.dot(a, b, preferred_element_type=F32)


def _dot_nt(a, b):
    return lax.dot_general(a, b, (((1,), (1,)), ((), ())), preferred_element_type=F32)


def _dot_tn(a, b):
    return lax.dot_general(a, b, (((0,), (0,)), ((), ())), preferred_element_type=F32)


def _store_cols(xn, w_ref, out_refs, c0=0, col_chunk=512):
    for o_ref in out_refs:
        n = o_ref.shape[-1]
        for j in range(0, n, col_chunk):
            wj = min(col_chunk, n - j)
            o_ref[:, j:j + wj] = _dot(xn, w_ref[:, c0 + j:c0 + j + wj]).astype(o_ref.dtype)
        c0 += n
    return c0


def _clamp_log2_decay(log2_f):
    return jnp.maximum(log2_f, LOG_DECAY_MIN * LOG2E)


def _in_proj_even_body(x_ref, g_ref, lb_ref, w_ref, a16_ref, a32_ref, g0_ref, g1_ref, g2_ref, u_ref):
    xf = _rms(x_ref[...], g_ref[...])
    xn = xf.astype(BF16)
    lane_tiles = D_MODEL // HD
    for c in range(lane_tiles):
        u_ref[c] = xf[:, c * HD:(c + 1) * HD]

    def class_major(dil):
        n = ROW_TILE // dil
        rows = [jnp.concatenate([u_ref[c, pl.ds(r, n, stride=dil), :] for c in range(lane_tiles)], axis=1)
                for r in range(dil)]
        return jnp.concatenate(rows, axis=0).astype(BF16)

    _store_cols(xn, w_ref, (a16_ref.at[:, :A_W],))
    _store_cols(xn, w_ref, (a16_ref.at[:, A_W:3 * A_W],), c0=3 * A_W)
    c0 = 5 * A_W
    for d in range(2):
        lb = lb_ref[:, d * A_W:(d + 1) * A_W]
        f = lb + (1.0 - lb) * _sigmoid(_dot(xn, w_ref[:, (1 + d) * A_W:(2 + d) * A_W]))
        a32_ref[:, d * A_W:(d + 1) * A_W] = _clamp_log2_decay(jnp.log2(f))
        a16_ref[:, (3 + d) * A_W:(4 + d) * A_W] = (1.0 - f).astype(BF16)
    for (_, dil), o_ref in zip(B_GROUPS, (g0_ref, g1_ref, g2_ref)):
        n = ROW_TILE // dil
        lhs = xn if dil == 1 else class_major(dil)
        for j in range(3):
            u = _dot(lhs, w_ref[:, c0:c0 + A_W])
            c0 += A_W
            if j == 0:
                u = u * (HD ** -0.5)
            for r in range(dil):
                o_ref[:, (3 * r + j) * A_W:(3 * r + j + 1) * A_W] = u[r * n:(r + 1) * n].astype(BF16)


def _in_proj_odd_body(x_ref, g_ref, w_ref, w2_ref, b2_ref, c16_ref, g2_ref):
    xn = _rms(x_ref[...], g_ref[...]).astype(BF16)
    c16_ref[:, :C_KD] = (_dot(xn, w_ref[:, :C_KD]) * (HD ** -0.5)).astype(BF16)
    c0 = _store_cols(xn, w_ref, (c16_ref.at[:, C_KD:],), c0=C_KD)
    lr = _dot(xn, w_ref[:, c0:c0 + LR_PAD]).astype(BF16)
    n = g2_ref.shape[-1]
    for j in range(0, n, 512):
        logit = _dot(lr, w2_ref[:, j:j + 512]) + b2_ref[:, j:j + 512]
        g2_ref[:, j:j + 512] = _clamp_log2_decay(_log2_sigmoid(logit) * (1.0 / GATE_NORMALIZER))


def _in_proj_even(x, g, lb, w, layer):
    m = x.shape[0]
    outs = [((m, 5 * A_W), (ROW_TILE, 5 * A_W), BF16), ((m, 2 * A_W), (ROW_TILE, 2 * A_W), F32)]
    for _, dil in B_GROUPS:
        outs.append(((m // dil, dil * QKV_W), (ROW_TILE // dil, dil * QKV_W), BF16))
    return pl.pallas_call(
        _in_proj_even_body,
        grid=(m // ROW_TILE,),
        in_specs=[pl.BlockSpec((ROW_TILE, D_MODEL), lambda i: (i, 0)),
                  pl.BlockSpec((1, D_MODEL), lambda i: (0, 0)),
                  pl.BlockSpec((1, 2 * A_W), lambda i: (0, 0)),
                  _resident_layer(w, layer)],
        out_specs=[pl.BlockSpec(blk, lambda i: (i, 0)) for _, blk, _ in outs],
        out_shape=[jax.ShapeDtypeStruct(shp, dt) for shp, _, dt in outs],
        scratch_shapes=[pltpu.VMEM((D_MODEL // HD, ROW_TILE, HD), F32)],
        compiler_params=_cparams("parallel"),
        name="in_proj_even",
    )(x, g, lb, w)


def _in_proj_odd(x, g, w, w2, b2, layer):
    m = x.shape[0]
    n16 = 2 * C_KD + 2 * C_VD
    return pl.pallas_call(
        _in_proj_odd_body,
        grid=(m // WIDE_TILE,),
        in_specs=[pl.BlockSpec((WIDE_TILE, D_MODEL), lambda i: (i, 0)),
                  pl.BlockSpec((1, D_MODEL), lambda i: (0, 0)),
                  _resident_layer(w, layer), _resident_layer(w2, layer),
                  pl.BlockSpec((1, 2 * C_KD), lambda i: (0, 0))],
        out_specs=[pl.BlockSpec((WIDE_TILE, n16), lambda i: (i, 0)),
                   pl.BlockSpec((WIDE_TILE, 2 * C_KD), lambda i: (i, 0))],
        out_shape=[jax.ShapeDtypeStruct((m, n16), BF16), jax.ShapeDtypeStruct((m, 2 * C_KD), F32)],
        compiler_params=_cparams("parallel"),
        name="in_proj_odd",
    )(x, g, w, w2, b2)


def _off_ranges(rev):
    out = []
    for j in range(N_SUB):
        lo, hi = (0, SUB * j) if rev else (SUB * (j + 1), CHUNK)
        if hi > lo:
            out.append((j, lo, hi))
    return out


def _gla_constants(rev):
    t = np.arange(CHUNK)
    tri = (t[None, :] >= t[:, None]) if rev else (t[None, :] <= t[:, None])
    causal = tri
    same8 = (t[None, :] // SUB8) == (t[:, None] // SUB8)
    same16 = (t[None, :] // SUB) == (t[:, None] // SUB)
    f = lambda a: jnp.asarray(a.astype(np.float32))
    return (jnp.asarray(np.concatenate([tri, tri], axis=1).astype(np.float32), BF16),
            f(causal), f(same16 & causal), f(same8 & causal), f(same16 & ~same8 & causal))


def _gla_shared_constants():
    e = np.zeros((SUB8 * HD, CHUNK), np.float32)
    for s in range(SUB8):
        e[s * HD:(s + 1) * HD, s::SUB8] = 1.0
    r = np.arange(ROW_TILE)
    sel = [(r[None, :] // n == np.arange(ROW_TILE // n)[:, None]).astype(np.float32) for n in (SUB, CHUNK)]
    return jnp.asarray(e, BF16), jnp.asarray(np.concatenate(sel, axis=0), BF16)


def _rows(ref, r, n):
    return jnp.broadcast_to(ref[pl.ds(r, 1), :], (n, HD))


def _gla_cumsum(k, g2, b_ref, k_ref, tri_ref, form):
    g_hi = g2.astype(BF16)
    g_lo = (g2 - g_hi.astype(F32)).astype(BF16)
    b = _dot(tri_ref[...], jnp.concatenate([g_hi, g_lo], axis=0))
    b_ref[...] = b
    if form == "pair":
        k_ref[...] = k
    return b


def _gla_scores(qs, k, v, b, st_ref, b_ref, k_ref, emat_ref, rev, form):
    b_tot = b_ref[pl.ds(0 if rev else CHUNK - 1, 1), :]

    st = st_ref[...]
    qd = qs * jnp.exp2(b)
    kd = (k * jnp.exp2(b_tot - b)).astype(BF16)
    o = _dot_nt(qd.astype(BF16), st.astype(BF16))
    st_ref[...] = st * jnp.exp2(b_tot) + _dot_tn(v, kd)
    if form == "chunk":
        return o, _dot_nt((qd * jnp.exp2(-b_tot)).astype(BF16), kd)

    edge16 = [SUB * i + (0 if rev else SUB - 1) for i in range(N_SUB)]
    e_ko = jnp.exp2(jnp.concatenate([_rows(b_ref, e, SUB) for e in edge16], axis=0) - b)
    k_off = k * e_ko
    zq = jnp.zeros((SUB, HD), F32)
    q_parts, k_parts = [], []
    for j, lo, hi in _off_ranges(rev):
        q_off = qs[lo:hi] * jnp.exp2(jnp.minimum(b[lo:hi] - _rows(b_ref, edge16[j], 1), 0.0))
        q_parts.append(jnp.concatenate([zq] * (lo // SUB) + [q_off] + [zq] * ((CHUNK - hi) // SUB), axis=0))
        k_parts.append(jnp.concatenate([zq] * j + [k_off[SUB * j:SUB * (j + 1)]] + [zq] * (N_SUB - 1 - j),
                                       axis=0))
    a_16 = _dot_nt(jnp.concatenate(q_parts, axis=1).astype(BF16),
                   jnp.concatenate(k_parts, axis=1).astype(BF16))

    if form == "block":
        return o, a_16, _dot_nt((qs * (1.0 / e_ko)).astype(BF16), k_off.astype(BF16))

    edge8 = [SUB * i + (SUB8 if rev else SUB8 - 1) for i in range(N_SUB)]
    d8 = b - jnp.concatenate([_rows(b_ref, e, SUB) for e in edge8], axis=0)
    second = (lax.broadcasted_iota(jnp.int32, (CHUNK, HD), 0) % SUB) >= SUB8
    q_side = jnp.logical_not(second) if rev else second
    w8 = jnp.exp2(jnp.minimum(jnp.where(q_side, d8, -d8), 0.0))
    a_8 = _dot_nt(jnp.where(q_side, qs * w8, 0.0).astype(BF16), jnp.where(q_side, 0.0, k * w8).astype(BF16))

    slabs = []
    for s in range(SUB8):
        rows = [SUB8 * i + s for i in range(CHUNK // SUB8)]
        b_s = jnp.concatenate([_rows(b_ref, r, SUB8) for r in rows], axis=0)
        k_s = jnp.concatenate([_rows(k_ref, r, SUB8) for r in rows], axis=0)
        slabs.append((qs * k_s * jnp.exp2(jnp.minimum(b - b_s, 0.0))).astype(BF16))
    a_dg = _dot(jnp.concatenate(slabs, axis=1), emat_ref[...])
    return o, a_16, a_8, a_dg


def _gla_output(scores, v, m_c_ref, m_16_ref, m_dg_ref, m_8_ref, form):
    if form == "chunk":
        o, a_c = scores
        a = jnp.where(m_c_ref[...] > 0.0, a_c, 0.0)
    elif form == "block":
        o, a_16, a_in = scores
        a = a_in * m_16_ref[...] + a_16
    else:
        o, a_16, a_8, a_dg = scores
        a = a_dg * m_dg_ref[...] + a_8 * m_8_ref[...] + a_16
    return o + _dot(a.astype(BF16), v)


def _decay_ranges_ok(g_ref, sel_ref, n_tiles):
    sums = [_dot(sel_ref[...], g_ref[i * ROW_TILE:(i + 1) * ROW_TILE, :].astype(BF16)) for i in range(n_tiles)]
    low = functools.reduce(jnp.minimum, sums)
    n16 = ROW_TILE // SUB
    return (jnp.min(low[n16:]) > -SAFE_LOG2_RANGE["chunk"], jnp.min(low[:n16]) > -SAFE_LOG2_RANGE["block"])


def _sigmoid(x):
    return 1.0 / (1.0 + jnp.exp(-x))


def _log2_sigmoid(x):
    return jnp.minimum(x, 0.0) * LOG2E - jnp.log2(1.0 + jnp.exp2(jnp.abs(x) * -LOG2E))


def _bidir_body(q_ref, kf_ref, kb_ref, g2f_ref, g2b_ref, v_ref, gate_ref, ng_ref,
                trif_ref, mcf_ref, m16f_ref, mdf_ref, m8f_ref, trib_ref, mcb_ref, m16b_ref, mdb_ref, m8b_ref,
                e_ref, sel_ref,
                o_ref, accf_ref, accb_ref, sf_ref, sb_ref, stage_ref, *, nc):
    def run(form):
        per = CHUNKS_PER_STEP[form]
        sf_ref[...] = jnp.zeros_like(sf_ref)
        sb_ref[...] = jnp.zeros_like(sb_ref)

        def step(c, carry):
            streams = []
            for u, rev in [(u, rev) for u in range(per) for rev in (False, True)]:
                cf = c * per + u
                sl = pl.ds(pl.multiple_of((nc - 1 - cf if rev else cf) * CHUNK, CHUNK), CHUNK)
                b_ref = stage_ref.at[4 * u + 2 * int(rev)]
                kst_ref = stage_ref.at[4 * u + 2 * int(rev) + 1]
                k = (kb_ref if rev else kf_ref)[sl, :].astype(F32)
                qs = q_ref[sl, :].astype(F32)
                b = _gla_cumsum(k, (g2b_ref if rev else g2f_ref)[sl, :], b_ref, kst_ref,
                                trib_ref if rev else trif_ref, form)
                streams.append((rev, sl, qs, k, b, b_ref, kst_ref))
            scores = [_gla_scores(qs, k, v_ref[sl, :], b, sb_ref if rev else sf_ref, b_ref, kst_ref, e_ref,
                                  rev, form)
                      for rev, sl, qs, k, b, b_ref, kst_ref in streams]
            for (rev, sl, *_), sc in zip(streams, scores):
                masks = (mcb_ref, m16b_ref, mdb_ref, m8b_ref) if rev else (mcf_ref, m16f_ref, mdf_ref, m8f_ref)
                (accb_ref if rev else accf_ref)[sl, :] = _gla_output(sc, v_ref[sl, :], *masks, form)
            return carry

        lax.fori_loop(0, nc // per, step, 0)

    n_tiles = (nc * CHUNK) // ROW_TILE
    chunk_f, block_f = _decay_ranges_ok(g2f_ref, sel_ref, n_tiles)
    chunk_b, block_b = _decay_ranges_ok(g2b_ref, sel_ref, n_tiles)
    chunk_ok = jnp.logical_and(chunk_f, chunk_b)
    block_ok = jnp.logical_and(jnp.logical_and(block_f, block_b), jnp.logical_not(chunk_ok))
    pl.when(chunk_ok)(functools.partial(run, "chunk"))
    pl.when(block_ok)(functools.partial(run, "block"))
    pl.when(jnp.logical_not(jnp.logical_or(chunk_ok, block_ok)))(functools.partial(run, "pair"))

    ng = ng_ref[...]

    def finish(i, carry):
        sl = pl.ds(pl.multiple_of(i * ROW_TILE, ROW_TILE), ROW_TILE)
        o = accf_ref[sl, :] + accb_ref[sl, :]
        gt = gate_ref[sl, :].astype(F32)
        o_ref[sl, :] = (_rms(o, ng) * (gt * _sigmoid(gt))).astype(o_ref.dtype)
        return carry

    lax.fori_loop(0, (nc * CHUNK) // ROW_TILE, finish, 0)


def _bidir_call(name, bsz, t, dv, args, in_specs):
    consts = (*_gla_constants(False), *_gla_constants(True), *_gla_shared_constants())
    const = lambda shape: pl.BlockSpec(shape, lambda b, h: (0,) * len(shape))
    return pl.pallas_call(
        functools.partial(_bidir_body, nc=t // CHUNK),
        grid=(bsz, HEADS),
        in_specs=in_specs + [const(c.shape) for c in consts],
        out_specs=pl.BlockSpec((None, t, dv), lambda b, h: (b, 0, h)),
        out_shape=jax.ShapeDtypeStruct((bsz, t, HEADS * dv), BF16),
        scratch_shapes=[pltpu.VMEM((t, dv), F32), pltpu.VMEM((t, dv), F32),
                        pltpu.VMEM((dv, HD), F32), pltpu.VMEM((dv, HD), F32)]
                       + [pltpu.VMEM((4 * max(CHUNKS_PER_STEP.values()), CHUNK, HD), F32)],
        compiler_params=_cparams("parallel", "parallel"),
        name=name,
    )(*args, *consts)


def _hgrn_mixer(a16, a32, norm_g):
    bsz, t, _ = a16.shape
    col = lambda off: pl.BlockSpec((None, t, HD), lambda b, h: (b, 0, off * HEADS + h))
    return _bidir_call("hgrn2_bidir", bsz, t, HD,
                       (a16, a16, a16, a32, a32, a16, a16, norm_g),
                       [col(0), col(3), col(4), col(0), col(1), col(1), col(2),
                        pl.BlockSpec((1, HD), lambda b, h: (0, h))])


def _gla_mixer(c16, g2, norm_g):
    bsz, t, _ = c16.shape
    col = lambda off: pl.BlockSpec((None, t, HD), lambda b, h: (b, 0, off * HEADS + h))
    wide = lambda off: pl.BlockSpec((None, t, C_HDV), lambda b, h: (b, 0, off + h))
    v_off = 2 * C_KD // C_HDV
    return _bidir_call("gla_bidir", bsz, t, C_HDV,
                       (c16, c16, c16, g2, g2, c16, c16, norm_g),
                       [col(0), col(1), col(1), col(0), col(1), wide(v_off), wide(v_off + HEADS),
                        pl.BlockSpec((1, C_HDV), lambda b, h: (0, h))])


def _attn_window(length, radius):
    win = min(ATT_Q + 2 * radius, length)
    return win, (0, -radius, ATT_Q - win)


def _attn_bias(gi, length):
    wsize, dil = B_GROUPS[gi]
    radius = wsize // (2 * dil)
    win, offsets = _attn_window(length, radius)
    n = N_GROUPS * HEADS
    slopes = np.array([2.0 ** (-ALIBI_MAX_EXP * (gi * HEADS + h + 1) / n) for h in range(HEADS)])
    dist = np.abs(np.array(offsets)[:, None, None] + np.arange(win)[None, None, :] - np.arange(ATT_Q)[None, :, None])
    bias = np.where(dist[:, None] <= radius, -slopes[None, :, None, None] * (dil * dist[:, None]), NEG_INF)
    return jnp.asarray(bias, F32)


def _attn_body(x_ref, bias_ref, o_ref, l_ref, *, length, radius, classes):
    win, _ = _attn_window(length, radius)
    n_tiles = length // ATT_Q
    per = max(n for n in (4, 2, 1) if n * classes <= ATT_CLASSES and n_tiles % n == 0)

    def step(i, carry):
        probs = []
        for u in range(per):
            tile = i * per + u
            m0 = pl.multiple_of(tile * ATT_Q, ATT_Q)
            k0 = pl.multiple_of(jnp.clip(m0 - radius, 0, length - win), radius)
            place = jnp.where(tile == 0, 0, jnp.where(tile == n_tiles - 1, 2, 1))
            probs += [(m0, k0, place, r, h) for r in range(classes) for h in range(HEADS)]
        col = lambda r, j, h: pl.ds((3 * r + j) * A_W + h * HD, HD)
        s = [_dot_nt(x_ref[pl.ds(m0, ATT_Q), col(r, 0, h)], x_ref[pl.ds(k0, win), col(r, 1, h)])
             for m0, k0, _, r, h in probs]
        s = [x + bias_ref[place, h] for x, (_, _, place, _, h) in zip(s, probs)]
        mx = [jnp.max(x, axis=-1, keepdims=True) for x in s]
        p = [jnp.exp(x - m) for x, m in zip(s, mx)]
        den = [jnp.sum(x, axis=-1, keepdims=True) for x in p]
        o = [_dot(x.astype(BF16), x_ref[pl.ds(k0, win), col(r, 2, h)]) for x, (_, k0, _, r, h) in zip(p, probs)]
        for x, d, m, (m0, _, _, r, h) in zip(o, den, mx, probs):
            o_ref[pl.ds(m0, ATT_Q), pl.ds(r * A_W + h * HD, HD)] = (x * (1.0 / d)).astype(o_ref.dtype)
            l_ref[pl.ds(m0, ATT_Q), pl.ds(r * LSE_W + h * LSE_REP, LSE_REP)] = jnp.broadcast_to(
                m + jnp.log(d), (ATT_Q, LSE_REP))
        return carry

    lax.fori_loop(0, n_tiles // per, step, 0)


def _dilated_attention(qkv, gi):
    bsz, length, _ = qkv.shape
    wsize, dil = B_GROUPS[gi]
    classes = min(dil, ATT_CLASSES)
    bias = _attn_bias(gi, length)
    return pl.pallas_call(
        functools.partial(_attn_body, length=length, radius=wsize // (2 * dil), classes=classes),
        grid=(bsz, dil // classes),
        in_specs=[pl.BlockSpec((None, length, classes * QKV_W), lambda b, r: (b, 0, r)),
                  pl.BlockSpec(bias.shape, lambda b, r: (0, 0, 0, 0))],
        out_specs=[pl.BlockSpec((None, length, classes * A_W), lambda b, r: (b, 0, r)),
                   pl.BlockSpec((None, length, classes * LSE_W), lambda b, r: (b, 0, r))],
        out_shape=[jax.ShapeDtypeStruct((bsz, length, dil * A_W), BF16),
                   jax.ShapeDtypeStruct((bsz, length, dil * LSE_W), F32)],
        compiler_params=_cparams("parallel", "parallel"),
        name=f"dilated_attn_g{gi}",
    )(qkv, bias)


def _lse_expansion():
    e = np.zeros((LSE_W, A_W), np.float32)
    for h in range(HEADS):
        e[h * LSE_REP, h * HD:(h + 1) * HD] = 1.0
    return jnp.asarray(e, BF16)


def _out_even_body(h_ref, a_ref, o0_ref, o1_ref, o2_ref, l0_ref, l1_ref, l2_ref, e_ref, w_ref, y_ref,
                   ob1_ref, ob2_ref, lb1_ref, lb2_ref):
    def natural(o_ref, l_ref, ob_ref, lb_ref, dil):
        if dil == 1:
            return o_ref[...].astype(F32), l_ref[...]
        n = WIDE_TILE // dil
        for r in range(dil):
            for c in range(HEADS):
                col = r * A_W + c * HD
                ob_ref[c, pl.ds(r, n, stride=dil), :] = o_ref[:, col:col + HD].astype(F32)
            lb_ref[pl.ds(r, n, stride=dil), :] = l_ref[:, r * LSE_W:(r + 1) * LSE_W]
        return jnp.concatenate([ob_ref[c] for c in range(HEADS)], axis=1), lb_ref[...]

    parts = [natural(o0_ref, l0_ref, None, None, B_GROUPS[0][1]),
             natural(o1_ref, l1_ref, ob1_ref, lb1_ref, B_GROUPS[1][1]),
             natural(o2_ref, l2_ref, ob2_ref, lb2_ref, B_GROUPS[2][1])]
    lses = [l for _, l in parts]
    mx = jnp.maximum(jnp.maximum(lses[0], lses[1]), lses[2])
    es = [jnp.exp(l - mx) for l in lses]
    inv = 1.0 / (es[0] + es[1] + es[2])
    emat = e_ref[...]
    mixed_b = None
    for e_g, (o_g, _) in zip(es, parts):
        alpha = (e_g * inv).astype(BF16)
        term = _dot(alpha, emat) * o_g
        mixed_b = term if mixed_b is None else mixed_b + term
    y_ref[...] = (h_ref[...] + _dot(a_ref[...], w_ref[:A_W, :]) + _dot(mixed_b.astype(BF16), w_ref[A_W:, :]))


def _out_even(h, out_a, os_, lses, w, layer):
    m = h.shape[0]
    emat = _lse_expansion()
    row = lambda n: pl.BlockSpec((WIDE_TILE, n), lambda i: (i, 0))
    grp = lambda width: [pl.BlockSpec((WIDE_TILE // dil, dil * width), lambda i: (i, 0)) for _, dil in B_GROUPS]
    return pl.pallas_call(
        _out_even_body,
        grid=(m // WIDE_TILE,),
        in_specs=[row(D_MODEL), row(A_W)] + grp(A_W) + grp(LSE_W) + [_resident(emat.shape), _resident_layer(w, layer)],
        out_specs=row(D_MODEL),
        out_shape=jax.ShapeDtypeStruct((m, D_MODEL), F32),
        scratch_shapes=[pltpu.VMEM((HEADS, WIDE_TILE, HD), F32), pltpu.VMEM((HEADS, WIDE_TILE, HD), F32),
                        pltpu.VMEM((WIDE_TILE, LSE_W), F32), pltpu.VMEM((WIDE_TILE, LSE_W), F32)],
        compiler_params=_cparams("parallel"),
        name="out_proj_even",
    )(h, out_a, *os_, *lses, emat, w)


def _out_odd_body(h_ref, a_ref, w_ref, y_ref):
    y_ref[...] = h_ref[...] + _dot(a_ref[...], w_ref[...])


def _out_odd(h, mixed, w, layer):
    m = h.shape[0]
    row = lambda n: pl.BlockSpec((WIDE_TILE, n), lambda i: (i, 0))
    return pl.pallas_call(
        _out_odd_body,
        grid=(m // WIDE_TILE,),
        in_specs=[row(D_MODEL), row(C_VD), _resident_layer(w, layer)],
        out_specs=row(D_MODEL),
        out_shape=jax.ShapeDtypeStruct((m, D_MODEL), F32),
        compiler_params=_cparams("parallel"),
        name="out_proj_odd",
    )(h, mixed, w)


def _ffn_body(hp_ref, h_ref, hn_ref, p_ref, gf_ref, wup_ref, cw_ref, cb_ref, wdn_ref, gp_ref, wg_ref, wp_ref,
              go_ref, y_ref, xe_ref, act_ref, *, tiles_per_seq, final):
    i = pl.program_id(0) % tiles_per_seq
    gf = gf_ref[...]
    h = h_ref[...]
    keep_prev = jnp.where(i > 0, 1.0, 0.0).astype(F32)
    keep_next = jnp.where(i < tiles_per_seq - 1, 1.0, 0.0).astype(F32)
    xe_ref[0:HALO, :] = _rms(hp_ref[...], gf) * keep_prev
    xe_ref[HALO:HALO + WIDE_TILE, :] = _rms(h, gf)
    xe_ref[HALO + WIDE_TILE:, :] = _rms(hn_ref[...], gf) * keep_next
    xe = xe_ref[...].astype(BF16)
    rows = WIDE_TILE + 2 * HALO

    def conv(u, c):
        w = cw_ref[:, c:c + FF_CHUNK]
        prev = pltpu.roll(u, 1, 0)[HALO:HALO + WIDE_TILE]
        nxt = pltpu.roll(u, rows - 1, 0)[HALO:HALO + WIDE_TILE]
        return (prev * w[0:1] + u[HALO:HALO + WIDE_TILE] * w[1:2] + nxt * w[2:3] + cb_ref[:, c:c + FF_CHUNK])

    for c in range(0, D_FF, FF_CHUNK):
        a = conv(_dot(xe, wup_ref[:, c:c + FF_CHUNK]), c)
        gt = conv(_dot(xe, wup_ref[:, D_FF + c:D_FF + c + FF_CHUNK]), D_FF + c)
        act_ref[:, c:c + FF_CHUNK] = (a * (0.5 * gt * (1.0 + lax.erf(gt * (2.0 ** -0.5))))).astype(BF16)
    h2 = h + _dot(act_ref[...], wdn_ref[...])
    sig = _sigmoid(_dot(_rms(h2, gp_ref[...]).astype(BF16), wg_ref[...]))
    h3 = h2 + sig * _dot(p_ref[...].astype(BF16), wp_ref[...])
    y_ref[...] = _rms(h3, go_ref[...]) if final else h3


def _ffn_ple(h, p, t, gf, wup, cw, cb, wdn, gp, wg, wp, go, layer, final):
    m = h.shape[0]
    per_tile = WIDE_TILE // HALO
    last_blk = m // HALO - 1
    vec = lambda n: pl.BlockSpec((1, n), lambda i: (0, 0))
    return pl.pallas_call(
        functools.partial(_ffn_body, tiles_per_seq=t // WIDE_TILE, final=final),
        grid=(m // WIDE_TILE,),
        in_specs=[pl.BlockSpec((HALO, D_MODEL), lambda i: (jnp.maximum(i * per_tile - 1, 0), 0)),
                  pl.BlockSpec((WIDE_TILE, D_MODEL), lambda i: (i, 0)),
                  pl.BlockSpec((HALO, D_MODEL), lambda i: (jnp.minimum((i + 1) * per_tile, last_blk), 0)),
                  pl.BlockSpec((None, WIDE_TILE, PLE_DIM), lambda i: (layer, i, 0)),
                  vec(D_MODEL), _resident_layer(wup, layer), _resident_layer(cw, layer), vec(2 * D_FF),
                  _resident_layer(wdn, layer), vec(D_MODEL), _resident_layer(wg, layer), _resident_layer(wp, layer),
                  vec(D_MODEL)],
        out_specs=pl.BlockSpec((WIDE_TILE, D_MODEL), lambda i: (i, 0)),
        out_shape=jax.ShapeDtypeStruct((m, D_MODEL), F32),
        scratch_shapes=[pltpu.VMEM((WIDE_TILE + 2 * HALO, D_MODEL), F32), pltpu.VMEM((WIDE_TILE, D_FF), BF16)],
        compiler_params=pltpu.CompilerParams(dimension_semantics=("parallel",), vmem_limit_bytes=VMEM_LIMIT_FFN),
        name="conv_ffn_ple",
    )(h, h, h, p, gf, wup, cw, cb, wdn, gp, wg, wp, go)


def _prep_weights(ev_w_in, od_w_in, gla_w_gate_up, gla_b_gate):
    ev_w = ev_w_in.astype(BF16)
    od_w = jnp.pad(od_w_in, ((0, 0), (0, 0), (0, LR_PAD - 2 * C_RANK))).astype(BF16)
    n_odd = od_w_in.shape[0]
    w2 = jnp.zeros((n_odd, LR_PAD, 2 * C_KD), F32)
    w2 = w2.at[:, :C_RANK, :C_KD].set(gla_w_gate_up[:, 0]).at[:, C_RANK:2 * C_RANK, C_KD:].set(gla_w_gate_up[:, 1])
    b2 = gla_b_gate.reshape(n_odd, 1, 2 * C_KD)
    return ev_w, od_w, w2.astype(BF16), b2


def _trunk(x, p, prm):
    bsz, t, _ = x.shape
    m = bsz * t
    h = x.reshape(m, D_MODEL)
    for l in range(DEPTH):
        g_mix = prm["norm_mix_g"][l][None]
        if l % 2 == 0:
            e = l // 2
            lb = jnp.concatenate([prm["lb"][0, e], prm["lb"][1, e]])[None]
            a16, a32, *qkv = _in_proj_even(h, g_mix, lb, prm["ev_w"], e)
            out_a = _hgrn_mixer(a16.reshape(bsz, t, -1), a32.reshape(bsz, t, -1), prm["hgrn_norm_g"][e][None])
            att = [_dilated_attention(x_g.reshape(bsz, t // dil, dil * QKV_W), gi)
                   for gi, (x_g, (_, dil)) in enumerate(zip(qkv, B_GROUPS))]
            h = _out_even(h, out_a.reshape(m, A_W),
                          [o.reshape(m // dil, dil * A_W) for (o, _), (_, dil) in zip(att, B_GROUPS)],
                          [l_.reshape(m // dil, dil * LSE_W) for (_, l_), (_, dil) in zip(att, B_GROUPS)],
                          prm["ev_w_out"], e)
        else:
            o = l // 2
            c16, lg = _in_proj_odd(h, g_mix, prm["od_w"], prm["od_w2"], prm["od_b2"][o], o)
            mixed = _gla_mixer(c16.reshape(bsz, t, -1), lg.reshape(bsz, t, -1), prm["gla_norm_g"][o][None])
            h = _out_odd(h, mixed.reshape(m, C_VD), prm["od_w_out"], o)
        h = _ffn_ple(h, p.reshape(DEPTH, m, PLE_DIM), t, prm["norm_ffn_g"][l][None], prm["ffn_w_up"],
                     prm["ffn_conv_w"], prm["ffn_conv_b"][l][None], prm["ffn_w_down"],
                     prm["norm_ple_g"][l][None], prm["ple_w_gate"], prm["ple_w_proj"],
                     prm["norm_out_g"][None], layer=l, final=(l == DEPTH - 1))
    return h.reshape(bsz, t, D_MODEL)


def kernel(x_prompt, x_sample, p_prompt, p_sample, norm_mix_g, ev_w_in, hgrn_lb_logits, hgrn_norm_g, ev_w_out, od_w_in, gla_w_gate_up, gla_b_gate, gla_norm_g, od_w_out, norm_ffn_g, ffn_w_up, ffn_conv_w, ffn_conv_b, ffn_w_down, norm_ple_g, ple_w_gate, ple_w_proj, norm_out_g):
    lb = jnp.cumsum(jax.nn.softmax(hgrn_lb_logits.astype(F32), axis=1), axis=1)
    lb = lb - lb[:, :1]
    ev_w, od_w, od_w2, od_b2 = _prep_weights(ev_w_in, od_w_in, gla_w_gate_up, gla_b_gate)
    prm = dict(norm_mix_g=norm_mix_g, ev_w=ev_w, lb=lb, hgrn_norm_g=hgrn_norm_g, ev_w_out=ev_w_out.astype(BF16),
               od_w=od_w, od_w2=od_w2, od_b2=od_b2, gla_norm_g=gla_norm_g, od_w_out=od_w_out.astype(BF16),
               norm_ffn_g=norm_ffn_g, ffn_w_up=ffn_w_up.astype(BF16), ffn_conv_w=ffn_conv_w,
               ffn_conv_b=ffn_conv_b, ffn_w_down=ffn_w_down.astype(BF16), norm_ple_g=norm_ple_g,
               ple_w_gate=ple_w_gate.astype(BF16), ple_w_proj=ple_w_proj.astype(BF16), norm_out_g=norm_out_g)
    return _trunk(x_prompt, p_prompt, prm), _trunk(x_sample, p_sample, prm)
```

```python
import functools

import numpy as np
import jax
import jax.numpy as jnp
from jax import lax
from jax.experimental import pallas as pl
from jax.experimental.pallas import tpu as pltpu

F32 = jnp.float32
BF16 = jnp.bfloat16

D_MODEL = 1024
DEPTH = 4
PLE_DIM = 256
EPS = 1e-6
HEADS = 4
HD = 128
A_W = HEADS * HD
B_GROUPS = ((128, 1), (512, 4), (2048, 16))
N_GROUPS = len(B_GROUPS)
QKV_W = 3 * A_W
ATT_Q = 128
ATT_CLASSES = 4
ALIBI_MAX_EXP = 8.0
C_KD = 512
C_VD = 1024
C_HDV = C_VD // HEADS
C_RANK = 16
GATE_NORMALIZER = 16.0
CHUNK = 64
SUB = 16
N_SUB = CHUNK // SUB
SUB8 = 8
CHUNKS_PER_STEP = {"chunk": 16, "block": 8, "pair": 4}
LOG_DECAY_MIN = -30.0
LOG2E = 1.4426950408889634
SAFE_LOG2_RANGE = {"chunk": 116.0, "block": 96.0}
NEG_INF = -1e30
D_FF = 2816
LSE_W = 128
LSE_REP = LSE_W // HEADS
LR_PAD = 128

ROW_TILE = 512
STREAM_BUFFERS = 3
WIDE_TILE = 1024
HALO = 8
FF_CHUNK = 256
VMEM_LIMIT = 56 * 1024 * 1024
VMEM_LIMIT_FFN = 60 * 1024 * 1024


def _cparams(*sem):
    return pltpu.CompilerParams(dimension_semantics=sem, vmem_limit_bytes=VMEM_LIMIT)


def _resident(shape):
    nd = len(shape)
    return pl.BlockSpec(shape, lambda *_: (0,) * nd, pipeline_mode=pl.Buffered(1))


def _resident_layer(stacked, layer):
    nd = stacked.ndim
    return pl.BlockSpec((None,) + stacked.shape[1:], lambda *_: (layer,) + (0,) * (nd - 1),
                        pipeline_mode=pl.Buffered(1))


def _rms(x, g):
    ms = jnp.mean(x * x, axis=-1, keepdims=True)
    return x * lax.rsqrt(ms + EPS) * g


def _dot(a, b):
    return jnp.dot(a, b, preferred_element_type=F32)


def _dot_nt(a, b):
    return lax.dot_general(a, b, (((1,), (1,)), ((), ())), preferred_element_type=F32)


def _dot_tn(a, b):
    return lax.dot_general(a, b, (((0,), (0,)), ((), ())), preferred_element_type=F32)


def _store_cols(xn, w_ref, out_refs, c0=0, col_chunk=512):
    for o_ref in out_refs:
        n = o_ref.shape[-1]
        for j in range(0, n, col_chunk):
            wj = min(col_chunk, n - j)
            o_ref[:, j:j + wj] = _dot(xn, w_ref[:, c0 + j:c0 + j + wj]).astype(o_ref.dtype)
        c0 += n
    return c0


def _clamp_log2_decay(log2_f):
    return jnp.maximum(log2_f, LOG_DECAY_MIN * LOG2E)


def _in_proj_even_body(x_ref, g_ref, lb_ref, w_ref, a16_ref, a32_ref, g0_ref, g1_ref, g2_ref, u_ref):
    xf = _rms(x_ref[...], g_ref[...])
    xn = xf.astype(BF16)
    lane_tiles = D_MODEL // HD
    for c in range(lane_tiles):
        u_ref[c] = xf[:, c * HD:(c + 1) * HD]

    def class_major(dil):
        n = ROW_TILE // dil
        rows = [jnp.concatenate([u_ref[c, pl.ds(r, n, stride=dil), :] for c in range(lane_tiles)], axis=1)
                for r in range(dil)]
        return jnp.concatenate(rows, axis=0).astype(BF16)

    _store_cols(xn, w_ref, (a16_ref.at[:, :A_W],))
    _store_cols(xn, w_ref, (a16_ref.at[:, A_W:3 * A_W],), c0=3 * A_W)
    c0 = 5 * A_W
    for d in range(2):
        lb = lb_ref[:, d * A_W:(d + 1) * A_W]
        f = lb + (1.0 - lb) * _sigmoid(_dot(xn, w_ref[:, (1 + d) * A_W:(2 + d) * A_W]))
        a32_ref[:, d * A_W:(d + 1) * A_W] = _clamp_log2_decay(jnp.log2(f))
        a16_ref[:, (3 + d) * A_W:(4 + d) * A_W] = (1.0 - f).astype(BF16)
    for (_, dil), o_ref in zip(B_GROUPS, (g0_ref, g1_ref, g2_ref)):
        n = ROW_TILE // dil
        lhs = xn if dil == 1 else class_major(dil)
        for j in range(3):
            u = _dot(lhs, w_ref[:, c0:c0 + A_W])
            c0 += A_W
            if j == 0:
                u = u * (HD ** -0.5)
            for r in range(dil):
                o_ref[:, (3 * r + j) * A_W:(3 * r + j + 1) * A_W] = u[r * n:(r + 1) * n].astype(BF16)


def _in_proj_odd_body(x_ref, g_ref, w_ref, w2_ref, b2_ref, c16_ref, g2_ref):
    xn = _rms(x_ref[...], g_ref[...]).astype(BF16)
    c16_ref[:, :C_KD] = (_dot(xn, w_ref[:, :C_KD]) * (HD ** -0.5)).astype(BF16)
    c0 = _store_cols(xn, w_ref, (c16_ref.at[:, C_KD:],), c0=C_KD)
    lr = _dot(xn, w_ref[:, c0:c0 + LR_PAD]).astype(BF16)
    n = g2_ref.shape[-1]
    for j in range(0, n, 512):
        logit = _dot(lr, w2_ref[:, j:j + 512]) + b2_ref[:, j:j + 512]
        g2_ref[:, j:j + 512] = _clamp_log2_decay(_log2_sigmoid(logit) * (1.0 / GATE_NORMALIZER))


def _in_proj_even(x, g, lb, w, layer):
    m = x.shape[0]
    outs = [((m, 5 * A_W), (ROW_TILE, 5 * A_W), BF16), ((m, 2 * A_W), (ROW_TILE, 2 * A_W), F32)]
    for _, dil in B_GROUPS:
        outs.append(((m // dil, dil * QKV_W), (ROW_TILE // dil, dil * QKV_W), BF16))
    return pl.pallas_call(
        _in_proj_even_body,
        grid=(m // ROW_TILE,),
        in_specs=[pl.BlockSpec((ROW_TILE, D_MODEL), lambda i: (i, 0)),
                  pl.BlockSpec((1, D_MODEL), lambda i: (0, 0)),
                  pl.BlockSpec((1, 2 * A_W), lambda i: (0, 0)),
                  _resident_layer(w, layer)],
        out_specs=[pl.BlockSpec(blk, lambda i: (i, 0)) for _, blk, _ in outs],
        out_shape=[jax.ShapeDtypeStruct(shp, dt) for shp, _, dt in outs],
        scratch_shapes=[pltpu.VMEM((D_MODEL // HD, ROW_TILE, HD), F32)],
        compiler_params=_cparams("parallel"),
        name="in_proj_even",
    )(x, g, lb, w)


def _in_proj_odd(x, g, w, w2, b2, layer):
    m = x.shape[0]
    n16 = 2 * C_KD + 2 * C_VD
    return pl.pallas_call(
        _in_proj_odd_body,
        grid=(m // WIDE_TILE,),
        in_specs=[pl.BlockSpec((WIDE_TILE, D_MODEL), lambda i: (i, 0)),
                  pl.BlockSpec((1, D_MODEL), lambda i: (0, 0)),
                  _resident_layer(w, layer), _resident_layer(w2, layer),
                  pl.BlockSpec((1, 2 * C_KD), lambda i: (0, 0))],
        out_specs=[pl.BlockSpec((WIDE_TILE, n16), lambda i: (i, 0)),
                   pl.BlockSpec((WIDE_TILE, 2 * C_KD), lambda i: (i, 0))],
        out_shape=[jax.ShapeDtypeStruct((m, n16), BF16), jax.ShapeDtypeStruct((m, 2 * C_KD), F32)],
        compiler_params=_cparams("parallel"),
        name="in_proj_odd",
    )(x, g, w, w2, b2)


def _off_ranges(rev):
    out = []
    for j in range(N_SUB):
        lo, hi = (0, SUB * j) if rev else (SUB * (j + 1), CHUNK)
        if hi > lo:
            out.append((j, lo, hi))
    return out


def _gla_constants(rev):
    t = np.arange(CHUNK)
    tri = (t[None, :] >= t[:, None]) if rev else (t[None, :] <= t[:, None])
    causal = tri
    same8 = (t[None, :] // SUB8) == (t[:, None] // SUB8)
    same16 = (t[None, :] // SUB) == (t[:, None] // SUB)
    f = lambda a: jnp.asarray(a.astype(np.float32))
    return (jnp.asarray(np.concatenate([tri, tri], axis=1).astype(np.float32), BF16),
            f(causal), f(same16 & causal), f(same8 & causal), f(same16 & ~same8 & causal))


def _gla_shared_constants():
    e = np.zeros((SUB8 * HD, CHUNK), np.float32)
    for s in range(SUB8):
        e[s * HD:(s + 1) * HD, s::SUB8] = 1.0
    r = np.arange(ROW_TILE)
    sel = [(r[None, :] // n == np.arange(ROW_TILE // n)[:, None]).astype(np.float32) for n in (SUB, CHUNK)]
    return jnp.asarray(e, BF16), jnp.asarray(np.concatenate(sel, axis=0), BF16)


def _rows(ref, r, n):
    return jnp.broadcast_to(ref[pl.ds(r, 1), :], (n, HD))


def _gla_cumsum(k, g2, b_ref, k_ref, tri_ref, form):
    g_hi = g2.astype(BF16)
    g_lo = (g2 - g_hi.astype(F32)).astype(BF16)
    b = _dot(tri_ref[...], jnp.concatenate([g_hi, g_lo], axis=0))
    b_ref[...] = b
    if form == "pair":
        k_ref[...] = k
    return b


def _gla_scores(qs, k, v, b, st_ref, b_ref, k_ref, emat_ref, rev, form):
    b_tot = b_ref[pl.ds(0 if rev else CHUNK - 1, 1), :]

    st = st_ref[...]
    qd = qs * jnp.exp2(b)
    kd = (k * jnp.exp2(b_tot - b)).astype(BF16)
    o = _dot_nt(qd.astype(BF16), st.astype(BF16))
    st_ref[...] = st * jnp.exp2(b_tot) + _dot_tn(v, kd)
    if form == "chunk":
        return o, _dot_nt((qd * jnp.exp2(-b_tot)).astype(BF16), kd)

    edge16 = [SUB * i + (0 if rev else SUB - 1) for i in range(N_SUB)]
    e_ko = jnp.exp2(jnp.concatenate([_rows(b_ref, e, SUB) for e in edge16], axis=0) - b)
    k_off = k * e_ko
    zq = jnp.zeros((SUB, HD), F32)
    q_parts, k_parts = [], []
    for j, lo, hi in _off_ranges(rev):
        q_off = qs[lo:hi] * jnp.exp2(jnp.minimum(b[lo:hi] - _rows(b_ref, edge16[j], 1), 0.0))
        q_parts.append(jnp.concatenate([zq] * (lo // SUB) + [q_off] + [zq] * ((CHUNK - hi) // SUB), axis=0))
        k_parts.append(jnp.concatenate([zq] * j + [k_off[SUB * j:SUB * (j + 1)]] + [zq] * (N_SUB - 1 - j),
                                       axis=0))
    a_16 = _dot_nt(jnp.concatenate(q_parts, axis=1).astype(BF16),
                   jnp.concatenate(k_parts, axis=1).astype(BF16))

    if form == "block":
        return o, a_16, _dot_nt((qs * (1.0 / e_ko)).astype(BF16), k_off.astype(BF16))

    edge8 = [SUB * i + (SUB8 if rev else SUB8 - 1) for i in range(N_SUB)]
    d8 = b - jnp.concatenate([_rows(b_ref, e, SUB) for e in edge8], axis=0)
    second = (lax.broadcasted_iota(jnp.int32, (CHUNK, HD), 0) % SUB) >= SUB8
    q_side = jnp.logical_not(second) if rev else second
    w8 = jnp.exp2(jnp.minimum(jnp.where(q_side, d8, -d8), 0.0))
    a_8 = _dot_nt(jnp.where(q_side, qs * w8, 0.0).astype(BF16), jnp.where(q_side, 0.0, k * w8).astype(BF16))

    slabs = []
    for s in range(SUB8):
        rows = [SUB8 * i + s for i in range(CHUNK // SUB8)]
        b_s = jnp.concatenate([_rows(b_ref, r, SUB8) for r in rows], axis=0)
        k_s = jnp.concatenate([_rows(k_ref, r, SUB8) for r in rows], axis=0)
        slabs.append((qs * k_s * jnp.exp2(jnp.minimum(b - b_s, 0.0))).astype(BF16))
    a_dg = _dot(jnp.concatenate(slabs, axis=1), emat_ref[...])
    return o, a_16, a_8, a_dg


def _gla_output(scores, v, m_c_ref, m_16_ref, m_dg_ref, m_8_ref, form):
    if form == "chunk":
        o, a_c = scores
        a = jnp.where(m_c_ref[...] > 0.0, a_c, 0.0)
    elif form == "block":
        o, a_16, a_in = scores
        a = a_in * m_16_ref[...] + a_16
    else:
        o, a_16, a_8, a_dg = scores
        a = a_dg * m_dg_ref[...] + a_8 * m_8_ref[...] + a_16
    return o + _dot(a.astype(BF16), v)


def _decay_ranges_ok(g_ref, sel_ref, n_tiles):
    sums = [_dot(sel_ref[...], g_ref[i * ROW_TILE:(i + 1) * ROW_TILE, :].astype(BF16)) for i in range(n_tiles)]
    low = functools.reduce(jnp.minimum, sums)
    n16 = ROW_TILE // SUB
    return (jnp.min(low[n16:]) > -SAFE_LOG2_RANGE["chunk"], jnp.min(low[:n16]) > -SAFE_LOG2_RANGE["block"])


def _sigmoid(x):
    return 1.0 / (1.0 + jnp.exp(-x))


def _log2_sigmoid(x):
    return jnp.minimum(x, 0.0) * LOG2E - jnp.log2(1.0 + jnp.exp2(jnp.abs(x) * -LOG2E))


def _bidir_body(q_ref, kf_ref, kb_ref, g2f_ref, g2b_ref, v_ref, gate_ref, ng_ref,
                trif_ref, mcf_ref, m16f_ref, mdf_ref, m8f_ref, trib_ref, mcb_ref, m16b_ref, mdb_ref, m8b_ref,
                e_ref, sel_ref,
                o_ref, accf_ref, accb_ref, sf_ref, sb_ref, stage_ref, *, nc):
    def run(form):
        per = CHUNKS_PER_STEP[form]
        sf_ref[...] = jnp.zeros_like(sf_ref)
        sb_ref[...] = jnp.zeros_like(sb_ref)

        def step(c, carry):
            streams = []
            for u, rev in [(u, rev) for u in range(per) for rev in (False, True)]:
                cf = c * per + u
                sl = pl.ds(pl.multiple_of((nc - 1 - cf if rev else cf) * CHUNK, CHUNK), CHUNK)
                b_ref = stage_ref.at[4 * u + 2 * int(rev)]
                kst_ref = stage_ref.at[4 * u + 2 * int(rev) + 1]
                k = (kb_ref if rev else kf_ref)[sl, :].astype(F32)
                qs = q_ref[sl, :].astype(F32)
                b = _gla_cumsum(k, (g2b_ref if rev else g2f_ref)[sl, :], b_ref, kst_ref,
                                trib_ref if rev else trif_ref, form)
                streams.append((rev, sl, qs, k, b, b_ref, kst_ref))
            scores = [_gla_scores(qs, k, v_ref[sl, :], b, sb_ref if rev else sf_ref, b_ref, kst_ref, e_ref,
                                  rev, form)
                      for rev, sl, qs, k, b, b_ref, kst_ref in streams]
            for (rev, sl, *_), sc in zip(streams, scores):
                masks = (mcb_ref, m16b_ref, mdb_ref, m8b_ref) if rev else (mcf_ref, m16f_ref, mdf_ref, m8f_ref)
                (accb_ref if rev else accf_ref)[sl, :] = _gla_output(sc, v_ref[sl, :], *masks, form)
            return carry

        lax.fori_loop(0, nc // per, step, 0)

    n_tiles = (nc * CHUNK) // ROW_TILE
    chunk_f, block_f = _decay_ranges_ok(g2f_ref, sel_ref, n_tiles)
    chunk_b, block_b = _decay_ranges_ok(g2b_ref, sel_ref, n_tiles)
    chunk_ok = jnp.logical_and(chunk_f, chunk_b)
    block_ok = jnp.logical_and(jnp.logical_and(block_f, block_b), jnp.logical_not(chunk_ok))
    pl.when(chunk_ok)(functools.partial(run, "chunk"))
    pl.when(block_ok)(functools.partial(run, "block"))
    pl.when(jnp.logical_not(jnp.logical_or(chunk_ok, block_ok)))(functools.partial(run, "pair"))

    ng = ng_ref[...]

    def finish(i, carry):
        sl = pl.ds(pl.multiple_of(i * ROW_TILE, ROW_TILE), ROW_TILE)
        o = accf_ref[sl, :] + accb_ref[sl, :]
        gt = gate_ref[sl, :].astype(F32)
        o_ref[sl, :] = (_rms(o, ng) * (gt * _sigmoid(gt))).astype(o_ref.dtype)
        return carry

    lax.fori_loop(0, (nc * CHUNK) // ROW_TILE, finish, 0)


def _bidir_call(name, bsz, t, dv, args, in_specs):
    consts = (*_gla_constants(False), *_gla_constants(True), *_gla_shared_constants())
    const = lambda shape: pl.BlockSpec(shape, lambda b, h: (0,) * len(shape))
    return pl.pallas_call(
        functools.partial(_bidir_body, nc=t // CHUNK),
        grid=(bsz, HEADS),
        in_specs=in_specs + [const(c.shape) for c in consts],
        out_specs=pl.BlockSpec((None, t, dv), lambda b, h: (b, 0, h)),
        out_shape=jax.ShapeDtypeStruct((bsz, t, HEADS * dv), BF16),
        scratch_shapes=[pltpu.VMEM((t, dv), F32), pltpu.VMEM((t, dv), F32),
                        pltpu.VMEM((dv, HD), F32), pltpu.VMEM((dv, HD), F32)]
                       + [pltpu.VMEM((4 * max(CHUNKS_PER_STEP.values()), CHUNK, HD), F32)],
        compiler_params=_cparams("parallel", "parallel"),
        name=name,
    )(*args, *consts)


def _hgrn_mixer(a16, a32, norm_g):
    bsz, t, _ = a16.shape
    col = lambda off: pl.BlockSpec((None, t, HD), lambda b, h: (b, 0, off * HEADS + h))
    return _bidir_call("hgrn2_bidir", bsz, t, HD,
                       (a16, a16, a16, a32, a32, a16, a16, norm_g),
                       [col(0), col(3), col(4), col(0), col(1), col(1), col(2),
                        pl.BlockSpec((1, HD), lambda b, h: (0, h))])


def _gla_mixer(c16, g2, norm_g):
    bsz, t, _ = c16.shape
    col = lambda off: pl.BlockSpec((None, t, HD), lambda b, h: (b, 0, off * HEADS + h))
    wide = lambda off: pl.BlockSpec((None, t, C_HDV), lambda b, h: (b, 0, off + h))
    v_off = 2 * C_KD // C_HDV
    return _bidir_call("gla_bidir", bsz, t, C_HDV,
                       (c16, c16, c16, g2, g2, c16, c16, norm_g),
                       [col(0), col(1), col(1), col(0), col(1), wide(v_off), wide(v_off + HEADS),
                        pl.BlockSpec((1, C_HDV), lambda b, h: (0, h))])


def _attn_window(length, radius):
    win = min(ATT_Q + 2 * radius, length)
    return win, (0, -radius, ATT_Q - win)


def _attn_bias(gi, length):
    wsize, dil = B_GROUPS[gi]
    radius = wsize // (2 * dil)
    win, offsets = _attn_window(length, radius)
    n = N_GROUPS * HEADS
    slopes = np.array([2.0 ** (-ALIBI_MAX_EXP * (gi * HEADS + h + 1) / n) for h in range(HEADS)])
    dist = np.abs(np.array(offsets)[:, None, None] + np.arange(win)[None, None, :] - np.arange(ATT_Q)[None, :, None])
    bias = np.where(dist[:, None] <= radius, -slopes[None, :, None, None] * (dil * dist[:, None]), NEG_INF)
    return jnp.asarray(bias, F32)


def _attn_body(x_ref, bias_ref, o_ref, l_ref, *, length, radius, classes):
    win, _ = _attn_window(length, radius)
    n_tiles = length // ATT_Q
    per = max(n for n in (4, 2, 1) if n * classes <= ATT_CLASSES and n_tiles % n == 0)

    def step(i, carry):
        probs = []
        for u in range(per):
            tile = i * per + u
            m0 = pl.multiple_of(tile * ATT_Q, ATT_Q)
            k0 = pl.multiple_of(jnp.clip(m0 - radius, 0, length - win), radius)
            place = jnp.where(tile == 0, 0, jnp.where(tile == n_tiles - 1, 2, 1))
            probs += [(m0, k0, place, r, h) for r in range(classes) for h in range(HEADS)]
        col = lambda r, j, h: pl.ds((3 * r + j) * A_W + h * HD, HD)
        s = [_dot_nt(x_ref[pl.ds(m0, ATT_Q), col(r, 0, h)], x_ref[pl.ds(k0, win), col(r, 1, h)])
             for m0, k0, _, r, h in probs]
        s = [x + bias_ref[place, h] for x, (_, _, place, _, h) in zip(s, probs)]
        mx = [jnp.max(x, axis=-1, keepdims=True) for x in s]
        p = [jnp.exp(x - m) for x, m in zip(s, mx)]
        den = [jnp.sum(x, axis=-1, keepdims=True) for x in p]
        o = [_dot(x.astype(BF16), x_ref[pl.ds(k0, win), col(r, 2, h)]) for x, (_, k0, _, r, h) in zip(p, probs)]
        for x, d, m, (m0, _, _, r, h) in zip(o, den, mx, probs):
            o_ref[pl.ds(m0, ATT_Q), pl.ds(r * A_W + h * HD, HD)] = (x * (1.0 / d)).astype(o_ref.dtype)
            l_ref[pl.ds(m0, ATT_Q), pl.ds(r * LSE_W + h * LSE_REP, LSE_REP)] = jnp.broadcast_to(
                m + jnp.log(d), (ATT_Q, LSE_REP))
        return carry

    lax.fori_loop(0, n_tiles // per, step, 0)


def _dilated_attention(qkv, gi):
    bsz, length, _ = qkv.shape
    wsize, dil = B_GROUPS[gi]
    classes = min(dil, ATT_CLASSES)
    bias = _attn_bias(gi, length)
    return pl.pallas_call(
        functools.partial(_attn_body, length=length, radius=wsize // (2 * dil), classes=classes),
        grid=(bsz, dil // classes),
        in_specs=[pl.BlockSpec((None, length, classes * QKV_W), lambda b, r: (b, 0, r)),
                  pl.BlockSpec(bias.shape, lambda b, r: (0, 0, 0, 0))],
        out_specs=[pl.BlockSpec((None, length, classes * A_W), lambda b, r: (b, 0, r)),
                   pl.BlockSpec((None, length, classes * LSE_W), lambda b, r: (b, 0, r))],
        out_shape=[jax.ShapeDtypeStruct((bsz, length, dil * A_W), BF16),
                   jax.ShapeDtypeStruct((bsz, length, dil * LSE_W), F32)],
        compiler_params=_cparams("parallel", "parallel"),
        name=f"dilated_attn_g{gi}",
    )(qkv, bias)


def _lse_expansion():
    e = np.zeros((LSE_W, A_W), np.float32)
    for h in range(HEADS):
        e[h * LSE_REP, h * HD:(h + 1) * HD] = 1.0
    return jnp.asarray(e, BF16)


def _out_even_body(h_ref, a_ref, o0_ref, o1_ref, o2_ref, l0_ref, l1_ref, l2_ref, e_ref, w_ref, y_ref,
                   ob1_ref, ob2_ref, lb1_ref, lb2_ref):
    def natural(o_ref, l_ref, ob_ref, lb_ref, dil):
        if dil == 1:
            return o_ref[...].astype(F32), l_ref[...]
        n = WIDE_TILE // dil
        for r in range(dil):
            for c in range(HEADS):
                col = r * A_W + c * HD
                ob_ref[c, pl.ds(r, n, stride=dil), :] = o_ref[:, col:col + HD].astype(F32)
            lb_ref[pl.ds(r, n, stride=dil), :] = l_ref[:, r * LSE_W:(r + 1) * LSE_W]
        return jnp.concatenate([ob_ref[c] for c in range(HEADS)], axis=1), lb_ref[...]

    parts = [natural(o0_ref, l0_ref, None, None, B_GROUPS[0][1]),
             natural(o1_ref, l1_ref, ob1_ref, lb1_ref, B_GROUPS[1][1]),
             natural(o2_ref, l2_ref, ob2_ref, lb2_ref, B_GROUPS[2][1])]
    lses = [l for _, l in parts]
    mx = jnp.maximum(jnp.maximum(lses[0], lses[1]), lses[2])
    es = [jnp.exp(l - mx) for l in lses]
    inv = 1.0 / (es[0] + es[1] + es[2])
    emat = e_ref[...]
    mixed_b = None
    for e_g, (o_g, _) in zip(es, parts):
        alpha = (e_g * inv).astype(BF16)
        term = _dot(alpha, emat) * o_g
        mixed_b = term if mixed_b is None else mixed_b + term
    y_ref[...] = (h_ref[...] + _dot(a_ref[...], w_ref[:A_W, :]) + _dot(mixed_b.astype(BF16), w_ref[A_W:, :]))


def _out_even(h, out_a, os_, lses, w, layer):
    m = h.shape[0]
    emat = _lse_expansion()
    row = lambda n: pl.BlockSpec((WIDE_TILE, n), lambda i: (i, 0))
    grp = lambda width: [pl.BlockSpec((WIDE_TILE // dil, dil * width), lambda i: (i, 0)) for _, dil in B_GROUPS]
    return pl.pallas_call(
        _out_even_body,
        grid=(m // WIDE_TILE,),
        in_specs=[row(D_MODEL), row(A_W)] + grp(A_W) + grp(LSE_W) + [_resident(emat.shape), _resident_layer(w, layer)],
        out_specs=row(D_MODEL),
        out_shape=jax.ShapeDtypeStruct((m, D_MODEL), F32),
        scratch_shapes=[pltpu.VMEM((HEADS, WIDE_TILE, HD), F32), pltpu.VMEM((HEADS, WIDE_TILE, HD), F32),
                        pltpu.VMEM((WIDE_TILE, LSE_W), F32), pltpu.VMEM((WIDE_TILE, LSE_W), F32)],
        compiler_params=_cparams("parallel"),
        name="out_proj_even",
    )(h, out_a, *os_, *lses, emat, w)


def _out_odd_body(h_hbm, a_hbm, w_ref, y_hbm, *, steps):
    def tile(h_ref, a_ref, y_ref):
        y_ref[...] = h_ref[...] + _dot(a_ref[...], w_ref[...])

    row = lambda n, **kw: pl.BlockSpec((WIDE_TILE, n), lambda i: (i, 0), **kw)
    deep = dict(pipeline_mode=pl.Buffered(STREAM_BUFFERS))
    pltpu.emit_pipeline(tile, grid=(steps,), in_specs=[row(D_MODEL, **deep), row(C_VD, **deep)],
                        out_specs=[row(D_MODEL)])(h_hbm, a_hbm, y_hbm)


def _out_odd(h, mixed, w, layer):
    m = h.shape[0]
    hbm = pl.BlockSpec(memory_space=pl.ANY)
    return pl.pallas_call(
        functools.partial(_out_odd_body, steps=m // WIDE_TILE),
        grid=(1,),
        in_specs=[hbm, hbm, _resident_layer(w, layer)],
        out_specs=hbm,
        out_shape=jax.ShapeDtypeStruct((m, D_MODEL), F32),
        compiler_params=_cparams("arbitrary"),
        name="out_proj_odd",
    )(h, mixed, w)


def _ffn_body(hp_ref, h_ref, hn_ref, p_ref, gf_ref, wup_ref, cw_ref, cb_ref, wdn_ref, gp_ref, wg_ref, wp_ref,
              go_ref, y_ref, xe_ref, act_ref, *, tiles_per_seq, final):
    i = pl.program_id(0) % tiles_per_seq
    gf = gf_ref[...]
    h = h_ref[...]
    keep_prev = jnp.where(i > 0, 1.0, 0.0).astype(F32)
    keep_next = jnp.where(i < tiles_per_seq - 1, 1.0, 0.0).astype(F32)
    xe_ref[0:HALO, :] = _rms(hp_ref[...], gf) * keep_prev
    xe_ref[HALO:HALO + WIDE_TILE, :] = _rms(h, gf)
    xe_ref[HALO + WIDE_TILE:, :] = _rms(hn_ref[...], gf) * keep_next
    xe = xe_ref[...].astype(BF16)
    rows = WIDE_TILE + 2 * HALO

    def conv(u, c):
        w = cw_ref[:, c:c + FF_CHUNK]
        prev = pltpu.roll(u, 1, 0)[HALO:HALO + WIDE_TILE]
        nxt = pltpu.roll(u, rows - 1, 0)[HALO:HALO + WIDE_TILE]
        return (prev * w[0:1] + u[HALO:HALO + WIDE_TILE] * w[1:2] + nxt * w[2:3] + cb_ref[:, c:c + FF_CHUNK])

    for c in range(0, D_FF, FF_CHUNK):
        a = conv(_dot(xe, wup_ref[:, c:c + FF_CHUNK]), c)
        gt = conv(_dot(xe, wup_ref[:, D_FF + c:D_FF + c + FF_CHUNK]), D_FF + c)
        act_ref[:, c:c + FF_CHUNK] = (a * (0.5 * gt * (1.0 + lax.erf(gt * (2.0 ** -0.5))))).astype(BF16)
    h2 = h + _dot(act_ref[...], wdn_ref[...])
    sig = _sigmoid(_dot(_rms(h2, gp_ref[...]).astype(BF16), wg_ref[...]))
    h3 = h2 + sig * _dot(p_ref[...].astype(BF16), wp_ref[...])
    y_ref[...] = _rms(h3, go_ref[...]) if final else h3


def _ffn_ple(h, p, t, gf, wup, cw, cb, wdn, gp, wg, wp, go, layer, final):
    m = h.shape[0]
    per_tile = WIDE_TILE // HALO
    last_blk = m // HALO - 1
    vec = lambda n: pl.BlockSpec((1, n), lambda i: (0, 0))
    return pl.pallas_call(
        functools.partial(_ffn_body, tiles_per_seq=t // WIDE_TILE, final=final),
        grid=(m // WIDE_TILE,),
        in_specs=[pl.BlockSpec((HALO, D_MODEL), lambda i: (jnp.maximum(i * per_tile - 1, 0), 0)),
                  pl.BlockSpec((WIDE_TILE, D_MODEL), lambda i: (i, 0)),
                  pl.BlockSpec((HALO, D_MODEL), lambda i: (jnp.minimum((i + 1) * per_tile, last_blk), 0)),
                  pl.BlockSpec((None, WIDE_TILE, PLE_DIM), lambda i: (layer, i, 0)),
                  vec(D_MODEL), _resident_layer(wup, layer), _resident_layer(cw, layer), vec(2 * D_FF),
                  _resident_layer(wdn, layer), vec(D_MODEL), _resident_layer(wg, layer), _resident_layer(wp, layer),
                  vec(D_MODEL)],
        out_specs=pl.BlockSpec((WIDE_TILE, D_MODEL), lambda i: (i, 0)),
        out_shape=jax.ShapeDtypeStruct((m, D_MODEL), F32),
        scratch_shapes=[pltpu.VMEM((WIDE_TILE + 2 * HALO, D_MODEL), F32), pltpu.VMEM((WIDE_TILE, D_FF), BF16)],
        compiler_params=pltpu.CompilerParams(dimension_semantics=("parallel",), vmem_limit_bytes=VMEM_LIMIT_FFN),
        name="conv_ffn_ple",
    )(h, h, h, p, gf, wup, cw, cb, wdn, gp, wg, wp, go)


def _prep_weights(ev_w_in, od_w_in, gla_w_gate_up, gla_b_gate):
    ev_w = ev_w_in.astype(BF16)
    od_w = jnp.pad(od_w_in, ((0, 0), (0, 0), (0, LR_PAD - 2 * C_RANK))).astype(BF16)
    n_odd = od_w_in.shape[0]
    w2 = jnp.zeros((n_odd, LR_PAD, 2 * C_KD), F32)
    w2 = w2.at[:, :C_RANK, :C_KD].set(gla_w_gate_up[:, 0]).at[:, C_RANK:2 * C_RANK, C_KD:].set(gla_w_gate_up[:, 1])
    b2 = gla_b_gate.reshape(n_odd, 1, 2 * C_KD)
    return ev_w, od_w, w2.astype(BF16), b2


def _trunk(x, p, prm):
    bsz, t, _ = x.shape
    m = bsz * t
    h = x.reshape(m, D_MODEL)
    for l in range(DEPTH):
        g_mix = prm["norm_mix_g"][l][None]
        if l % 2 == 0:
            e = l // 2
            lb = jnp.concatenate([prm["lb"][0, e], prm["lb"][1, e]])[None]
            a16, a32, *qkv = _in_proj_even(h, g_mix, lb, prm["ev_w"], e)
            out_a = _hgrn_mixer(a16.reshape(bsz, t, -1), a32.reshape(bsz, t, -1), prm["hgrn_norm_g"][e][None])
            att = [_dilated_attention(x_g.reshape(bsz, t // dil, dil * QKV_W), gi)
                   for gi, (x_g, (_, dil)) in enumerate(zip(qkv, B_GROUPS))]
            h = _out_even(h, out_a.reshape(m, A_W),
                          [o.reshape(m // dil, dil * A_W) for (o, _), (_, dil) in zip(att, B_GROUPS)],
                          [l_.reshape(m // dil, dil * LSE_W) for (_, l_), (_, dil) in zip(att, B_GROUPS)],
                          prm["ev_w_out"], e)
        else:
            o = l // 2
            c16, lg = _in_proj_odd(h, g_mix, prm["od_w"], prm["od_w2"], prm["od_b2"][o], o)
            mixed = _gla_mixer(c16.reshape(bsz, t, -1), lg.reshape(bsz, t, -1), prm["gla_norm_g"][o][None])
            h = _out_odd(h, mixed.reshape(m, C_VD), prm["od_w_out"], o)
        h = _ffn_ple(h, p.reshape(DEPTH, m, PLE_DIM), t, prm["norm_ffn_g"][l][None], prm["ffn_w_up"],
                     prm["ffn_conv_w"], prm["ffn_conv_b"][l][None], prm["ffn_w_down"],
                     prm["norm_ple_g"][l][None], prm["ple_w_gate"], prm["ple_w_proj"],
                     prm["norm_out_g"][None], layer=l, final=(l == DEPTH - 1))
    return h.reshape(bsz, t, D_MODEL)


def kernel(x_prompt, x_sample, p_prompt, p_sample, norm_mix_g, ev_w_in, hgrn_lb_logits, hgrn_norm_g, ev_w_out, od_w_in, gla_w_gate_up, gla_b_gate, gla_norm_g, od_w_out, norm_ffn_g, ffn_w_up, ffn_conv_w, ffn_conv_b, ffn_w_down, norm_ple_g, ple_w_gate, ple_w_proj, norm_out_g):
    lb = jnp.cumsum(jax.nn.softmax(hgrn_lb_logits.astype(F32), axis=1), axis=1)
    lb = lb - lb[:, :1]
    ev_w, od_w, od_w2, od_b2 = _prep_weights(ev_w_in, od_w_in, gla_w_gate_up, gla_b_gate)
    prm = dict(norm_mix_g=norm_mix_g, ev_w=ev_w, lb=lb, hgrn_norm_g=hgrn_norm_g, ev_w_out=ev_w_out.astype(BF16),
               od_w=od_w, od_w2=od_w2, od_b2=od_b2, gla_norm_g=gla_norm_g, od_w_out=od_w_out.astype(BF16),
               norm_ffn_g=norm_ffn_g, ffn_w_up=ffn_w_up.astype(BF16), ffn_conv_w=ffn_conv_w,
               ffn_conv_b=ffn_conv_b, ffn_w_down=ffn_w_down.astype(BF16), norm_ple_g=norm_ple_g,
               ple_w_gate=ple_w_gate.astype(BF16), ple_w_proj=ple_w_proj.astype(BF16), norm_out_g=norm_out_g)
    return _trunk(x_prompt, p_prompt, prm), _trunk(x_sample, p_sample, prm)
```

```python
import functools

import numpy as np
import jax
import jax.numpy as jnp
from jax import lax
from jax.experimental import pallas as pl
from jax.experimental.pallas import tpu as pltpu

F32 = jnp.float32
BF16 = jnp.bfloat16

D_MODEL = 1024
DEPTH = 4
PLE_DIM = 256
EPS = 1e-6
HEADS = 4
HD = 128
A_W = HEADS * HD
B_GROUPS = ((128, 1), (512, 4), (2048, 16))
N_GROUPS = len(B_GROUPS)
QKV_W = 3 * A_W
ATT_Q = 128
ATT_CLASSES = 4
ALIBI_MAX_EXP = 8.0
C_KD = 512
C_VD = 1024
C_HDV = C_VD // HEADS
C_RANK = 16
GATE_NORMALIZER = 16.0
CHUNK = 64
SUB = 16
N_SUB = CHUNK // SUB
CUMSUM_GROUP = 2
SUB8 = 8
CHUNKS_PER_STEP = {"chunk": 16, "block": 8, "pair": 4}
LOG_DECAY_MIN = -30.0
LOG2E = 1.4426950408889634
SAFE_LOG2_RANGE = {"chunk": 116.0, "block": 96.0}
NEG_INF = -1e30
D_FF = 2816
LSE_W = 128
LSE_REP = LSE_W // HEADS
LR_PAD = 128

ROW_TILE = 512
STREAM_BUFFERS = 3
WIDE_TILE = 1024
HALO = 8
FF_CHUNK = 256
VMEM_LIMIT = 56 * 1024 * 1024
VMEM_LIMIT_FFN = 60 * 1024 * 1024


def _cparams(*sem):
    return pltpu.CompilerParams(dimension_semantics=sem, vmem_limit_bytes=VMEM_LIMIT)


def _resident(shape):
    nd = len(shape)
    return pl.BlockSpec(shape, lambda *_: (0,) * nd, pipeline_mode=pl.Buffered(1))


def _resident_layer(stacked, layer):
    nd = stacked.ndim
    return pl.BlockSpec((None,) + stacked.shape[1:], lambda *_: (layer,) + (0,) * (nd - 1),
                        pipeline_mode=pl.Buffered(1))


def _rms(x, g):
    ms = jnp.mean(x * x, axis=-1, keepdims=True)
    return x * lax.rsqrt(ms + EPS) * g


def _dot(a, b):
    return jnp.dot(a, b, preferred_element_type=F32)


def _dot_nt(a, b):
    return lax.dot_general(a, b, (((1,), (1,)), ((), ())), preferred_element_type=F32)


def _dot_tn(a, b):
    return lax.dot_general(a, b, (((0,), (0,)), ((), ())), preferred_element_type=F32)


def _store_cols(xn, w_ref, out_refs, c0=0, col_chunk=512):
    for o_ref in out_refs:
        n = o_ref.shape[-1]
        for j in range(0, n, col_chunk):
            wj = min(col_chunk, n - j)
            o_ref[:, j:j + wj] = _dot(xn, w_ref[:, c0 + j:c0 + j + wj]).astype(o_ref.dtype)
        c0 += n
    return c0


def _clamp_log2_decay(log2_f):
    return jnp.maximum(log2_f, LOG_DECAY_MIN * LOG2E)


def _in_proj_even_body(x_ref, g_ref, lb_ref, w_ref, a16_ref, a32_ref, g0_ref, g1_ref, g2_ref, u_ref):
    xf = _rms(x_ref[...], g_ref[...])
    xn = xf.astype(BF16)
    lane_tiles = D_MODEL // HD
    for c in range(lane_tiles):
        u_ref[c] = xf[:, c * HD:(c + 1) * HD]

    def class_major(dil):
        n = ROW_TILE // dil
        rows = [jnp.concatenate([u_ref[c, pl.ds(r, n, stride=dil), :] for c in range(lane_tiles)], axis=1)
                for r in range(dil)]
        return jnp.concatenate(rows, axis=0).astype(BF16)

    _store_cols(xn, w_ref, (a16_ref.at[:, :A_W],))
    _store_cols(xn, w_ref, (a16_ref.at[:, A_W:3 * A_W],), c0=3 * A_W)
    c0 = 5 * A_W
    for d in range(2):
        lb = lb_ref[:, d * A_W:(d + 1) * A_W]
        f = lb + (1.0 - lb) * _sigmoid(_dot(xn, w_ref[:, (1 + d) * A_W:(2 + d) * A_W]))
        a32_ref[:, d * A_W:(d + 1) * A_W] = _clamp_log2_decay(jnp.log2(f))
        a16_ref[:, (3 + d) * A_W:(4 + d) * A_W] = (1.0 - f).astype(BF16)
    for (_, dil), o_ref in zip(B_GROUPS, (g0_ref, g1_ref, g2_ref)):
        n = ROW_TILE // dil
        lhs = xn if dil == 1 else class_major(dil)
        for j in range(3):
            u = _dot(lhs, w_ref[:, c0:c0 + A_W])
            c0 += A_W
            if j == 0:
                u = u * (HD ** -0.5)
            for r in range(dil):
                o_ref[:, (3 * r + j) * A_W:(3 * r + j + 1) * A_W] = u[r * n:(r + 1) * n].astype(BF16)


def _in_proj_odd_body(x_ref, g_ref, w_ref, w2_ref, b2_ref, c16_ref, g2_ref):
    xn = _rms(x_ref[...], g_ref[...]).astype(BF16)
    c16_ref[:, :C_KD] = (_dot(xn, w_ref[:, :C_KD]) * (HD ** -0.5)).astype(BF16)
    c0 = _store_cols(xn, w_ref, (c16_ref.at[:, C_KD:],), c0=C_KD)
    lr = _dot(xn, w_ref[:, c0:c0 + LR_PAD]).astype(BF16)
    n = g2_ref.shape[-1]
    for j in range(0, n, 512):
        logit = _dot(lr, w2_ref[:, j:j + 512]) + b2_ref[:, j:j + 512]
        g2_ref[:, j:j + 512] = _clamp_log2_decay(_log2_sigmoid(logit) * (1.0 / GATE_NORMALIZER))


def _in_proj_even(x, g, lb, w, layer):
    m = x.shape[0]
    outs = [((m, 5 * A_W), (ROW_TILE, 5 * A_W), BF16), ((m, 2 * A_W), (ROW_TILE, 2 * A_W), F32)]
    for _, dil in B_GROUPS:
        outs.append(((m // dil, dil * QKV_W), (ROW_TILE // dil, dil * QKV_W), BF16))
    return pl.pallas_call(
        _in_proj_even_body,
        grid=(m // ROW_TILE,),
        in_specs=[pl.BlockSpec((ROW_TILE, D_MODEL), lambda i: (i, 0)),
                  pl.BlockSpec((1, D_MODEL), lambda i: (0, 0)),
                  pl.BlockSpec((1, 2 * A_W), lambda i: (0, 0)),
                  _resident_layer(w, layer)],
        out_specs=[pl.BlockSpec(blk, lambda i: (i, 0)) for _, blk, _ in outs],
        out_shape=[jax.ShapeDtypeStruct(shp, dt) for shp, _, dt in outs],
        scratch_shapes=[pltpu.VMEM((D_MODEL // HD, ROW_TILE, HD), F32)],
        compiler_params=_cparams("parallel"),
        name="in_proj_even",
    )(x, g, lb, w)


def _in_proj_odd(x, g, w, w2, b2, layer):
    m = x.shape[0]
    n16 = 2 * C_KD + 2 * C_VD
    return pl.pallas_call(
        _in_proj_odd_body,
        grid=(m // WIDE_TILE,),
        in_specs=[pl.BlockSpec((WIDE_TILE, D_MODEL), lambda i: (i, 0)),
                  pl.BlockSpec((1, D_MODEL), lambda i: (0, 0)),
                  _resident_layer(w, layer), _resident_layer(w2, layer),
                  pl.BlockSpec((1, 2 * C_KD), lambda i: (0, 0))],
        out_specs=[pl.BlockSpec((WIDE_TILE, n16), lambda i: (i, 0)),
                   pl.BlockSpec((WIDE_TILE, 2 * C_KD), lambda i: (i, 0))],
        out_shape=[jax.ShapeDtypeStruct((m, n16), BF16), jax.ShapeDtypeStruct((m, 2 * C_KD), F32)],
        compiler_params=_cparams("parallel"),
        name="in_proj_odd",
    )(x, g, w, w2, b2)


def _off_ranges(rev):
    out = []
    for j in range(N_SUB):
        lo, hi = (0, SUB * j) if rev else (SUB * (j + 1), CHUNK)
        if hi > lo:
            out.append((j, lo, hi))
    return out


def _gla_constants(rev):
    t = np.arange(CHUNK)
    tri = (t[None, :] >= t[:, None]) if rev else (t[None, :] <= t[:, None])
    causal = tri
    same8 = (t[None, :] // SUB8) == (t[:, None] // SUB8)
    same16 = (t[None, :] // SUB) == (t[:, None] // SUB)
    f = lambda a: jnp.asarray(a.astype(np.float32))
    return (jnp.asarray(np.concatenate([tri, tri], axis=1).astype(np.float32), BF16),
            f(causal), f(same16 & causal), f(same8 & causal), f(same16 & ~same8 & causal))


def _gla_shared_constants():
    e = np.zeros((SUB8 * HD, CHUNK), np.float32)
    for s in range(SUB8):
        e[s * HD:(s + 1) * HD, s::SUB8] = 1.0
    r = np.arange(ROW_TILE)
    sel = [(r[None, :] // n == np.arange(ROW_TILE // n)[:, None]).astype(np.float32) for n in (SUB, CHUNK)]
    return jnp.asarray(e, BF16), jnp.asarray(np.concatenate(sel, axis=0), BF16)


def _rows(ref, r, n):
    return jnp.broadcast_to(ref[pl.ds(r, 1), :], (n, HD))


def _gla_cumsum(g2s, tri_ref):
    g2 = jnp.concatenate(g2s, axis=1)
    g_hi = g2.astype(BF16)
    g_lo = (g2 - g_hi.astype(F32)).astype(BF16)
    b = _dot(tri_ref[...], jnp.concatenate([g_hi, g_lo], axis=0))
    return [b[:, i * HD:(i + 1) * HD] for i in range(len(g2s))]


def _gla_scores(qs, k, v, b, st_ref, b_ref, k_ref, emat_ref, rev, form):
    b_tot = b_ref[pl.ds(0 if rev else CHUNK - 1, 1), :]

    st = st_ref[...]
    qd = qs * jnp.exp2(b)
    kd = (k * jnp.exp2(b_tot - b)).astype(BF16)
    o = _dot_nt(qd.astype(BF16), st.astype(BF16))
    st_ref[...] = st * jnp.exp2(b_tot) + _dot_tn(v, kd)
    if form == "chunk":
        return o, _dot_nt((qd * jnp.exp2(-b_tot)).astype(BF16), kd)

    edge16 = [SUB * i + (0 if rev else SUB - 1) for i in range(N_SUB)]
    e_ko = jnp.exp2(jnp.concatenate([_rows(b_ref, e, SUB) for e in edge16], axis=0) - b)
    k_off = k * e_ko
    zq = jnp.zeros((SUB, HD), F32)
    q_parts, k_parts = [], []
    for j, lo, hi in _off_ranges(rev):
        q_off = qs[lo:hi] * jnp.exp2(jnp.minimum(b[lo:hi] - _rows(b_ref, edge16[j], 1), 0.0))
        q_parts.append(jnp.concatenate([zq] * (lo // SUB) + [q_off] + [zq] * ((CHUNK - hi) // SUB), axis=0))
        k_parts.append(jnp.concatenate([zq] * j + [k_off[SUB * j:SUB * (j + 1)]] + [zq] * (N_SUB - 1 - j),
                                       axis=0))
    a_16 = _dot_nt(jnp.concatenate(q_parts, axis=1).astype(BF16),
                   jnp.concatenate(k_parts, axis=1).astype(BF16))

    if form == "block":
        return o, a_16, _dot_nt((qs * (1.0 / e_ko)).astype(BF16), k_off.astype(BF16))

    edge8 = [SUB * i + (SUB8 if rev else SUB8 - 1) for i in range(N_SUB)]
    d8 = b - jnp.concatenate([_rows(b_ref, e, SUB) for e in edge8], axis=0)
    second = (lax.broadcasted_iota(jnp.int32, (CHUNK, HD), 0) % SUB) >= SUB8
    q_side = jnp.logical_not(second) if rev else second
    w8 = jnp.exp2(jnp.minimum(jnp.where(q_side, d8, -d8), 0.0))
    a_8 = _dot_nt(jnp.where(q_side, qs * w8, 0.0).astype(BF16), jnp.where(q_side, 0.0, k * w8).astype(BF16))

    slabs = []
    for s in range(SUB8):
        rows = [SUB8 * i + s for i in range(CHUNK // SUB8)]
        b_s = jnp.concatenate([_rows(b_ref, r, SUB8) for r in rows], axis=0)
        k_s = jnp.concatenate([_rows(k_ref, r, SUB8) for r in rows], axis=0)
        slabs.append((qs * k_s * jnp.exp2(jnp.minimum(b - b_s, 0.0))).astype(BF16))
    a_dg = _dot(jnp.concatenate(slabs, axis=1), emat_ref[...])
    return o, a_16, a_8, a_dg


def _gla_output(scores, v, m_c_ref, m_16_ref, m_dg_ref, m_8_ref, form):
    if form == "chunk":
        o, a_c = scores
        a = jnp.where(m_c_ref[...] > 0.0, a_c, 0.0)
    elif form == "block":
        o, a_16, a_in = scores
        a = a_in * m_16_ref[...] + a_16
    else:
        o, a_16, a_8, a_dg = scores
        a = a_dg * m_dg_ref[...] + a_8 * m_8_ref[...] + a_16
    return o + _dot(a.astype(BF16), v)


def _decay_ranges_ok(g_ref, sel_ref, n_tiles):
    sums = [_dot(sel_ref[...], g_ref[i * ROW_TILE:(i + 1) * ROW_TILE, :].astype(BF16)) for i in range(n_tiles)]
    low = functools.reduce(jnp.minimum, sums)
    n16 = ROW_TILE // SUB
    return (jnp.min(low[n16:]) > -SAFE_LOG2_RANGE["chunk"], jnp.min(low[:n16]) > -SAFE_LOG2_RANGE["block"])


def _sigmoid(x):
    return 1.0 / (1.0 + jnp.exp(-x))


def _log2_sigmoid(x):
    return jnp.minimum(x, 0.0) * LOG2E - jnp.log2(1.0 + jnp.exp2(jnp.abs(x) * -LOG2E))


def _bidir_body(q_ref, kf_ref, kb_ref, g2f_ref, g2b_ref, v_ref, gate_ref, ng_ref,
                trif_ref, mcf_ref, m16f_ref, mdf_ref, m8f_ref, trib_ref, mcb_ref, m16b_ref, mdb_ref, m8b_ref,
                e_ref, sel_ref,
                o_ref, accf_ref, accb_ref, sf_ref, sb_ref, stage_ref, *, nc):
    def run(form):
        per = CHUNKS_PER_STEP[form]
        sf_ref[...] = jnp.zeros_like(sf_ref)
        sb_ref[...] = jnp.zeros_like(sb_ref)

        def step(c, carry):
            streams = []
            for u0, rev in [(u0, rev) for u0 in range(0, per, CUMSUM_GROUP) for rev in (False, True)]:
                us = range(u0, u0 + CUMSUM_GROUP)
                sls = [pl.ds(pl.multiple_of((nc - 1 - (c * per + u) if rev else c * per + u) * CHUNK, CHUNK), CHUNK)
                       for u in us]
                bs = _gla_cumsum([(g2b_ref if rev else g2f_ref)[sl, :] for sl in sls], trib_ref if rev else trif_ref)
                for u, sl, b in zip(us, sls, bs):
                    b_ref = stage_ref.at[4 * u + 2 * int(rev)]
                    kst_ref = stage_ref.at[4 * u + 2 * int(rev) + 1]
                    k = (kb_ref if rev else kf_ref)[sl, :].astype(F32)
                    b_ref[...] = b
                    if form == "pair":
                        kst_ref[...] = k
                    streams.append((rev, sl, q_ref[sl, :].astype(F32), k, b, b_ref, kst_ref))
            scores = [_gla_scores(qs, k, v_ref[sl, :], b, sb_ref if rev else sf_ref, b_ref, kst_ref, e_ref,
                                  rev, form)
                      for rev, sl, qs, k, b, b_ref, kst_ref in streams]
            for (rev, sl, *_), sc in zip(streams, scores):
                masks = (mcb_ref, m16b_ref, mdb_ref, m8b_ref) if rev else (mcf_ref, m16f_ref, mdf_ref, m8f_ref)
                (accb_ref if rev else accf_ref)[sl, :] = _gla_output(sc, v_ref[sl, :], *masks, form)
            return carry

        lax.fori_loop(0, nc // per, step, 0)

    n_tiles = (nc * CHUNK) // ROW_TILE
    chunk_f, block_f = _decay_ranges_ok(g2f_ref, sel_ref, n_tiles)
    chunk_b, block_b = _decay_ranges_ok(g2b_ref, sel_ref, n_tiles)
    chunk_ok = jnp.logical_and(chunk_f, chunk_b)
    block_ok = jnp.logical_and(jnp.logical_and(block_f, block_b), jnp.logical_not(chunk_ok))
    pl.when(chunk_ok)(functools.partial(run, "chunk"))
    pl.when(block_ok)(functools.partial(run, "block"))
    pl.when(jnp.logical_not(jnp.logical_or(chunk_ok, block_ok)))(functools.partial(run, "pair"))

    ng = ng_ref[...]

    def finish(i, carry):
        sl = pl.ds(pl.multiple_of(i * ROW_TILE, ROW_TILE), ROW_TILE)
        o = accf_ref[sl, :] + accb_ref[sl, :]
        gt = gate_ref[sl, :].astype(F32)
        o_ref[sl, :] = (_rms(o, ng) * (gt * _sigmoid(gt))).astype(o_ref.dtype)
        return carry

    lax.fori_loop(0, (nc * CHUNK) // ROW_TILE, finish, 0)


def _bidir_call(name, bsz, t, dv, args, in_specs):
    consts = (*_gla_constants(False), *_gla_constants(True), *_gla_shared_constants())
    const = lambda shape: pl.BlockSpec(shape, lambda b, h: (0,) * len(shape))
    return pl.pallas_call(
        functools.partial(_bidir_body, nc=t // CHUNK),
        grid=(bsz, HEADS),
        in_specs=in_specs + [const(c.shape) for c in consts],
        out_specs=pl.BlockSpec((None, t, dv), lambda b, h: (b, 0, h)),
        out_shape=jax.ShapeDtypeStruct((bsz, t, HEADS * dv), BF16),
        scratch_shapes=[pltpu.VMEM((t, dv), F32), pltpu.VMEM((t, dv), F32),
                        pltpu.VMEM((dv, HD), F32), pltpu.VMEM((dv, HD), F32)]
                       + [pltpu.VMEM((4 * max(CHUNKS_PER_STEP.values()), CHUNK, HD), F32)],
        compiler_params=_cparams("parallel", "parallel"),
        name=name,
    )(*args, *consts)


def _hgrn_mixer(a16, a32, norm_g):
    bsz, t, _ = a16.shape
    col = lambda off: pl.BlockSpec((None, t, HD), lambda b, h: (b, 0, off * HEADS + h))
    return _bidir_call("hgrn2_bidir", bsz, t, HD,
                       (a16, a16, a16, a32, a32, a16, a16, norm_g),
                       [col(0), col(3), col(4), col(0), col(1), col(1), col(2),
                        pl.BlockSpec((1, HD), lambda b, h: (0, h))])


def _gla_mixer(c16, g2, norm_g):
    bsz, t, _ = c16.shape
    col = lambda off: pl.BlockSpec((None, t, HD), lambda b, h: (b, 0, off * HEADS + h))
    wide = lambda off: pl.BlockSpec((None, t, C_HDV), lambda b, h: (b, 0, off + h))
    v_off = 2 * C_KD // C_HDV
    return _bidir_call("gla_bidir", bsz, t, C_HDV,
                       (c16, c16, c16, g2, g2, c16, c16, norm_g),
                       [col(0), col(1), col(1), col(0), col(1), wide(v_off), wide(v_off + HEADS),
                        pl.BlockSpec((1, C_HDV), lambda b, h: (0, h))])


def _attn_window(length, radius):
    win = min(ATT_Q + 2 * radius, length)
    return win, (0, -radius, ATT_Q - win)


def _attn_bias(gi, length):
    wsize, dil = B_GROUPS[gi]
    radius = wsize // (2 * dil)
    win, offsets = _attn_window(length, radius)
    n = N_GROUPS * HEADS
    slopes = np.array([2.0 ** (-ALIBI_MAX_EXP * (gi * HEADS + h + 1) / n) for h in range(HEADS)])
    dist = np.abs(np.array(offsets)[:, None, None] + np.arange(win)[None, None, :] - np.arange(ATT_Q)[None, :, None])
    bias = np.where(dist[:, None] <= radius, -slopes[None, :, None, None] * (dil * dist[:, None]), NEG_INF)
    return jnp.asarray(bias, F32)


def _attn_body(x_ref, bias_ref, o_ref, l_ref, *, length, radius, classes):
    win, _ = _attn_window(length, radius)
    n_tiles = length // ATT_Q
    per = max(n for n in (4, 2, 1) if n * classes <= ATT_CLASSES and n_tiles % n == 0)

    def step(i, carry):
        probs = []
        for u in range(per):
            tile = i * per + u
            m0 = pl.multiple_of(tile * ATT_Q, ATT_Q)
            k0 = pl.multiple_of(jnp.clip(m0 - radius, 0, length - win), radius)
            place = jnp.where(tile == 0, 0, jnp.where(tile == n_tiles - 1, 2, 1))
            probs += [(m0, k0, place, r, h) for r in range(classes) for h in range(HEADS)]
        col = lambda r, j, h: pl.ds((3 * r + j) * A_W + h * HD, HD)
        s = [_dot_nt(x_ref[pl.ds(m0, ATT_Q), col(r, 0, h)], x_ref[pl.ds(k0, win), col(r, 1, h)])
             for m0, k0, _, r, h in probs]
        s = [x + bias_ref[place, h] for x, (_, _, place, _, h) in zip(s, probs)]
        mx = [jnp.max(x, axis=-1, keepdims=True) for x in s]
        p = [jnp.exp(x - m) for x, m in zip(s, mx)]
        den = [jnp.sum(x, axis=-1, keepdims=True) for x in p]
        o = [_dot(x.astype(BF16), x_ref[pl.ds(k0, win), col(r, 2, h)]) for x, (_, k0, _, r, h) in zip(p, probs)]
        for x, d, m, (m0, _, _, r, h) in zip(o, den, mx, probs):
            o_ref[pl.ds(m0, ATT_Q), pl.ds(r * A_W + h * HD, HD)] = (x * (1.0 / d)).astype(o_ref.dtype)
            l_ref[pl.ds(m0, ATT_Q), pl.ds(r * LSE_W + h * LSE_REP, LSE_REP)] = jnp.broadcast_to(
                m + jnp.log(d), (ATT_Q, LSE_REP))
        return carry

    lax.fori_loop(0, n_tiles // per, step, 0)


def _dilated_attention(qkv, gi):
    bsz, length, _ = qkv.shape
    wsize, dil = B_GROUPS[gi]
    classes = min(dil, ATT_CLASSES)
    bias = _attn_bias(gi, length)
    return pl.pallas_call(
        functools.partial(_attn_body, length=length, radius=wsize // (2 * dil), classes=classes),
        grid=(bsz, dil // classes),
        in_specs=[pl.BlockSpec((None, length, classes * QKV_W), lambda b, r: (b, 0, r)),
                  pl.BlockSpec(bias.shape, lambda b, r: (0, 0, 0, 0))],
        out_specs=[pl.BlockSpec((None, length, classes * A_W), lambda b, r: (b, 0, r)),
                   pl.BlockSpec((None, length, classes * LSE_W), lambda b, r: (b, 0, r))],
        out_shape=[jax.ShapeDtypeStruct((bsz, length, dil * A_W), BF16),
                   jax.ShapeDtypeStruct((bsz, length, dil * LSE_W), F32)],
        compiler_params=_cparams("parallel", "parallel"),
        name=f"dilated_attn_g{gi}",
    )(qkv, bias)


def _lse_expansion():
    e = np.zeros((LSE_W, A_W), np.float32)
    for h in range(HEADS):
        e[h * LSE_REP, h * HD:(h + 1) * HD] = 1.0
    return jnp.asarray(e, BF16)


def _out_even_body(h_ref, a_ref, o0_ref, o1_ref, o2_ref, l0_ref, l1_ref, l2_ref, e_ref, w_ref, y_ref,
                   ob1_ref, ob2_ref, lb1_ref, lb2_ref):
    def natural(o_ref, l_ref, ob_ref, lb_ref, dil):
        if dil == 1:
            return o_ref[...].astype(F32), l_ref[...]
        n = WIDE_TILE // dil
        for r in range(dil):
            for c in range(HEADS):
                col = r * A_W + c * HD
                ob_ref[c, pl.ds(r, n, stride=dil), :] = o_ref[:, col:col + HD].astype(F32)
            lb_ref[pl.ds(r, n, stride=dil), :] = l_ref[:, r * LSE_W:(r + 1) * LSE_W]
        return jnp.concatenate([ob_ref[c] for c in range(HEADS)], axis=1), lb_ref[...]

    parts = [natural(o0_ref, l0_ref, None, None, B_GROUPS[0][1]),
             natural(o1_ref, l1_ref, ob1_ref, lb1_ref, B_GROUPS[1][1]),
             natural(o2_ref, l2_ref, ob2_ref, lb2_ref, B_GROUPS[2][1])]
    lses = [l for _, l in parts]
    mx = jnp.maximum(jnp.maximum(lses[0], lses[1]), lses[2])
    es = [jnp.exp(l - mx) for l in lses]
    inv = 1.0 / (es[0] + es[1] + es[2])
    emat = e_ref[...]
    mixed_b = None
    for e_g, (o_g, _) in zip(es, parts):
        alpha = (e_g * inv).astype(BF16)
        term = _dot(alpha, emat) * o_g
        mixed_b = term if mixed_b is None else mixed_b + term
    y_ref[...] = (h_ref[...] + _dot(a_ref[...], w_ref[:A_W, :]) + _dot(mixed_b.astype(BF16), w_ref[A_W:, :]))


def _out_even(h, out_a, os_, lses, w, layer):
    m = h.shape[0]
    emat = _lse_expansion()
    row = lambda n: pl.BlockSpec((WIDE_TILE, n), lambda i: (i, 0))
    grp = lambda width: [pl.BlockSpec((WIDE_TILE // dil, dil * width), lambda i: (i, 0)) for _, dil in B_GROUPS]
    return pl.pallas_call(
        _out_even_body,
        grid=(m // WIDE_TILE,),
        in_specs=[row(D_MODEL), row(A_W)] + grp(A_W) + grp(LSE_W) + [_resident(emat.shape), _resident_layer(w, layer)],
        out_specs=row(D_MODEL),
        out_shape=jax.ShapeDtypeStruct((m, D_MODEL), F32),
        scratch_shapes=[pltpu.VMEM((HEADS, WIDE_TILE, HD), F32), pltpu.VMEM((HEADS, WIDE_TILE, HD), F32),
                        pltpu.VMEM((WIDE_TILE, LSE_W), F32), pltpu.VMEM((WIDE_TILE, LSE_W), F32)],
        compiler_params=_cparams("parallel"),
        name="out_proj_even",
    )(h, out_a, *os_, *lses, emat, w)


def _out_odd_body(h_hbm, a_hbm, w_ref, y_hbm, *, steps):
    def tile(h_ref, a_ref, y_ref):
        y_ref[...] = h_ref[...] + _dot(a_ref[...], w_ref[...])

    row = lambda n, **kw: pl.BlockSpec((WIDE_TILE, n), lambda i: (i, 0), **kw)
    deep = dict(pipeline_mode=pl.Buffered(STREAM_BUFFERS))
    pltpu.emit_pipeline(tile, grid=(steps,), in_specs=[row(D_MODEL, **deep), row(C_VD, **deep)],
                        out_specs=[row(D_MODEL)])(h_hbm, a_hbm, y_hbm)


def _out_odd(h, mixed, w, layer):
    m = h.shape[0]
    hbm = pl.BlockSpec(memory_space=pl.ANY)
    return pl.pallas_call(
        functools.partial(_out_odd_body, steps=m // WIDE_TILE),
        grid=(1,),
        in_specs=[hbm, hbm, _resident_layer(w, layer)],
        out_specs=hbm,
        out_shape=jax.ShapeDtypeStruct((m, D_MODEL), F32),
        compiler_params=_cparams("arbitrary"),
        name="out_proj_odd",
    )(h, mixed, w)


def _ffn_body(hp_ref, h_ref, hn_ref, p_ref, gf_ref, wup_ref, cw_ref, cb_ref, wdn_ref, gp_ref, wg_ref, wp_ref,
              go_ref, y_ref, xe_ref, act_ref, *, tiles_per_seq, final):
    i = pl.program_id(0) % tiles_per_seq
    gf = gf_ref[...]
    h = h_ref[...]
    keep_prev = jnp.where(i > 0, 1.0, 0.0).astype(F32)
    keep_next = jnp.where(i < tiles_per_seq - 1, 1.0, 0.0).astype(F32)
    xe_ref[0:HALO, :] = _rms(hp_ref[...], gf) * keep_prev
    xe_ref[HALO:HALO + WIDE_TILE, :] = _rms(h, gf)
    xe_ref[HALO + WIDE_TILE:, :] = _rms(hn_ref[...], gf) * keep_next
    xe = xe_ref[...].astype(BF16)
    rows = WIDE_TILE + 2 * HALO

    def conv(u, c):
        w = cw_ref[:, c:c + FF_CHUNK]
        prev = pltpu.roll(u, 1, 0)[HALO:HALO + WIDE_TILE]
        nxt = pltpu.roll(u, rows - 1, 0)[HALO:HALO + WIDE_TILE]
        return (prev * w[0:1] + u[HALO:HALO + WIDE_TILE] * w[1:2] + nxt * w[2:3] + cb_ref[:, c:c + FF_CHUNK])

    for c in range(0, D_FF, FF_CHUNK):
        a = conv(_dot(xe, wup_ref[:, c:c + FF_CHUNK]), c)
        gt = conv(_dot(xe, wup_ref[:, D_FF + c:D_FF + c + FF_CHUNK]), D_FF + c)
        act_ref[:, c:c + FF_CHUNK] = (a * (0.5 * gt * (1.0 + lax.erf(gt * (2.0 ** -0.5))))).astype(BF16)
    h2 = h + _dot(act_ref[...], wdn_ref[...])
    sig = _sigmoid(_dot(_rms(h2, gp_ref[...]).astype(BF16), wg_ref[...]))
    h3 = h2 + sig * _dot(p_ref[...].astype(BF16), wp_ref[...])
    y_ref[...] = _rms(h3, go_ref[...]) if final else h3


def _ffn_ple(h, p, t, gf, wup, cw, cb, wdn, gp, wg, wp, go, layer, final):
    m = h.shape[0]
    per_tile = WIDE_TILE // HALO
    last_blk = m // HALO - 1
    vec = lambda n: pl.BlockSpec((1, n), lambda i: (0, 0))
    return pl.pallas_call(
        functools.partial(_ffn_body, tiles_per_seq=t // WIDE_TILE, final=final),
        grid=(m // WIDE_TILE,),
        in_specs=[pl.BlockSpec((HALO, D_MODEL), lambda i: (jnp.maximum(i * per_tile - 1, 0), 0)),
                  pl.BlockSpec((WIDE_TILE, D_MODEL), lambda i: (i, 0)),
                  pl.BlockSpec((HALO, D_MODEL), lambda i: (jnp.minimum((i + 1) * per_tile, last_blk), 0)),
                  pl.BlockSpec((None, WIDE_TILE, PLE_DIM), lambda i: (layer, i, 0)),
                  vec(D_MODEL), _resident_layer(wup, layer), _resident_layer(cw, layer), vec(2 * D_FF),
                  _resident_layer(wdn, layer), vec(D_MODEL), _resident_layer(wg, layer), _resident_layer(wp, layer),
                  vec(D_MODEL)],
        out_specs=pl.BlockSpec((WIDE_TILE, D_MODEL), lambda i: (i, 0)),
        out_shape=jax.ShapeDtypeStruct((m, D_MODEL), F32),
        scratch_shapes=[pltpu.VMEM((WIDE_TILE + 2 * HALO, D_MODEL), F32), pltpu.VMEM((WIDE_TILE, D_FF), BF16)],
        compiler_params=pltpu.CompilerParams(dimension_semantics=("parallel",), vmem_limit_bytes=VMEM_LIMIT_FFN),
        name="conv_ffn_ple",
    )(h, h, h, p, gf, wup, cw, cb, wdn, gp, wg, wp, go)


def _prep_weights(ev_w_in, od_w_in, gla_w_gate_up, gla_b_gate):
    ev_w = ev_w_in.astype(BF16)
    od_w = jnp.pad(od_w_in, ((0, 0), (0, 0), (0, LR_PAD - 2 * C_RANK))).astype(BF16)
    n_odd = od_w_in.shape[0]
    w2 = jnp.zeros((n_odd, LR_PAD, 2 * C_KD), F32)
    w2 = w2.at[:, :C_RANK, :C_KD].set(gla_w_gate_up[:, 0]).at[:, C_RANK:2 * C_RANK, C_KD:].set(gla_w_gate_up[:, 1])
    b2 = gla_b_gate.reshape(n_odd, 1, 2 * C_KD)
    return ev_w, od_w, w2.astype(BF16), b2


def _trunk(x, p, prm):
    bsz, t, _ = x.shape
    m = bsz * t
    h = x.reshape(m, D_MODEL)
    for l in range(DEPTH):
        g_mix = prm["norm_mix_g"][l][None]
        if l % 2 == 0:
            e = l // 2
            lb = jnp.concatenate([prm["lb"][0, e], prm["lb"][1, e]])[None]
            a16, a32, *qkv = _in_proj_even(h, g_mix, lb, prm["ev_w"], e)
            out_a = _hgrn_mixer(a16.reshape(bsz, t, -1), a32.reshape(bsz, t, -1), prm["hgrn_norm_g"][e][None])
            att = [_dilated_attention(x_g.reshape(bsz, t // dil, dil * QKV_W), gi)
                   for gi, (x_g, (_, dil)) in enumerate(zip(qkv, B_GROUPS))]
            h = _out_even(h, out_a.reshape(m, A_W),
                          [o.reshape(m // dil, dil * A_W) for (o, _), (_, dil) in zip(att, B_GROUPS)],
                          [l_.reshape(m // dil, dil * LSE_W) for (_, l_), (_, dil) in zip(att, B_GROUPS)],
                          prm["ev_w_out"], e)
        else:
            o = l // 2
            c16, lg = _in_proj_odd(h, g_mix, prm["od_w"], prm["od_w2"], prm["od_b2"][o], o)
            mixed = _gla_mixer(c16.reshape(bsz, t, -1), lg.reshape(bsz, t, -1), prm["gla_norm_g"][o][None])
            h = _out_odd(h, mixed.reshape(m, C_VD), prm["od_w_out"], o)
        h = _ffn_ple(h, p.reshape(DEPTH, m, PLE_DIM), t, prm["norm_ffn_g"][l][None], prm["ffn_w_up"],
                     prm["ffn_conv_w"], prm["ffn_conv_b"][l][None], prm["ffn_w_down"],
                     prm["norm_ple_g"][l][None], prm["ple_w_gate"], prm["ple_w_proj"],
                     prm["norm_out_g"][None], layer=l, final=(l == DEPTH - 1))
    return h.reshape(bsz, t, D_MODEL)


def kernel(x_prompt, x_sample, p_prompt, p_sample, norm_mix_g, ev_w_in, hgrn_lb_logits, hgrn_norm_g, ev_w_out, od_w_in, gla_w_gate_up, gla_b_gate, gla_norm_g, od_w_out, norm_ffn_g, ffn_w_up, ffn_conv_w, ffn_conv_b, ffn_w_down, norm_ple_g, ple_w_gate, ple_w_proj, norm_out_g):
    lb = jnp.cumsum(jax.nn.softmax(hgrn_lb_logits.astype(F32), axis=1), axis=1)
    lb = lb - lb[:, :1]
    ev_w, od_w, od_w2, od_b2 = _prep_weights(ev_w_in, od_w_in, gla_w_gate_up, gla_b_gate)
    prm = dict(norm_mix_g=norm_mix_g, ev_w=ev_w, lb=lb, hgrn_norm_g=hgrn_norm_g, ev_w_out=ev_w_out.astype(BF16),
               od_w=od_w, od_w2=od_w2, od_b2=od_b2, gla_norm_g=gla_norm_g, od_w_out=od_w_out.astype(BF16),
               norm_ffn_g=norm_ffn_g, ffn_w_up=ffn_w_up.astype(BF16), ffn_conv_w=ffn_conv_w,
               ffn_conv_b=ffn_conv_b, ffn_w_down=ffn_w_down.astype(BF16), norm_ple_g=norm_ple_g,
               ple_w_gate=ple_w_gate.astype(BF16), ple_w_proj=ple_w_proj.astype(BF16), norm_out_g=norm_out_g)
    return _trunk(x_prompt, p_prompt, prm), _trunk(x_sample, p_sample, prm)
```
